```python
import jax, jax.numpy as jnp
from jax import lax
import numpy as np

D_MODEL = 1024
BATCH = 8
SEQ = 2048
DEPTH = 1
DEC_BATCH = 128
DEC_SEQ = 1
PAST_LEN = 16384
PAGE_SIZE = 128

SSM_EXPAND = 2
D_INNER = SSM_EXPAND * D_MODEL
SSM_HEADDIM = 64
SSM_HEADS = D_INNER // SSM_HEADDIM
SSM_GROUPS = 8
SSM_HPG = SSM_HEADS // SSM_GROUPS
SSM_STATE = 128
CONV_WIDTH = 4
CONV_DIM = D_INNER + 2 * SSM_GROUPS * SSM_STATE
SSD_CHUNK = 128
RWKV_DIM = D_MODEL
RWKV_HEADSIZE = 64
RWKV_HEADS = RWKV_DIM // RWKV_HEADSIZE
DECAY_LORA = 64
AAA_LORA = 64
GATE_LORA = 128
SHIFT_DIM = 3 * RWKV_DIM + DECAY_LORA + AAA_LORA + GATE_LORA
LN_X_EPS = 64e-5
IN_DIM = D_INNER + CONV_DIM + SSM_HEADS + SHIFT_DIM + 2 * D_MODEL
PEER_HEADS = 8
N_KEYS = 128
N_EXPERTS = N_KEYS * N_KEYS
PEER_QDIM = 256
PEER_TOPK = 16
PEER_BLOCK = 128
PLE_DIM = 256
EPS = 1e-6

kernel_name = 'hybrid_ssd_rwkv7_peer_step'


def rmsnorm(x, g):
    xf = x.astype(jnp.float32)
    y = xf * lax.rsqrt(jnp.mean(xf * xf, axis=-1, keepdims=True) + EPS)
    return (y * g.astype(jnp.float32)).astype(x.dtype)


def grouped_rmsnorm(x, g, n_groups):
    shp = x.shape
    xf = x.astype(jnp.float32).reshape(shp[:-1] + (n_groups, shp[-1] // n_groups))
    y = xf * lax.rsqrt(jnp.mean(xf * xf, axis=-1, keepdims=True) + EPS)
    return (y.reshape(shp) * g.astype(jnp.float32)).astype(x.dtype)


def split_proj(u):
    sizes = (D_INNER, CONV_DIM, SSM_HEADS, SHIFT_DIM, D_MODEL, D_MODEL)
    return jnp.split(u, np.cumsum(sizes)[:-1].tolist(), axis=-1)


def causal_conv(xbc, conv_prev, w, b):
    l = xbc.shape[1]
    full = jnp.concatenate([conv_prev, xbc], axis=1)
    out = b + sum(full[:, k:k + l] * w[k] for k in range(CONV_WIDTH))
    return jax.nn.silu(out), full[:, -(CONV_WIDTH - 1):]


def ssd_scan(xh, dt, a, bm, cm, h0, chunk):
    b, l = xh.shape[:2]
    c = l // chunk
    rs = lambda t: t.reshape((b, c, chunk) + t.shape[2:])
    xh, dt, bm, cm = rs(xh), rs(dt), rs(bm), rs(cm)
    a_cum = jnp.cumsum(dt * a, axis=2)
    seg = a_cum[:, :, :, None] - a_cum[:, :, None, :]
    causal = jnp.tril(jnp.ones((chunk, chunk), bool))[:, :, None, None]
    lmat = jnp.exp(jnp.where(causal, seg, -jnp.inf))
    cb = jnp.einsum('bclgn,bcsgn->bclsg', cm, bm)
    m = cb[..., None] * lmat * dt[:, :, None]
    y_diag = jnp.einsum('bclsgr,bcsgrp->bclgrp', m, xh)
    decay_out = jnp.exp(a_cum[:, :, -1:] - a_cum)
    states = jnp.einsum('bcsgn,bcsgr,bcsgrp->bcgrpn', bm, decay_out * dt, xh)
    chunk_decay = jnp.exp(a_cum[:, :, -1])

    def step(h, inp):
        st, dec = inp
        return h * dec[..., None, None] + st, h

    h_last, h_prev = lax.scan(step, h0.astype(states.dtype),
                              (jnp.moveaxis(states, 1, 0), jnp.moveaxis(chunk_decay, 1, 0)))
    h_prev = jnp.moveaxis(h_prev, 0, 1)
    y_off = jnp.einsum('bclgn,bcgrpn,bclgr->bclgrp', cm, h_prev, jnp.exp(a_cum))
    y = (y_diag + y_off).reshape((b, l) + y_diag.shape[3:])
    return y, h_last


def mamba_branch(z, xbc_raw, dt_raw, conv_prev, ssm_prev, conv_w, conv_b, dt_bias, a_log,
                 d_skip, ssm_norm, w_out_ssm):
    b, l, _ = z.shape
    xbc, conv_new = causal_conv(xbc_raw, conv_prev, conv_w, conv_b)
    xs, bm, cm = jnp.split(xbc, [D_INNER, D_INNER + SSM_GROUPS * SSM_STATE], axis=-1)
    xh = xs.reshape(b, l, SSM_GROUPS, SSM_HPG, SSM_HEADDIM)
    bm = bm.reshape(b, l, SSM_GROUPS, SSM_STATE)
    cm = cm.reshape(b, l, SSM_GROUPS, SSM_STATE)
    dt = jax.nn.softplus(dt_raw + dt_bias).reshape(b, l, SSM_GROUPS, SSM_HPG)
    a = -jnp.exp(a_log).reshape(SSM_GROUPS, SSM_HPG)
    h0 = ssm_prev.reshape(b, SSM_GROUPS, SSM_HPG, SSM_HEADDIM, SSM_STATE)
    chunk = SSD_CHUNK if l % SSD_CHUNK == 0 else l
    y, h_last = ssd_scan(xh, dt, a, bm, cm, h0, chunk)
    y = (y + xh * d_skip.reshape(SSM_GROUPS, SSM_HPG, 1)).reshape(b, l, D_INNER)
    y = grouped_rmsnorm(y * jax.nn.silu(z), ssm_norm, SSM_GROUPS)
    return y @ w_out_ssm, conv_new, h_last.reshape(b, SSM_HEADS, SSM_HEADDIM, SSM_STATE)


def wkv_scan(r, decay, k, v, a_vec, b_vec, s0):
    def step(s, inp):
        rt, wt, kt, vt, at, bt = inp
        s = (s * wt[:, :, None, :]
             + jnp.einsum('bhvk,bhk->bhv', s, at)[..., None] * bt[:, :, None, :]
             + vt[..., None] * kt[:, :, None, :])
        return s, jnp.einsum('bhvk,bhk->bhv', s, rt)

    seqs = tuple(jnp.moveaxis(t, 1, 0) for t in (r, decay, k, v, a_vec, b_vec))
    s_last, ys = lax.scan(step, s0.astype(r.dtype), seqs)
    return jnp.moveaxis(ys, 0, 1), s_last


def rwkv_branch(proj, shift_prev, wkv_prev, shift_mu, decay_w0, decay_w2, aaa_a0, aaa_a2,
                gate_g2, k_k, k_a, r_k, lnx_g, lnx_b, w_out_rwkv):
    b, l, _ = proj.shape
    prev = jnp.concatenate([shift_prev[:, None, :], proj[:, :-1]], axis=1)
    xs = proj + (prev - proj) * shift_mu
    offs = np.cumsum((RWKV_DIM, RWKV_DIM, RWKV_DIM, DECAY_LORA, AAA_LORA)).tolist()
    r, k, v, xw, xa, xg = jnp.split(xs, offs, axis=-1)
    w = -jax.nn.softplus(-(decay_w0 + jnp.tanh(xw) @ decay_w2)) - 0.5
    a = jax.nn.sigmoid(aaa_a0 + xa @ aaa_a2)
    g = jax.nn.sigmoid(xg) @ gate_g2
    hs = lambda t: t.reshape(b, l, RWKV_HEADS, RWKV_HEADSIZE)
    r, w, k, v, a = hs(r), hs(w), hs(k), hs(v), hs(a)
    kkf = (k * k_k).astype(jnp.float32)
    kk = (kkf * lax.rsqrt(jnp.sum(kkf * kkf, axis=-1, keepdims=True) + 1e-12)).astype(k.dtype)
    k = k * (1 + (a - 1) * k_a)
    decay = jnp.exp(-jnp.exp(w))
    y, s_last = wkv_scan(r, decay, k, v, -kk, kk * a, wkv_prev)
    yf = y.astype(jnp.float32)
    mu = jnp.mean(yf, axis=-1, keepdims=True)
    var = jnp.mean(jnp.square(yf - mu), axis=-1, keepdims=True)
    yn = ((yf - mu) * lax.rsqrt(var + LN_X_EPS)).reshape(b, l, RWKV_DIM)
    y = (yn * lnx_g.astype(jnp.float32) + lnx_b.astype(jnp.float32)).astype(proj.dtype)
    bonus = (jnp.sum(r * k * r_k, axis=-1, keepdims=True) * v).reshape(b, l, RWKV_DIM)
    return ((y + bonus) * g) @ w_out_rwkv, s_last, proj[:, -1]


def peer(h, peer_wq, peer_k1, peer_k2, peer_u, peer_v):
    b, l, d = h.shape
    t = b * l
    nblk = -(-t // PEER_BLOCK)
    hp = jnp.pad(h.reshape(t, d), ((0, nblk * PEER_BLOCK - t), (0, 0))).reshape(nblk, PEER_BLOCK, d)

    def block(xb):
        q = (xb @ peer_wq).reshape(PEER_BLOCK, PEER_HEADS, 2, PEER_QDIM // 2)
        s1 = jnp.einsum('thd,hnd->thn', q[:, :, 0], peer_k1)
        s2 = jnp.einsum('thd,hnd->thn', q[:, :, 1], peer_k2)
        v1, i1 = lax.top_k(s1, PEER_TOPK)
        v2, i2 = lax.top_k(s2, PEER_TOPK)
        cand = (v1[..., :, None] + v2[..., None, :]).reshape(PEER_BLOCK, PEER_HEADS, -1)
        cidx = (i1[..., :, None] * N_KEYS + i2[..., None, :]).reshape(PEER_BLOCK, PEER_HEADS, -1)
        top, pos = lax.top_k(cand, PEER_TOPK)
        idx = jnp.take_along_axis(cidx, pos, axis=-1)
        gate = jax.nn.softmax(top.astype(jnp.float32), axis=-1).astype(xb.dtype)
        act = jax.nn.gelu(jnp.einsum('thkd,td->thk', peer_u[idx], xb), approximate=False)
        return jnp.einsum('thk,thkd->td', gate * act, peer_v[idx])

    out = lax.map(block, hp).reshape(-1, d)[:t]
    return out.reshape(b, l, d)


def layer(x, p, conv_prev, ssm_prev, wkv_prev, shift_prev, lp):
    h = rmsnorm(x, lp['norm_mix'])
    z, xbc_raw, dt_raw, rw_proj, gate_m, gate_r = split_proj(h @ lp['w_in'])
    y_m, conv_new, ssm_new = mamba_branch(z, xbc_raw, dt_raw, conv_prev, ssm_prev, lp['conv_w'],
                                          lp['conv_b'], lp['dt_bias'], lp['a_log'], lp['d_skip'],
                                          lp['ssm_norm'], lp['w_out_ssm'])
    y_r, wkv_new, shift_new = rwkv_branch(rw_proj, shift_prev, wkv_prev, lp['shift_mu'],
                                          lp['decay_w0'], lp['decay_w2'], lp['aaa_a0'],
                                          lp['aaa_a2'], lp['gate_g2'], lp['k_k'], lp['k_a'],
                                          lp['r_k'], lp['lnx_g'], lp['lnx_b'], lp['w_out_rwkv'])
    x = x + (jax.nn.sigmoid(gate_m) * y_m + jax.nn.sigmoid(gate_r) * y_r) @ lp['w_out']
    x = x + peer(rmsnorm(x, lp['norm_ffn']), lp['peer_wq'], lp['peer_k1'], lp['peer_k2'],
                 lp['peer_u'], lp['peer_v'])
    x = x + jax.nn.sigmoid(rmsnorm(x, lp['norm_ple']) @ lp['w_ple_gate']) * (p @ lp['w_ple_proj'])
    return x, ssm_new, conv_new, wkv_new, shift_new


def setup_inputs(seed: int = 0) -> dict:
    key = jax.random.key(seed)
    ks = list(jax.random.split(key, 48))

    def nrm(shape, scale):
        return jax.random.normal(ks.pop(), shape, jnp.float32) * scale

    def unif(shape, lo, hi):
        return jax.random.uniform(ks.pop(), shape, jnp.float32, lo, hi)

    L = DEPTH
    dt0 = jnp.exp(unif((L, SSM_HEADS), float(np.log(1e-3)), float(np.log(1e-1))))
    return {
        'x_prompt': nrm((BATCH, SEQ, D_MODEL), 1.0),
        'x_sample': nrm((DEC_BATCH, DEC_SEQ, D_MODEL), 1.0),
        'p_prompt': nrm((L, BATCH, SEQ, PLE_DIM), 1.0),
        'p_sample': nrm((L, DEC_BATCH, DEC_SEQ, PLE_DIM), 1.0),
        'state_ssm': nrm((L, DEC_BATCH, SSM_HEADS, SSM_HEADDIM, SSM_STATE), 0.1),
        'state_conv': nrm((L, DEC_BATCH, CONV_WIDTH - 1, CONV_DIM), 1.0),
        'state_wkv': nrm((L, DEC_BATCH, RWKV_HEADS, RWKV_HEADSIZE, RWKV_HEADSIZE), 0.1),
        'state_shift': nrm((L, DEC_BATCH, SHIFT_DIM), 1.0),
        'norm_mix': 1.0 + nrm((L, D_MODEL), 0.02),
        'w_in': nrm((L, D_MODEL, IN_DIM), D_MODEL ** -0.5),
        'conv_w': nrm((L, CONV_WIDTH, CONV_DIM), CONV_WIDTH ** -0.5),
        'conv_b': nrm((L, CONV_DIM), 0.02),
        'dt_bias': dt0 + jnp.log(-jnp.expm1(-dt0)),
        'a_log': jnp.log(unif((L, SSM_HEADS), 1.0, 16.0)),
        'd_skip': 1.0 + nrm((L, SSM_HEADS), 0.02),
        'ssm_norm': 1.0 + nrm((L, D_INNER), 0.02),
        'w_out_ssm': nrm((L, D_INNER, D_MODEL), D_INNER ** -0.5),
        'shift_mu': unif((L, SHIFT_DIM), 0.0, 1.0),
        'decay_w0': unif((L, RWKV_DIM), -6.0, -1.0),
        'decay_w2': nrm((L, DECAY_LORA, RWKV_DIM), 0.5 * DECAY_LORA ** -0.5),
        'aaa_a0': nrm((L, RWKV_DIM), 0.1),
        'aaa_a2': nrm((L, AAA_LORA, RWKV_DIM), AAA_LORA ** -0.5),
        'gate_g2': nrm((L, GATE_LORA, RWKV_DIM), GATE_LORA ** -0.5),
        'k_k': 0.85 + nrm((L, RWKV_HEADS, RWKV_HEADSIZE), 0.02),
        'k_a': 1.0 + nrm((L, RWKV_HEADS, RWKV_HEADSIZE), 0.02),
        'r_k': nrm((L, RWKV_HEADS, RWKV_HEADSIZE), 0.1),
        'lnx_g': 1.0 + nrm((L, RWKV_DIM), 0.02),
        'lnx_b': nrm((L, RWKV_DIM), 0.02),
        'w_out_rwkv': nrm((L, RWKV_DIM, D_MODEL), RWKV_DIM ** -0.5),
        'w_out': nrm((L, D_MODEL, D_MODEL), D_MODEL ** -0.5),
        'norm_ffn': 1.0 + nrm((L, D_MODEL), 0.02),
        'peer_wq': nrm((L, D_MODEL, PEER_HEADS * PEER_QDIM), D_MODEL ** -0.5),
        'peer_k1': nrm((L, PEER_HEADS, N_KEYS, PEER_QDIM // 2), (PEER_QDIM // 2) ** -0.5),
        'peer_k2': nrm((L, PEER_HEADS, N_KEYS, PEER_QDIM // 2), (PEER_QDIM // 2) ** -0.5),
        'peer_u': nrm((L, N_EXPERTS, D_MODEL), D_MODEL ** -0.5),
        'peer_v': nrm((L, N_EXPERTS, D_MODEL), 0.2),
        'norm_ple': 1.0 + nrm((L, D_MODEL), 0.02),
        'w_ple_gate': nrm((L, D_MODEL, D_MODEL), D_MODEL ** -0.5),
        'w_ple_proj': nrm((L, PLE_DIM, D_MODEL), PLE_DIM ** -0.5),
        'norm_final': 1.0 + nrm((D_MODEL,), 0.02),
    }


def reference(x_prompt, x_sample, p_prompt, p_sample, state_ssm, state_conv, state_wkv,
              state_shift, norm_mix, w_in, conv_w, conv_b, dt_bias, a_log, d_skip, ssm_norm,
              w_out_ssm, shift_mu, decay_w0, decay_w2, aaa_a0, aaa_a2, gate_g2, k_k, k_a, r_k,
              lnx_g, lnx_b, w_out_rwkv, w_out, norm_ffn, peer_wq, peer_k1, peer_k2, peer_u,
              peer_v, norm_ple, w_ple_gate, w_ple_proj, norm_final):
    bp = x_prompt.shape[0]
    dt_ = x_prompt.dtype
    xp, xs = x_prompt, x_sample
    ssm_p, conv_p, wkv_p, shift_p = [], [], [], []
    ssm_s, conv_s, wkv_s, shift_s = [], [], [], []
    for i in range(DEPTH):
        lp = {
            'norm_mix': norm_mix[i], 'w_in': w_in[i], 'conv_w': conv_w[i], 'conv_b': conv_b[i],
            'dt_bias': dt_bias[i], 'a_log': a_log[i], 'd_skip': d_skip[i],
            'ssm_norm': ssm_norm[i], 'w_out_ssm': w_out_ssm[i], 'shift_mu': shift_mu[i],
            'decay_w0': decay_w0[i], 'decay_w2': decay_w2[i], 'aaa_a0': aaa_a0[i],
            'aaa_a2': aaa_a2[i], 'gate_g2': gate_g2[i], 'k_k': k_k[i], 'k_a': k_a[i],
            'r_k': r_k[i], 'lnx_g': lnx_g[i], 'lnx_b': lnx_b[i], 'w_out_rwkv': w_out_rwkv[i],
            'w_out': w_out[i], 'norm_ffn': norm_ffn[i], 'peer_wq': peer_wq[i],
            'peer_k1': peer_k1[i], 'peer_k2': peer_k2[i], 'peer_u': peer_u[i],
            'peer_v': peer_v[i], 'norm_ple': norm_ple[i], 'w_ple_gate': w_ple_gate[i],
            'w_ple_proj': w_ple_proj[i],
        }
        xp, s1, s2, s3, s4 = layer(
            xp, p_prompt[i],
            jnp.zeros((bp, CONV_WIDTH - 1, CONV_DIM), dt_),
            jnp.zeros((bp, SSM_HEADS, SSM_HEADDIM, SSM_STATE), dt_),
            jnp.zeros((bp, RWKV_HEADS, RWKV_HEADSIZE, RWKV_HEADSIZE), dt_),
            jnp.zeros((bp, SHIFT_DIM), dt_), lp)
        ssm_p.append(s1); conv_p.append(s2); wkv_p.append(s3); shift_p.append(s4)
        xs, t1, t2, t3, t4 = layer(xs, p_sample[i], state_conv[i], state_ssm[i], state_wkv[i],
                                   state_shift[i], lp)
        ssm_s.append(t1); conv_s.append(t2); wkv_s.append(t3); shift_s.append(t4)
    y_prompt = rmsnorm(xp, norm_final)
    y_sample = rmsnorm(xs, norm_final)
    return (y_prompt, y_sample, jnp.stack(ssm_p), jnp.stack(conv_p), jnp.stack(wkv_p),
            jnp.stack(shift_p), jnp.stack(ssm_s), jnp.stack(conv_s), jnp.stack(wkv_s),
            jnp.stack(shift_s))
```

```python
import functools

import numpy as np
import jax
import jax.numpy as jnp
from jax import lax
from jax.experimental import pallas as pl
from jax.experimental.pallas import tpu as pltpu

F32 = jnp.float32
BF16 = jnp.bfloat16

EPS = 1e-6
LN_X_EPS = 64e-5
HEAD = 64
SSM_STATE = 128
SSM_GROUPS = 8
CONV_WIDTH = 4
CHUNK = 128
LANES = 128
VMEM_LIMIT = 56 * 1024 * 1024


def _cparams(n_axes):
    return pltpu.CompilerParams(dimension_semantics=("arbitrary",) * n_axes,
                                vmem_limit_bytes=VMEM_LIMIT)


def _dot(a, b):
    return jnp.dot(a, b, preferred_element_type=F32)


def _dot_nt(a, b):
    return lax.dot_general(a, b, (((1,), (1,)), ((), ())), preferred_element_type=F32)


def _dot_tn(a, b):
    return lax.dot_general(a, b, (((0,), (0,)), ((), ())), preferred_element_type=F32)


def _split2(x):
    hi = x.astype(BF16)
    lo = (x - hi.astype(F32)).astype(BF16)
    return hi, lo


def _split3(x):
    x1 = x.astype(BF16)
    r = x - x1.astype(F32)
    x2 = r.astype(BF16)
    x3 = (r - x2.astype(F32)).astype(BF16)
    return x1, x2, x3


def _dot_hi(a, b, dot=_dot):
    a1, a2 = _split2(a)
    b1, b2 = _split2(b)
    return dot(a1, b1) + (dot(a1, b2) + dot(a2, b1))


def _dot_exact_rhs(a, e):
    a1, a2, a3 = _split3(a)
    return _dot(a1, e) + (_dot(a2, e) + _dot(a3, e))


def _dot_exact_lhs(e, a):
    a1, a2, a3 = _split3(a)
    return _dot(e, a1) + (_dot(e, a2) + _dot(e, a3))


def _rms(x, g):
    return x * lax.rsqrt(jnp.mean(x * x, axis=-1, keepdims=True) + EPS) * g


def _sigmoid(x):
    return 1.0 / (1.0 + jnp.exp(-x))


def _softplus(x):
    return jnp.maximum(x, 0.0) + jnp.log1p(jnp.exp(-jnp.abs(x)))


def _tril(n, k=0, dtype=F32):
    r = lax.broadcasted_iota(jnp.int32, (n, n), 0)
    c = lax.broadcasted_iota(jnp.int32, (n, n), 1)
    return (c <= r + k)


def _norm_matmul_kernel(x_ref, g_ref, w_ref, o_ref, h_ref):
    @pl.when(pl.program_id(1) == 0)
    def _():
        h_ref[...] = _rms(x_ref[...], g_ref[...]).astype(BF16)

    o_ref[...] = _dot(h_ref[...], w_ref[...])


def norm_matmul(x, g, w, tm, tn):
    m, k = x.shape
    n = w.shape[1]
    tm = min(tm, m)
    assert m % tm == 0 and n % tn == 0, (m, tm, n, tn)
    return pl.pallas_call(
        _norm_matmul_kernel,
        grid=(m // tm, n // tn),
        in_specs=[pl.BlockSpec((tm, k), lambda i, j: (i, 0)),
                  pl.BlockSpec((1, k), lambda i, j: (0, 0)),
                  pl.BlockSpec((k, tn), lambda i, j: (0, j))],
        out_specs=pl.BlockSpec((tm, tn), lambda i, j: (i, j)),
        out_shape=jax.ShapeDtypeStruct((m, n), F32),
        scratch_shapes=[pltpu.VMEM((tm, k), BF16)],
        compiler_params=_cparams(2),
        name="norm_matmul",
    )(x, g, w)


def _ssd_chunk_kernel(xbc_ref, z_ref, dt_ref, convw_ref, convb_ref, dtb_ref, alog_ref, dskip_ref,
                      normg_ref, e_ref, et_ref, y_ref, h_ref, ext_ref, act_ref, *, n_heads):
    c = pl.program_id(1)
    C = xbc_ref.shape[1]
    d_inner = n_heads * HEAD
    gw = d_inner // SSM_GROUPS
    hpg = n_heads // SSM_GROUPS

    @pl.when(c == 0)
    def _():
        ext_ref[0:8, :] = jnp.zeros((8, ext_ref.shape[1]), F32)
        h_ref[...] = jnp.zeros(h_ref.shape, F32)

    @pl.when(c > 0)
    def _():
        ext_ref[0:8, :] = ext_ref[C:C + 8, :]

    ext_ref[8:C + 8, :] = xbc_ref[0]
    conv = convb_ref[...]
    for k in range(CONV_WIDTH):
        off = 8 - (CONV_WIDTH - 1) + k
        conv = conv + ext_ref[off:off + C, :] * convw_ref[k:k + 1, :]
    act_ref[...] = conv * _sigmoid(conv)

    dt = _softplus(dt_ref[0] + dtb_ref[...])
    a = -jnp.exp(alog_ref[...])
    tril = _tril(C)
    acum = _dot_exact_lhs(tril.astype(BF16), dt * a)
    acum_t = acum.T
    dt_t = dt.T
    e = e_ref[...]
    eacum_x = _dot_exact_rhs(jnp.exp(acum), e)
    wdec_x = _dot_exact_rhs(jnp.exp(acum[C - 1:C, :] - acum) * dt, e)
    dec_b = jnp.broadcast_to(jnp.exp(acum_t[:, C - 1:C]), (LANES, SSM_STATE))
    lane = lax.broadcasted_iota(jnp.int32, (C, gw), 1)

    for g in range(SSM_GROUPS):
        cols = slice(g * gw, (g + 1) * gw)
        xg = act_ref[:, cols]
        bg = act_ref[:, d_inner + g * SSM_STATE:d_inner + (g + 1) * SSM_STATE].astype(BF16)
        cg = act_ref[:, d_inner + (SSM_GROUPS + g) * SSM_STATE:
                     d_inner + (SSM_GROUPS + g + 1) * SSM_STATE].astype(BF16)
        cb = _dot_nt(cg, bg)
        y = xg * dskip_ref[:, cols]
        for r in range(hpg):
            h = g * hpg + r
            seg = acum[:, h:h + 1] - acum_t[h:h + 1, :]
            m = cb * jnp.exp(jnp.where(tril, seg, -jnp.inf)) * dt_t[h:h + 1, :]
            xm = jnp.where((lane >= r * HEAD) & (lane < (r + 1) * HEAD), xg, 0.0)
            y = y + _dot(m.astype(BF16), xm.astype(BF16))
        hg = h_ref[0, cols, :]
        y = y + _dot_nt(cg, hg.astype(BF16)) * eacum_x[:, cols]
        zg = z_ref[0, :, cols]
        y = y * (zg * _sigmoid(zg))
        y = _rms(y, normg_ref[:, cols])
        y_ref[0, :, cols] = y.astype(y_ref.dtype)
        dec = _dot_exact_lhs(et_ref[cols, :], dec_b)
        h_ref[0, cols, :] = dec * hg + _dot_tn((xg * wdec_x[:, cols]).astype(BF16), bg)


def ssd_prompt(u, conv_w, conv_b, dt_bias, a_log, d_skip, ssm_norm, n_heads):
    b, l, _ = u.shape
    d_inner = n_heads * HEAD
    conv_dim = d_inner + 2 * SSM_GROUPS * SSM_STATE
    C = CHUNK
    assert l % C == 0 and conv_dim % d_inner == 0
    pad = lambda v: jnp.pad(v.reshape(1, -1), ((0, 0), (0, LANES - v.size)))
    e = (np.arange(d_inner)[None, :] // HEAD == np.arange(LANES)[:, None])
    e_bf = jnp.asarray(e, BF16)
    et_bf = jnp.asarray(e.T, BF16)
    full = lambda shape: pl.BlockSpec(shape, lambda i, j: (0,) * len(shape))
    return pl.pallas_call(
        functools.partial(_ssd_chunk_kernel, n_heads=n_heads),
        grid=(b, l // C),
        in_specs=[pl.BlockSpec((1, C, conv_dim), lambda i, j: (i, j, 0)),
                  pl.BlockSpec((1, C, d_inner), lambda i, j: (i, j, conv_dim // d_inner)),
                  pl.BlockSpec((1, C, LANES), lambda i, j: (i, j, (conv_dim + d_inner) // LANES)),
                  full((CONV_WIDTH, conv_dim)), full((1, conv_dim)), full((1, LANES)), full((1, LANES)),
                  full((1, d_inner)), full((1, d_inner)), full((LANES, d_inner)), full((d_inner, LANES))],
        out_specs=[pl.BlockSpec((1, C, d_inner), lambda i, j: (i, j, 0)),
                   pl.BlockSpec((1, d_inner, SSM_STATE), lambda i, j: (i, 0, 0))],
        out_shape=[jax.ShapeDtypeStruct((b, l, d_inner), BF16),
                   jax.ShapeDtypeStruct((b, d_inner, SSM_STATE), F32)],
        scratch_shapes=[pltpu.VMEM((C + 8, conv_dim), F32), pltpu.VMEM((C, conv_dim), F32)],
        compiler_params=_cparams(2),
        name="ssd_prompt",
    )(u, u, u, conv_w, conv_b.reshape(1, -1), pad(dt_bias), pad(a_log),
      jnp.repeat(d_skip, HEAD).reshape(1, -1), ssm_norm.reshape(1, -1), e_bf, et_bf)


def _head_masks(shape):
    lane = lax.broadcasted_iota(jnp.int32, shape, 1)
    return lane < HEAD, lane >= HEAD


def _rwkv_features(proj, prev, mu_ref, w0_ref, w2_ref, a0_ref, a2_ref, g2_ref, kk_ref, ka_ref, dim):
    xs = proj + (prev - proj) * mu_ref[...]
    k = xs[:, dim:2 * dim]
    t_wa = xs[:, 3 * dim:3 * dim + LANES]
    xg = xs[:, 3 * dim + LANES:3 * dim + 2 * LANES]
    w = -_softplus(-(w0_ref[...] + _dot(jnp.tanh(t_wa).astype(BF16), w2_ref[...]))) - 0.5
    a = _sigmoid(a0_ref[...] + _dot(t_wa.astype(BF16), a2_ref[...]))
    g = _dot(_sigmoid(xg).astype(BF16), g2_ref[...])
    return (xs[:, 0:dim], k * (1.0 + (a - 1.0) * ka_ref[...]), xs[:, 2 * dim:3 * dim], k * kk_ref[...], a,
            -jnp.exp(w), g)


def _rwkv_chunk_kernel(p_ref, mu_ref, w0_ref, w2_ref, a0_ref, a2_ref, g2_ref, kk_ref, ka_ref, rk_ref,
                       lng_ref, lnb_ref, ones2_ref, o_ref, s_ref,
                       ext_ref, r_s, k_s, v_s, kkn_s, a_s, lw_s, g_s, *, dim):
    c = pl.program_id(1)
    C = p_ref.shape[1]
    n_pairs = dim // LANES

    @pl.when(c == 0)
    def _():
        ext_ref[0:8, :] = jnp.zeros((8, ext_ref.shape[1]), F32)
        s_ref[...] = jnp.zeros(s_ref.shape, F32)

    @pl.when(c > 0)
    def _():
        ext_ref[0:8, :] = ext_ref[C:C + 8, :]

    proj = p_ref[0]
    ext_ref[8:C + 8, :] = proj
    (r_s[...], k_s[...], v_s[...], kkn_s[...], a_s[...], lw_s[...], g_s[...]) = _rwkv_features(
        proj, ext_ref[7:C + 7, :], mu_ref, w0_ref, w2_ref, a0_ref, a2_ref, g2_ref, kk_ref, ka_ref, dim)

    tril = _tril(C)
    tril_strict = _tril(C, -1)
    tril_bf = tril.astype(BF16)
    ones2 = ones2_ref[...]
    m0, m1 = _head_masks((C, LANES))
    row = lax.broadcasted_iota(jnp.int32, (LANES, LANES), 0)
    col = lax.broadcasted_iota(jnp.int32, (LANES, LANES), 1)
    blockdiag = (row < HEAD) == (col < HEAD)
    n_levels = C.bit_length() - 1
    assert 1 << n_levels == C

    def by_head(x):
        return jnp.concatenate([jnp.where(m0, x, 0.0), jnp.where(m1, x, 0.0)], axis=0)

    for j in range(n_pairs):
        cols = slice(j * LANES, (j + 1) * LANES)
        lw = lw_s[:, cols]
        logp = _dot_exact_lhs(tril_bf, lw)
        logpc = logp[C - 1:C, :]
        ep, epinv = jnp.exp(logp), jnp.exp(-logp)
        epc = jnp.exp(logpc - logp)
        kkf = kkn_s[:, cols]
        kk = kkf * lax.rsqrt(_dot_exact_rhs(kkf * kkf, ones2) + 1e-12)
        aj, kj, vj, rj = a_s[:, cols], k_s[:, cols], v_s[:, cols], r_s[:, cols]
        bvec = kk * aj
        ar = jnp.concatenate([-kk * jnp.exp(logp - lw), rj * ep], axis=0)
        bk = jnp.concatenate([bvec * epinv, kj * epinv], axis=0)
        s = s_ref[0, j]
        as_ = _dot_nt(ar.astype(BF16), s.astype(BF16))
        mab, mak, qab, qak = [], [], [], []
        for hm in (m0, m1):
            gm = _dot_hi(jnp.where(jnp.concatenate([hm, hm], axis=0), ar, 0.0), bk, _dot_nt)
            mab.append(jnp.where(tril_strict, gm[0:C, 0:C], 0.0))
            mak.append(jnp.where(tril_strict, gm[0:C, C:2 * C], 0.0))
            qab.append(jnp.where(tril, gm[C:2 * C, 0:C], 0.0))
            qak.append(jnp.where(tril, gm[C:2 * C, C:2 * C], 0.0))
        vs = by_head(vj)
        x = as_[0:C] + _dot_hi(jnp.concatenate(mak, axis=1), vs)
        n0, n1 = mab
        for lvl in range(n_levels):
            x = x + _dot_hi(jnp.concatenate([n0, n1], axis=1), by_head(x))
            if lvl + 1 < n_levels:
                n0, n1 = _dot_hi(n0, n0), _dot_hi(n1, n1)
        y = (as_[C:2 * C] + _dot(jnp.concatenate(qab, axis=1).astype(BF16), by_head(x).astype(BF16))
             + _dot(jnp.concatenate(qak, axis=1).astype(BF16), vs.astype(BF16)))
        upd = _dot_tn(jnp.concatenate([x, vj], axis=0).astype(BF16),
                      jnp.concatenate([bvec * epc, kj * epc], axis=0).astype(BF16))
        s_ref[0, j] = jnp.exp(logpc) * s + jnp.where(blockdiag, upd, 0.0)

        mean = _dot_exact_rhs(y, ones2) * (1.0 / HEAD)
        d = y - mean
        var = _dot_exact_rhs(d * d, ones2) * (1.0 / HEAD)
        yn = d * lax.rsqrt(var + LN_X_EPS) * lng_ref[:, cols] + lnb_ref[:, cols]
        bonus = _dot_exact_rhs(rj * kj * rk_ref[:, cols], ones2) * vj
        o_ref[0, :, cols] = ((yn + bonus) * g_s[:, cols]).astype(o_ref.dtype)


def rwkv_prompt(proj, shift_mu, decay_w0, decay_w2, aaa_a0, aaa_a2, gate_g2, k_k, k_a, r_k, lnx_g, lnx_b):
    b, l, sd = proj.shape
    dim = decay_w0.size
    n_heads = dim // HEAD
    dl, al, gl = decay_w2.shape[0], aaa_a2.shape[0], gate_g2.shape[0]
    assert dl + al == LANES and gl == LANES and sd == 3 * dim + 2 * LANES and l % CHUNK == 0
    C = CHUNK
    w2 = jnp.concatenate([decay_w2, jnp.zeros((al, dim), F32)], axis=0).astype(BF16)
    a2 = jnp.concatenate([jnp.zeros((dl, dim), F32), aaa_a2], axis=0).astype(BF16)
    ones2 = jnp.asarray(np.arange(LANES)[:, None] // HEAD == np.arange(LANES)[None, :] // HEAD, BF16)
    row = lambda v: v.reshape(1, -1)
    full = lambda shape: pl.BlockSpec(shape, lambda i, j: (0,) * len(shape))
    vec = full((1, dim))
    y, s = pl.pallas_call(
        functools.partial(_rwkv_chunk_kernel, dim=dim),
        grid=(b, l // C),
        in_specs=[pl.BlockSpec((1, C, sd), lambda i, j: (i, j, 0)), full((1, sd)), vec, full((LANES, dim)), vec,
                  full((LANES, dim)), full((LANES, dim)), vec, vec, vec, vec, vec, full((LANES, LANES))],
        out_specs=[pl.BlockSpec((1, C, dim), lambda i, j: (i, j, 0)),
                   pl.BlockSpec((1, dim // LANES, LANES, LANES), lambda i, j: (i, 0, 0, 0))],
        out_shape=[jax.ShapeDtypeStruct((b, l, dim), BF16),
                   jax.ShapeDtypeStruct((b, dim // LANES, LANES, LANES), F32)],
        scratch_shapes=[pltpu.VMEM((C + 8, sd), F32)] + [pltpu.VMEM((C, dim), F32)] * 7,
        compiler_params=_cparams(2),
        name="rwkv_prompt",
    )(proj, row(shift_mu), row(decay_w0), w2, row(aaa_a0), a2, gate_g2.astype(BF16), row(k_k), row(k_a),
      row(r_k), row(lnx_g), row(lnx_b), ones2)
    s = s.reshape(b, dim // LANES, 2, HEAD, 2, HEAD)
    s = jnp.stack([s[:, :, 0, :, 0, :], s[:, :, 1, :, 1, :]], axis=2).reshape(b, n_heads, HEAD, HEAD)
    return y, s


def _mix_kernel(x_ref, ym_ref, yr_ref, gate_ref, wm_ref, wr_ref, wo_ref, nf_ref, wq_ref, x2_ref, q_ref):
    d = x_ref.shape[1]
    y_m = _dot(ym_ref[...].astype(BF16), wm_ref[...])
    y_r = _dot(yr_ref[...].astype(BF16), wr_ref[...])
    mix = _sigmoid(gate_ref[:, 0:d]) * y_m + _sigmoid(gate_ref[:, d:2 * d]) * y_r
    x2 = x_ref[...] + _dot(mix.astype(BF16), wo_ref[...])
    x2_ref[...] = x2
    q_ref[...] = _dot(_rms(x2, nf_ref[...]).astype(BF16), wq_ref[...])


def mix_and_query(x, ym, yr, gates, w_out_ssm, w_out_rwkv, w_out, norm_ffn, peer_wq, tm):
    m, d = x.shape
    tm = min(tm, m)
    assert m % tm == 0
    rows = lambda n: pl.BlockSpec((tm, n), lambda i: (i, 0))
    full = lambda a: pl.BlockSpec(a.shape, lambda i: (0, 0))
    nq = peer_wq.shape[1]
    args = (x, ym, yr, gates, w_out_ssm, w_out_rwkv, w_out, norm_ffn.reshape(1, -1), peer_wq)
    return pl.pallas_call(
        _mix_kernel,
        grid=(m // tm,),
        in_specs=[rows(d), rows(ym.shape[1]), rows(yr.shape[1]), rows(2 * d)] + [full(a) for a in args[4:]],
        out_specs=[rows(d), rows(nq)],
        out_shape=[jax.ShapeDtypeStruct((m, d), F32), jax.ShapeDtypeStruct((m, nq), F32)],
        compiler_params=_cparams(1),
        name="mix_and_query",
    )(*args)


PEER_TOPK = 16
N_KEYS = 128


def _erf_gelu(x):
    return 0.5 * x * (1.0 + lax.erf(x * np.float32(1.0 / np.sqrt(2.0))))


def _kth_largest(s, k, collect=False):
    tops = []
    for _ in range(k):
        m = jnp.max(s, axis=0, keepdims=True)
        tops.append(m)
        s = jnp.where(s == m, -jnp.inf, s)
    return tops


def _peer_kernel(x2_ref, q_ref, p_ref, k1_ref, k2_ref, u_ref, vt_ref, nf_ref, npl_ref, wg_ref, wp_ref, nfin_ref,
                 y_ref, hb_s, s1_s, s2_s, c_s, d_s, th_s, act_s, w_s, acc_s, *, n_heads):
    e = pl.program_id(1)
    n_e = pl.num_programs(1)
    tt = x2_ref.shape[0]
    ec = u_ref.shape[0]
    qd = N_KEYS

    @pl.when(e == 0)
    def _():
        hb_s[...] = _rms(x2_ref[...], nf_ref[...]).astype(BF16)
        acc_s[...] = jnp.zeros(acc_s.shape, F32)
        for h in range(n_heads):
            for ts in range(tt // LANES):
                tok = slice(ts * LANES, (ts + 1) * LANES)
                q1 = q_ref[tok, (2 * h) * qd:(2 * h + 1) * qd].astype(BF16)
                q2 = q_ref[tok, (2 * h + 1) * qd:(2 * h + 2) * qd].astype(BF16)
                s1 = _dot_nt(k1_ref[h], q1)
                s2 = _dot_nt(k2_ref[h], q2)
                v1 = _kth_largest(s1, PEER_TOPK)
                v2 = jnp.concatenate(_kth_largest(s2, PEER_TOPK), axis=0)
                cand = jnp.concatenate([v + v2 for v in v1], axis=0)
                top = _kth_largest(cand, PEER_TOPK)
                z = sum(jnp.exp(t - top[0]) for t in top)
                s1_s[h, :, tok] = s1
                s2_s[h, :, tok] = s2
                c_s[h, :, tok] = jnp.exp(s1 - v1[0]) / z
                d_s[h, :, tok] = jnp.exp(s2 - v2[0:1])
                th_s[h, :, tok] = jnp.broadcast_to(top[PEER_TOPK - 1], (8, LANES))

    act_s[...] = _dot_nt(u_ref[...], hb_s[...])

    assert ec == 8 * N_KEYS
    i_rows = pl.ds(pl.multiple_of(e * 8, 8), 8)

    def tile(ts, carry):
        tok = pl.ds(pl.multiple_of(ts * LANES, LANES), LANES)
        for il in range(8):
            rows = slice(il * N_KEYS, (il + 1) * N_KEYS)
            w = jnp.zeros((N_KEYS, LANES), F32)
            for h in range(n_heads):
                s1_row = s1_s[h, i_rows, tok][il:il + 1]
                c_row = c_s[h, i_rows, tok][il:il + 1]
                sel = (s1_row + s2_s[h, :, tok]) >= th_s[h, 0:1, tok]
                w = w + jnp.where(sel, d_s[h, :, tok], 0.0) * c_row
            w_s[rows, tok] = (w * _erf_gelu(act_s[rows, tok])).astype(BF16)
        return carry

    lax.fori_loop(0, tt // LANES, tile, 0)
    acc_s[...] += _dot(vt_ref[...], w_s[...])

    @pl.when(e == n_e - 1)
    def _():
        x3 = x2_ref[...] + acc_s[...].T
        gate = _sigmoid(_dot(_rms(x3, npl_ref[...]).astype(BF16), wg_ref[...]))
        x4 = x3 + gate * _dot(p_ref[...].astype(BF16), wp_ref[...])
        y_ref[...] = _rms(x4, nfin_ref[...])


def peer_ple_final(x2, q, p, peer_k1, peer_k2, peer_u, peer_vt, norm_ffn, norm_ple, w_ple_gate, w_ple_proj,
                   norm_final, tt, ec):
    t, d = x2.shape
    n_heads = peer_k1.shape[0]
    n_exp = peer_u.shape[0]
    tt = min(tt, t)
    assert t % tt == 0 and n_exp % ec == 0 and tt % LANES == 0 and ec % N_KEYS == 0
    assert peer_k1.shape[1:] == (N_KEYS, N_KEYS) and n_exp == N_KEYS * N_KEYS
    rows = lambda n: pl.BlockSpec((tt, n), lambda i, j: (i, 0))
    full = lambda a: pl.BlockSpec(a.shape, lambda i, j: (0,) * a.ndim)
    row = lambda v: v.reshape(1, -1)
    consts = (row(norm_ffn), row(norm_ple), w_ple_gate, w_ple_proj, row(norm_final))
    hs = pltpu.VMEM((n_heads, N_KEYS, tt), F32)
    return pl.pallas_call(
        functools.partial(_peer_kernel, n_heads=n_heads),
        grid=(t // tt, n_exp // ec),
        in_specs=[rows(d), rows(q.shape[1]), rows(p.shape[1]), full(peer_k1), full(peer_k2),
                  pl.BlockSpec((ec, d), lambda i, j: (j, 0)), pl.BlockSpec((d, ec), lambda i, j: (0, j))]
                 + [full(a) for a in consts],
        out_specs=rows(d),
        out_shape=jax.ShapeDtypeStruct((t, d), F32),
        scratch_shapes=[pltpu.VMEM((tt, d), BF16), hs, hs, hs, hs, pltpu.VMEM((n_heads, 8, tt), F32),
                        pltpu.VMEM((ec, tt), F32), pltpu.VMEM((ec, tt), BF16), pltpu.VMEM((d, tt), F32)],
        compiler_params=_cparams(2),
        name="peer_ple_final",
    )(x2, q, p, peer_k1, peer_k2, peer_u, peer_vt, *consts)


def _as_column(x_row):
    n = x_row.shape[1]
    eye = lax.broadcasted_iota(jnp.int32, (n, n), 0) == lax.broadcasted_iota(jnp.int32, (n, n), 1)
    return jnp.sum(jnp.where(eye, jnp.broadcast_to(x_row, (n, n)), 0.0), axis=1, keepdims=True)


def _rows8(x_row):
    return jnp.broadcast_to(x_row, (8, x_row.shape[1]))


def _ssd_step_kernel(xbc_ref, z_ref, dt_ref, conv_ref, h_ref, convw_ref, convb_ref, dtb_ref, alog_ref, dskip_ref,
                     normg_ref, e_ref, y_ref, ho_ref, *, n_heads):
    d_inner = n_heads * HEAD
    gw = d_inner // SSM_GROUPS
    cs = conv_ref[0]
    conv = convb_ref[...] + xbc_ref[0] * convw_ref[CONV_WIDTH - 1:CONV_WIDTH, :]
    for k in range(CONV_WIDTH - 1):
        conv = conv + cs[k:k + 1, :] * convw_ref[k:k + 1, :]
    act = conv * _sigmoid(conv)
    dt = _softplus(dt_ref[0] + dtb_ref[...])
    da = jnp.exp(dt * -jnp.exp(alog_ref[...]))
    e = e_ref[...]
    dt_x = _dot_exact_rhs(_rows8(dt), e)[0:1]
    da_x = _dot_exact_rhs(_rows8(da), e)[0:1]
    for g in range(SSM_GROUPS):
        cols = slice(g * gw, (g + 1) * gw)
        xg = act[:, cols]
        bg = act[:, d_inner + g * SSM_STATE:d_inner + (g + 1) * SSM_STATE]
        cg = act[:, d_inner + (SSM_GROUPS + g) * SSM_STATE:d_inner + (SSM_GROUPS + g + 1) * SSM_STATE]
        hn = _as_column(da_x[:, cols]) * h_ref[0, cols, :] + _as_column(xg * dt_x[:, cols]) * bg
        ho_ref[0, cols, :] = hn
        y = _dot_nt(_rows8(cg).astype(BF16), hn.astype(BF16))[0:1] + xg * dskip_ref[:, cols]
        zg = z_ref[0, :, cols]
        y = y * (zg * _sigmoid(zg))
        y_ref[0, :, cols] = _rms(y, normg_ref[:, cols])


def ssd_step(u, state_conv, state_ssm, conv_w, conv_b, dt_bias, a_log, d_skip, ssm_norm, n_heads):
    b = u.shape[0]
    d_inner = n_heads * HEAD
    conv_dim = d_inner + 2 * SSM_GROUPS * SSM_STATE
    pad = lambda v: jnp.pad(v.reshape(1, -1), ((0, 0), (0, LANES - v.size)))
    e_bf = jnp.asarray(np.arange(d_inner)[None, :] // HEAD == np.arange(LANES)[:, None], BF16)
    full = lambda shape: pl.BlockSpec(shape, lambda i: (0,) * len(shape))
    u3 = u.reshape(b, 1, -1)
    y, h = pl.pallas_call(
        functools.partial(_ssd_step_kernel, n_heads=n_heads),
        grid=(b,),
        in_specs=[pl.BlockSpec((1, 1, conv_dim), lambda i: (i, 0, 0)),
                  pl.BlockSpec((1, 1, d_inner), lambda i: (i, 0, conv_dim // d_inner)),
                  pl.BlockSpec((1, 1, LANES), lambda i: (i, 0, (conv_dim + d_inner) // LANES)),
                  pl.BlockSpec((1, CONV_WIDTH - 1, conv_dim), lambda i: (i, 0, 0)),
                  pl.BlockSpec((1, d_inner, SSM_STATE), lambda i: (i, 0, 0)),
                  full((CONV_WIDTH, conv_dim)), full((1, conv_dim)), full((1, LANES)), full((1, LANES)),
                  full((1, d_inner)), full((1, d_inner)), full((LANES, d_inner))],
        out_specs=[pl.BlockSpec((1, 1, d_inner), lambda i: (i, 0, 0)),
                   pl.BlockSpec((1, d_inner, SSM_STATE), lambda i: (i, 0, 0))],
        out_shape=[jax.ShapeDtypeStruct((b, 1, d_inner), F32),
                   jax.ShapeDtypeStruct((b, d_inner, SSM_STATE), F32)],
        compiler_params=_cparams(1),
        name="ssd_step",
    )(u3, u3, u3, state_conv, state_ssm.reshape(b, d_inner, SSM_STATE), conv_w, conv_b.reshape(1, -1),
      pad(dt_bias), pad(a_log), jnp.repeat(d_skip, HEAD).reshape(1, -1), ssm_norm.reshape(1, -1), e_bf)
    return y.reshape(b, d_inner), h


def _rwkv_step_features_kernel(p_ref, prev_ref, mu_ref, w0_ref, w2_ref, a0_ref, a2_ref, g2_ref, kk_ref, ka_ref,
                               ones2_ref, o_ref, *, dim):
    r, k, v, kkf, a, lw, g = _rwkv_features(p_ref[...], prev_ref[...], mu_ref, w0_ref, w2_ref, a0_ref, a2_ref,
                                            g2_ref, kk_ref, ka_ref, dim)
    for j in range(dim // LANES):
        cols = slice(j * LANES, (j + 1) * LANES)
        kj = kkf[:, cols]
        o_ref[3, :, cols] = kj * lax.rsqrt(_dot_exact_rhs(kj * kj, ones2_ref[...]) + 1e-12)
    o_ref[0], o_ref[1], o_ref[2], o_ref[4], o_ref[5], o_ref[6] = r, k, v, a, jnp.exp(lw), g


def _rwkv_step_kernel(f_ref, s_ref, rk_ref, lng_ref, lnb_ref, y_ref, so_ref, y_s):
    n_heads = s_ref.shape[1]
    eye = lax.broadcasted_iota(jnp.int32, (HEAD, HEAD), 0) == lax.broadcasted_iota(jnp.int32, (HEAD, HEAD), 1)
    for h in range(n_heads):
        row = lambda i: f_ref[i, 0, h:h + 1, :]
        r, k, v, kk, a, w = (row(i) for i in range(6))
        s = s_ref[0, h]
        sa = jnp.sum(s * -kk, axis=1, keepdims=True)
        v_col = jnp.sum(jnp.where(eye, jnp.broadcast_to(v, (HEAD, HEAD)), 0.0), axis=1, keepdims=True)
        sn = s * w + sa * (kk * a) + v_col * k
        so_ref[0, h] = sn
        y_s[h:h + 1, :] = _dot_nt(_rows8(r).astype(BF16), sn.astype(BF16))[0:1]
    y = y_s[...]
    r, k, v, g = f_ref[0, 0], f_ref[1, 0], f_ref[2, 0], f_ref[6, 0]
    d = y - jnp.mean(y, axis=-1, keepdims=True)
    var = jnp.mean(d * d, axis=-1, keepdims=True)
    yn = d * lax.rsqrt(var + LN_X_EPS) * lng_ref[...] + lnb_ref[...]
    bonus = jnp.sum(r * k * rk_ref[...], axis=-1, keepdims=True) * v
    y_ref[0] = (yn + bonus) * g


def rwkv_step(proj, shift_prev, state_wkv, shift_mu, decay_w0, decay_w2, aaa_a0, aaa_a2, gate_g2, k_k, k_a, r_k,
              lnx_g, lnx_b):
    b, sd = proj.shape
    dim = decay_w0.size
    n_heads = dim // HEAD
    dl, al, gl = decay_w2.shape[0], aaa_a2.shape[0], gate_g2.shape[0]
    assert dl + al == LANES and gl == LANES and sd == 3 * dim + 2 * LANES
    w2 = jnp.concatenate([decay_w2, jnp.zeros((al, dim), F32)], axis=0).astype(BF16)
    a2 = jnp.concatenate([jnp.zeros((dl, dim), F32), aaa_a2], axis=0).astype(BF16)
    ones2 = jnp.asarray(np.arange(LANES)[:, None] // HEAD == np.arange(LANES)[None, :] // HEAD, BF16)
    row = lambda v: v.reshape(1, -1)
    args = (proj, shift_prev, row(shift_mu), row(decay_w0), w2, row(aaa_a0), a2, gate_g2.astype(BF16), row(k_k),
            row(k_a), ones2)
    feats = pl.pallas_call(
        functools.partial(_rwkv_step_features_kernel, dim=dim),
        grid=(1,),
        in_specs=[pl.BlockSpec(a.shape, lambda i: (0, 0)) for a in args],
        out_specs=pl.BlockSpec((7, b, dim), lambda i: (0, 0, 0)),
        out_shape=jax.ShapeDtypeStruct((7, b, dim), F32),
        compiler_params=_cparams(1),
        name="rwkv_step_features",
    )(*args)
    hv = lambda v: v.reshape(n_heads, HEAD)
    full = pl.BlockSpec((n_heads, HEAD), lambda i: (0, 0))
    y, s = pl.pallas_call(
        _rwkv_step_kernel,
        grid=(b,),
        in_specs=[pl.BlockSpec((7, 1, n_heads, HEAD), lambda i: (0, i, 0, 0)),
                  pl.BlockSpec((1, n_heads, HEAD, HEAD), lambda i: (i, 0, 0, 0)), full, full, full],
        out_specs=[pl.BlockSpec((1, n_heads, HEAD), lambda i: (i, 0, 0)),
                   pl.BlockSpec((1, n_heads, HEAD, HEAD), lambda i: (i, 0, 0, 0))],
        out_shape=[jax.ShapeDtypeStruct((b, n_heads, HEAD), F32),
                   jax.ShapeDtypeStruct((b, n_heads, HEAD, HEAD), F32)],
        scratch_shapes=[pltpu.VMEM((n_heads, HEAD), F32)],
        compiler_params=_cparams(1),
        name="rwkv_step",
    )(feats.reshape(7, b, n_heads, HEAD), state_wkv, hv(r_k), hv(lnx_g), hv(lnx_b))
    return y.reshape(b, dim), s


def _layer(x, p, states, wts, n_ssm_heads):
    b, l, d = x.shape
    xt = x.reshape(b * l, d)
    g_mix = wts['norm_mix'].reshape(1, -1)
    tm = 1024
    u_ssm = norm_matmul(xt, g_mix, wts['w_ssm'], tm, wts['w_ssm'].shape[1] // 7)
    u_rwkv = norm_matmul(xt, g_mix, wts['w_shift'], tm, wts['w_shift'].shape[1] // 2)
    u_gate = norm_matmul(xt, g_mix, wts['w_gates'], tm, wts['w_gates'].shape[1] // 2)
    d_inner = n_ssm_heads * HEAD
    conv_dim = d_inner + 2 * SSM_GROUPS * SSM_STATE
    ssd_w = (wts['conv_w'], wts['conv_b'], wts['dt_bias'], wts['a_log'], wts['d_skip'], wts['ssm_norm'])
    rwkv_w = tuple(wts[k] for k in ('shift_mu', 'decay_w0', 'decay_w2', 'aaa_a0', 'aaa_a2', 'gate_g2', 'k_k', 'k_a',
                                    'r_k', 'lnx_g', 'lnx_b'))
    if states is None:
        ym, ssm_new = ssd_prompt(u_ssm.reshape(b, l, -1), *ssd_w, n_ssm_heads)
        ym = ym.reshape(b * l, d_inner)
        conv_new = u_ssm.reshape(b, l, -1)[:, l - (CONV_WIDTH - 1):, :conv_dim]
        yr, wkv_new = rwkv_prompt(u_rwkv.reshape(b, l, -1), *rwkv_w)
        yr = yr.reshape(b * l, -1)
        shift_new = u_rwkv.reshape(b, l, -1)[:, l - 1]
    else:
        conv_prev, ssm_prev, wkv_prev, shift_prev = states
        ym, ssm_new = ssd_step(u_ssm, conv_prev, ssm_prev, *ssd_w, n_ssm_heads)
        conv_new = jnp.concatenate([conv_prev[:, 1:], u_ssm[:, None, :conv_dim]], axis=1)
        yr, wkv_new = rwkv_step(u_rwkv, shift_prev, wkv_prev, *rwkv_w)
        shift_new = u_rwkv
    x2, q = mix_and_query(xt, ym, yr, u_gate, wts['w_out_ssm'], wts['w_out_rwkv'], wts['w_out'], wts['norm_ffn'],
                          wts['peer_wq'], 512)
    y = peer_ple_final(x2, q, p.reshape(b * l, -1), wts['peer_k1'], wts['peer_k2'], wts['peer_u'], wts['peer_vt'],
                       wts['norm_ffn'], wts['norm_ple'], wts['w_ple_gate'], wts['w_ple_proj'], wts['norm_final'],
                       512, 8 * N_KEYS)
    return (y.reshape(b, l, d), ssm_new.reshape(b, n_ssm_heads, HEAD, SSM_STATE), conv_new, wkv_new, shift_new)


def kernel(x_prompt, x_sample, p_prompt, p_sample, state_ssm, state_conv, state_wkv, state_shift, norm_mix, w_in,
           conv_w, conv_b, dt_bias, a_log, d_skip, ssm_norm, w_out_ssm, shift_mu, decay_w0, decay_w2, aaa_a0, aaa_a2,
           gate_g2, k_k, k_a, r_k, lnx_g, lnx_b, w_out_rwkv, w_out, norm_ffn, peer_wq, peer_k1, peer_k2, peer_u,
           peer_v, norm_ple, w_ple_gate, w_ple_proj, norm_final):
    depth = w_in.shape[0]
    assert depth == 1, "single-layer trunk"
    d_model = x_prompt.shape[-1]
    n_ssm_heads = dt_bias.shape[1]
    d_inner = n_ssm_heads * HEAD
    conv_dim = conv_w.shape[2]
    shift_dim = shift_mu.shape[1]
    bf = lambda a: a.astype(BF16)
    o = np.cumsum([0, d_inner, conv_dim, n_ssm_heads, shift_dim, d_model, d_model])
    wi = w_in[0]
    wts = {
        'w_ssm': bf(jnp.concatenate([wi[:, o[1]:o[2]], wi[:, o[0]:o[1]], wi[:, o[2]:o[3]],
                                     jnp.zeros((d_model, LANES - n_ssm_heads), F32)], axis=1)),
        'w_shift': bf(wi[:, o[3]:o[4]]),
        'w_gates': bf(wi[:, o[4]:o[6]]),
        'w_out_ssm': bf(w_out_ssm[0]), 'w_out_rwkv': bf(w_out_rwkv[0]), 'w_out': bf(w_out[0]),
        'peer_wq': bf(peer_wq[0]), 'peer_k1': bf(peer_k1[0]), 'peer_k2': bf(peer_k2[0]),
        'peer_u': bf(peer_u[0]), 'peer_vt': bf(peer_v[0]).T,
        'w_ple_gate': bf(w_ple_gate[0]), 'w_ple_proj': bf(w_ple_proj[0]), 'norm_final': norm_final,
    }
    for name, val in (('norm_mix', norm_mix), ('conv_w', conv_w), ('conv_b', conv_b), ('dt_bias', dt_bias),
                      ('a_log', a_log), ('d_skip', d_skip), ('ssm_norm', ssm_norm), ('shift_mu', shift_mu),
                      ('decay_w0', decay_w0), ('decay_w2', decay_w2), ('aaa_a0', aaa_a0), ('aaa_a2', aaa_a2),
                      ('gate_g2', gate_g2), ('k_k', k_k), ('k_a', k_a), ('r_k', r_k), ('lnx_g', lnx_g),
                      ('lnx_b', lnx_b), ('norm_ffn', norm_ffn), ('norm_ple', norm_ple)):
        wts[name] = val[0]
    yp, ssm_p, conv_p, wkv_p, shift_p = _layer(x_prompt, p_prompt[0], None, wts, n_ssm_heads)
    ys, ssm_s, conv_s, wkv_s, shift_s = _layer(
        x_sample, p_sample[0], (state_conv[0], state_ssm[0], state_wkv[0], state_shift[0]), wts, n_ssm_heads)
    return (yp, ys, ssm_p[None], conv_p[None], wkv_p[None], shift_p[None],
            ssm_s[None], conv_s[None], wkv_s[None], shift_s[None])
```

```python
import functools

import numpy as np
import jax
import jax.numpy as jnp
from jax import lax
from jax.experimental import pallas as pl
from jax.experimental.pallas import tpu as pltpu

F32 = jnp.float32
BF16 = jnp.bfloat16

EPS = 1e-6
LN_X_EPS = 64e-5
HEAD = 64
SSM_STATE = 128
SSM_GROUPS = 8
CONV_WIDTH = 4
CHUNK = 128
HI_LEVELS = 2
LANES = 128
VMEM_LIMIT = 56 * 1024 * 1024


def _cparams(n_axes):
    return pltpu.CompilerParams(dimension_semantics=("arbitrary",) * n_axes,
                                vmem_limit_bytes=VMEM_LIMIT)


def _dot(a, b):
    return jnp.dot(a, b, preferred_element_type=F32)


def _dot_nt(a, b):
    return lax.dot_general(a, b, (((1,), (1,)), ((), ())), preferred_element_type=F32)


def _dot_tn(a, b):
    return lax.dot_general(a, b, (((0,), (0,)), ((), ())), preferred_element_type=F32)


def _split2(x):
    hi = x.astype(BF16)
    lo = (x - hi.astype(F32)).astype(BF16)
    return hi, lo


def _split3(x):
    x1 = x.astype(BF16)
    r = x - x1.astype(F32)
    x2 = r.astype(BF16)
    x3 = (r - x2.astype(F32)).astype(BF16)
    return x1, x2, x3


def _dot_hi(a, b, dot=_dot):
    a1, a2 = _split2(a)
    b1, b2 = _split2(b)
    return dot(a1, b1) + (dot(a1, b2) + dot(a2, b1))


def _dot_exact_rhs(a, e, passes=3):
    if passes == 2:
        a1, a2 = _split2(a)
        return _dot(a1, e) + _dot(a2, e)
    a1, a2, a3 = _split3(a)
    return _dot(a1, e) + (_dot(a2, e) + _dot(a3, e))


def _dot_exact_lhs(e, a):
    a1, a2, a3 = _split3(a)
    return _dot(e, a1) + (_dot(e, a2) + _dot(e, a3))


def _rms(x, g):
    return x * lax.rsqrt(jnp.mean(x * x, axis=-1, keepdims=True) + EPS) * g


def _sigmoid(x):
    return 1.0 / (1.0 + jnp.exp(-x))


def _softplus(x):
    return jnp.maximum(x, 0.0) + jnp.log1p(jnp.exp(-jnp.abs(x)))


def _tril(n, k=0, dtype=F32):
    r = lax.broadcasted_iota(jnp.int32, (n, n), 0)
    c = lax.broadcasted_iota(jnp.int32, (n, n), 1)
    return (c <= r + k)


def _norm_matmul_kernel(x_ref, g_ref, w_ref, o_ref, h_ref):
    @pl.when(pl.program_id(1) == 0)
    def _():
        h_ref[...] = _rms(x_ref[...], g_ref[...]).astype(BF16)

    o_ref[...] = _dot(h_ref[...], w_ref[...])


def norm_matmul(x, g, w, tm, tn):
    m, k = x.shape
    n = w.shape[1]
    tm = min(tm, m)
    assert m % tm == 0 and n % tn == 0, (m, tm, n, tn)
    return pl.pallas_call(
        _norm_matmul_kernel,
        grid=(m // tm, n // tn),
        in_specs=[pl.BlockSpec((tm, k), lambda i, j: (i, 0)),
                  pl.BlockSpec((1, k), lambda i, j: (0, 0)),
                  pl.BlockSpec((k, tn), lambda i, j: (0, j))],
        out_specs=pl.BlockSpec((tm, tn), lambda i, j: (i, j)),
        out_shape=jax.ShapeDtypeStruct((m, n), F32),
        scratch_shapes=[pltpu.VMEM((tm, k), BF16)],
        compiler_params=_cparams(2),
        name="norm_matmul",
    )(x, g, w)


def _ssd_chunk_kernel(xbc_ref, z_ref, dt_ref, convw_ref, convb_ref, dtb_ref, alog_ref, dskip_ref,
                      normg_ref, e_ref, et_ref, y_ref, h_ref, ext_ref, act_ref, *, n_heads):
    c = pl.program_id(1)
    C = xbc_ref.shape[1]
    d_inner = n_heads * HEAD
    gw = d_inner // SSM_GROUPS
    hpg = n_heads // SSM_GROUPS

    @pl.when(c == 0)
    def _():
        ext_ref[0:8, :] = jnp.zeros((8, ext_ref.shape[1]), F32)
        h_ref[...] = jnp.zeros(h_ref.shape, F32)

    @pl.when(c > 0)
    def _():
        ext_ref[0:8, :] = ext_ref[C:C + 8, :]

    ext_ref[8:C + 8, :] = xbc_ref[0]
    conv = convb_ref[...]
    for k in range(CONV_WIDTH):
        off = 8 - (CONV_WIDTH - 1) + k
        conv = conv + ext_ref[off:off + C, :] * convw_ref[k:k + 1, :]
    act_ref[...] = conv * _sigmoid(conv)

    dt = _softplus(dt_ref[0] + dtb_ref[...])
    a = -jnp.exp(alog_ref[...])
    tril = _tril(C)
    acum = _dot_exact_lhs(tril.astype(BF16), dt * a)
    acum_t = acum.T
    dt_t = dt.T
    e = e_ref[...]
    eacum_x = _dot_exact_rhs(jnp.exp(acum), e)
    wdec_x = _dot_exact_rhs(jnp.exp(acum[C - 1:C, :] - acum) * dt, e)
    dec_b = jnp.broadcast_to(jnp.exp(acum_t[:, C - 1:C]), (LANES, SSM_STATE))
    lane = lax.broadcasted_iota(jnp.int32, (C, gw), 1)

    for g in range(SSM_GROUPS):
        cols = slice(g * gw, (g + 1) * gw)
        xg = act_ref[:, cols]
        bg = act_ref[:, d_inner + g * SSM_STATE:d_inner + (g + 1) * SSM_STATE].astype(BF16)
        cg = act_ref[:, d_inner + (SSM_GROUPS + g) * SSM_STATE:
                     d_inner + (SSM_GROUPS + g + 1) * SSM_STATE].astype(BF16)
        cb = _dot_nt(cg, bg)
        y = xg * dskip_ref[:, cols]
        for r in range(hpg):
            h = g * hpg + r
            seg = acum[:, h:h + 1] - acum_t[h:h + 1, :]
            m = cb * jnp.exp(jnp.where(tril, seg, -jnp.inf)) * dt_t[h:h + 1, :]
            xm = jnp.where((lane >= r * HEAD) & (lane < (r + 1) * HEAD), xg, 0.0)
            y = y + _dot(m.astype(BF16), xm.astype(BF16))
        hg = h_ref[0, cols, :]
        y = y + _dot_nt(cg, hg.astype(BF16)) * eacum_x[:, cols]
        zg = z_ref[0, :, cols]
        y = y * (zg * _sigmoid(zg))
        y = _rms(y, normg_ref[:, cols])
        y_ref[0, :, cols] = y.astype(y_ref.dtype)
        dec = _dot_exact_lhs(et_ref[cols, :], dec_b)
        h_ref[0, cols, :] = dec * hg + _dot_tn((xg * wdec_x[:, cols]).astype(BF16), bg)


def ssd_prompt(u, conv_w, conv_b, dt_bias, a_log, d_skip, ssm_norm, n_heads):
    b, l, _ = u.shape
    d_inner = n_heads * HEAD
    conv_dim = d_inner + 2 * SSM_GROUPS * SSM_STATE
    C = CHUNK
    assert l % C == 0 and conv_dim % d_inner == 0
    pad = lambda v: jnp.pad(v.reshape(1, -1), ((0, 0), (0, LANES - v.size)))
    e = (np.arange(d_inner)[None, :] // HEAD == np.arange(LANES)[:, None])
    e_bf = jnp.asarray(e, BF16)
    et_bf = jnp.asarray(e.T, BF16)
    full = lambda shape: pl.BlockSpec(shape, lambda i, j: (0,) * len(shape))
    return pl.pallas_call(
        functools.partial(_ssd_chunk_kernel, n_heads=n_heads),
        grid=(b, l // C),
        in_specs=[pl.BlockSpec((1, C, conv_dim), lambda i, j: (i, j, 0)),
                  pl.BlockSpec((1, C, d_inner), lambda i, j: (i, j, conv_dim // d_inner)),
                  pl.BlockSpec((1, C, LANES), lambda i, j: (i, j, (conv_dim + d_inner) // LANES)),
                  full((CONV_WIDTH, conv_dim)), full((1, conv_dim)), full((1, LANES)), full((1, LANES)),
                  full((1, d_inner)), full((1, d_inner)), full((LANES, d_inner)), full((d_inner, LANES))],
        out_specs=[pl.BlockSpec((1, C, d_inner), lambda i, j: (i, j, 0)),
                   pl.BlockSpec((1, d_inner, SSM_STATE), lambda i, j: (i, 0, 0))],
        out_shape=[jax.ShapeDtypeStruct((b, l, d_inner), BF16),
                   jax.ShapeDtypeStruct((b, d_inner, SSM_STATE), F32)],
        scratch_shapes=[pltpu.VMEM((C + 8, conv_dim), F32), pltpu.VMEM((C, conv_dim), F32)],
        compiler_params=_cparams(2),
        name="ssd_prompt",
    )(u, u, u, conv_w, conv_b.reshape(1, -1), pad(dt_bias), pad(a_log),
      jnp.repeat(d_skip, HEAD).reshape(1, -1), ssm_norm.reshape(1, -1), e_bf, et_bf)


def _head_masks(shape):
    lane = lax.broadcasted_iota(jnp.int32, shape, 1)
    return lane < HEAD, lane >= HEAD


def _rwkv_features(proj, prev, mu_ref, w0_ref, w2_ref, a0_ref, a2_ref, g2_ref, kk_ref, ka_ref, dim):
    xs = proj + (prev - proj) * mu_ref[...]
    k = xs[:, dim:2 * dim]
    t_wa = xs[:, 3 * dim:3 * dim + LANES]
    xg = xs[:, 3 * dim + LANES:3 * dim + 2 * LANES]
    w = -_softplus(-(w0_ref[...] + _dot(jnp.tanh(t_wa).astype(BF16), w2_ref[...]))) - 0.5
    a = _sigmoid(a0_ref[...] + _dot(t_wa.astype(BF16), a2_ref[...]))
    g = _dot(_sigmoid(xg).astype(BF16), g2_ref[...])
    return (xs[:, 0:dim], k * (1.0 + (a - 1.0) * ka_ref[...]), xs[:, 2 * dim:3 * dim], k * kk_ref[...], a,
            -jnp.exp(w), g)


def _rwkv_chunk_kernel(p_ref, mu_ref, w0_ref, w2_ref, a0_ref, a2_ref, g2_ref, kk_ref, ka_ref, rk_ref,
                       lng_ref, lnb_ref, ones2_ref, o_ref, s_ref,
                       ext_ref, r_s, k_s, v_s, kkn_s, a_s, lw_s, g_s, x_s, n_s, q_s, yb_s, bkh_s, pc_s, *, dim):
    c = pl.program_id(1)
    C = p_ref.shape[1]
    n_pairs = dim // LANES

    @pl.when(c == 0)
    def _():
        ext_ref[0:8, :] = jnp.zeros((8, ext_ref.shape[1]), F32)
        s_ref[...] = jnp.zeros(s_ref.shape, F32)

    @pl.when(c > 0)
    def _():
        ext_ref[0:8, :] = ext_ref[C:C + 8, :]

    proj = p_ref[0]
    ext_ref[8:C + 8, :] = proj
    (r_s[...], k_s[...], v_s[...], kkn_s[...], a_s[...], lw_s[...], g_s[...]) = _rwkv_features(
        proj, ext_ref[7:C + 7, :], mu_ref, w0_ref, w2_ref, a0_ref, a2_ref, g2_ref, kk_ref, ka_ref, dim)

    tril = _tril(C)
    tril_strict = _tril(C, -1)
    tril_bf = tril.astype(BF16)
    ones2 = ones2_ref[...]
    m0, m1 = _head_masks((C, LANES))
    row = lax.broadcasted_iota(jnp.int32, (LANES, LANES), 0)
    col = lax.broadcasted_iota(jnp.int32, (LANES, LANES), 1)
    blockdiag = (row < HEAD) == (col < HEAD)
    n_levels = C.bit_length() - 1
    assert 1 << n_levels == C

    def by_head(x):
        return jnp.concatenate([jnp.where(m0, x, 0.0), jnp.where(m1, x, 0.0)], axis=0)

    for j in range(n_pairs):
        cols = slice(j * LANES, (j + 1) * LANES)
        lw = lw_s[:, cols]
        logp = _dot_exact_lhs(tril_bf, lw)
        logpc = logp[C - 1:C, :]
        ep, epinv = jnp.exp(logp), jnp.exp(-logp)
        epc = jnp.exp(logpc - logp)
        kkf = kkn_s[:, cols]
        kk = kkf * lax.rsqrt(_dot_exact_rhs(kkf * kkf, ones2, 2) + 1e-12)
        aj, kj, vj, rj = a_s[:, cols], k_s[:, cols], v_s[:, cols], r_s[:, cols]
        bvec = kk * aj
        at = -kk * jnp.exp(logp - lw)
        rt = rj * ep
        bk = jnp.concatenate([bvec * epinv, kj * epinv], axis=0)
        bk_bf = bk.astype(BF16)
        as_ = _dot_nt(jnp.concatenate([at, rt], axis=0).astype(BF16), s_ref[0, j].astype(BF16))
        mab, mak, qab, qak = [], [], [], []
        for hm in (m0, m1):
            gm = _dot_hi(jnp.where(hm, at, 0.0), bk, _dot_nt)
            mab.append(jnp.where(tril_strict, gm[:, 0:C], 0.0))
            mak.append(jnp.where(tril_strict, gm[:, C:2 * C], 0.0))
            gq = _dot_nt(jnp.where(hm, rt, 0.0).astype(BF16), bk_bf)
            qab.append(jnp.where(tril, gq[:, 0:C], 0.0))
            qak.append(jnp.where(tril, gq[:, C:2 * C], 0.0))
        x_s[j] = as_[0:C] + _dot_hi(jnp.concatenate(mak, axis=1), by_head(vj))
        n_s[j] = jnp.concatenate(mab, axis=1)
        q_s[j] = jnp.concatenate(qab + qak, axis=1).astype(BF16)
        yb_s[j] = as_[C:2 * C]
        bkh_s[j] = jnp.concatenate([bvec * epc, kj * epc], axis=0).astype(BF16)
        pc_s[j] = jnp.broadcast_to(jnp.exp(logpc), (8, LANES))

    zero = jnp.zeros((C, C), F32)
    for lvl in range(n_levels):
        for j in range(n_pairs):
            n = n_s[j]
            xs = by_head(x_s[j])
            if lvl < HI_LEVELS:
                x_s[j] += _dot_hi(n, xs)
            else:
                x_s[j] += _dot(n.astype(BF16), xs.astype(BF16))
            if lvl + 1 < n_levels:
                nd = jnp.concatenate([jnp.concatenate([n[:, 0:C], zero], axis=1),
                                      jnp.concatenate([zero, n[:, C:2 * C]], axis=1)], axis=0)
                if lvl + 1 < HI_LEVELS:
                    n_s[j] = _dot_hi(n, nd)
                else:
                    n_s[j] = _dot(n.astype(BF16), nd.astype(BF16))

    for j in range(n_pairs):
        cols = slice(j * LANES, (j + 1) * LANES)
        kj, vj, rj = k_s[:, cols], v_s[:, cols], r_s[:, cols]
        x = x_s[j]
        y = yb_s[j] + _dot(q_s[j], jnp.concatenate([by_head(x), by_head(vj)], axis=0).astype(BF16))
        upd = _dot_tn(jnp.concatenate([x, vj], axis=0).astype(BF16), bkh_s[j])
        s_ref[0, j] = pc_s[j, 0:1, :] * s_ref[0, j] + jnp.where(blockdiag, upd, 0.0)

        mean = _dot_exact_rhs(y, ones2, 2) * (1.0 / HEAD)
        d = y - mean
        var = _dot_exact_rhs(d * d, ones2, 2) * (1.0 / HEAD)
        yn = d * lax.rsqrt(var + LN_X_EPS) * lng_ref[:, cols] + lnb_ref[:, cols]
        bonus = _dot_exact_rhs(rj * kj * rk_ref[:, cols], ones2, 2) * vj
        o_ref[0, :, cols] = ((yn + bonus) * g_s[:, cols]).astype(o_ref.dtype)


def rwkv_prompt(proj, shift_mu, decay_w0, decay_w2, aaa_a0, aaa_a2, gate_g2, k_k, k_a, r_k, lnx_g, lnx_b):
    b, l, sd = proj.shape
    dim = decay_w0.size
    n_heads = dim // HEAD
    dl, al, gl = decay_w2.shape[0], aaa_a2.shape[0], gate_g2.shape[0]
    assert dl + al == LANES and gl == LANES and sd == 3 * dim + 2 * LANES and l % CHUNK == 0
    C = CHUNK
    n_pairs = dim // LANES
    w2 = jnp.concatenate([decay_w2, jnp.zeros((al, dim), F32)], axis=0).astype(BF16)
    a2 = jnp.concatenate([jnp.zeros((dl, dim), F32), aaa_a2], axis=0).astype(BF16)
    ones2 = jnp.asarray(np.arange(LANES)[:, None] // HEAD == np.arange(LANES)[None, :] // HEAD, BF16)
    row = lambda v: v.reshape(1, -1)
    full = lambda shape: pl.BlockSpec(shape, lambda i, j: (0,) * len(shape))
    vec = full((1, dim))
    y, s = pl.pallas_call(
        functools.partial(_rwkv_chunk_kernel, dim=dim),
        grid=(b, l // C),
        in_specs=[pl.BlockSpec((1, C, sd), lambda i, j: (i, j, 0)), full((1, sd)), vec, full((LANES, dim)), vec,
                  full((LANES, dim)), full((LANES, dim)), vec, vec, vec, vec, vec, full((LANES, LANES))],
        out_specs=[pl.BlockSpec((1, C, dim), lambda i, j: (i, j, 0)),
                   pl.BlockSpec((1, dim // LANES, LANES, LANES), lambda i, j: (i, 0, 0, 0))],
        out_shape=[jax.ShapeDtypeStruct((b, l, dim), BF16),
                   jax.ShapeDtypeStruct((b, dim // LANES, LANES, LANES), F32)],
        scratch_shapes=[pltpu.VMEM((C + 8, sd), F32)] + [pltpu.VMEM((C, dim), F32)] * 7
                       + [pltpu.VMEM((n_pairs, C, LANES), F32), pltpu.VMEM((n_pairs, C, 2 * C), F32),
                          pltpu.VMEM((n_pairs, C, 4 * C), BF16), pltpu.VMEM((n_pairs, C, LANES), F32),
                          pltpu.VMEM((n_pairs, 2 * C, LANES), BF16), pltpu.VMEM((n_pairs, 8, LANES), F32)],
        compiler_params=_cparams(2),
        name="rwkv_prompt",
    )(proj, row(shift_mu), row(decay_w0), w2, row(aaa_a0), a2, gate_g2.astype(BF16), row(k_k), row(k_a),
      row(r_k), row(lnx_g), row(lnx_b), ones2)
    s = s.reshape(b, dim // LANES, 2, HEAD, 2, HEAD)
    s = jnp.stack([s[:, :, 0, :, 0, :], s[:, :, 1, :, 1, :]], axis=2).reshape(b, n_heads, HEAD, HEAD)
    return y, s


def _mix_kernel(x_ref, ym_ref, yr_ref, gate_ref, wm_ref, wr_ref, wo_ref, nf_ref, wq_ref, x2_ref, q_ref):
    d = x_ref.shape[1]
    y_m = _dot(ym_ref[...].astype(BF16), wm_ref[...])
    y_r = _dot(yr_ref[...].astype(BF16), wr_ref[...])
    mix = _sigmoid(gate_ref[:, 0:d]) * y_m + _sigmoid(gate_ref[:, d:2 * d]) * y_r
    x2 = x_ref[...] + _dot(mix.astype(BF16), wo_ref[...])
    x2_ref[...] = x2
    q_ref[...] = _dot(_rms(x2, nf_ref[...]).astype(BF16), wq_ref[...]).astype(q_ref.dtype)


def mix_and_query(x, ym, yr, gates, w_out_ssm, w_out_rwkv, w_out, norm_ffn, peer_wq, tm):
    m, d = x.shape
    tm = min(tm, m)
    assert m % tm == 0
    rows = lambda n: pl.BlockSpec((tm, n), lambda i: (i, 0))
    full = lambda a: pl.BlockSpec(a.shape, lambda i: (0, 0))
    nq = peer_wq.shape[1]
    args = (x, ym, yr, gates, w_out_ssm, w_out_rwkv, w_out, norm_ffn.reshape(1, -1), peer_wq)
    return pl.pallas_call(
        _mix_kernel,
        grid=(m // tm,),
        in_specs=[rows(d), rows(ym.shape[1]), rows(yr.shape[1]), rows(2 * d)] + [full(a) for a in args[4:]],
        out_specs=[rows(d), rows(nq)],
        out_shape=[jax.ShapeDtypeStruct((m, d), F32), jax.ShapeDtypeStruct((m, nq), BF16)],
        compiler_params=_cparams(1),
        name="mix_and_query",
    )(*args)


PEER_TOPK = 16
N_KEYS = 128


def _erf_gelu(x):
    return 0.5 * x * (1.0 + lax.erf(x * np.float32(1.0 / np.sqrt(2.0))))


NOT_TOP = 64.0


def _kth_largest(s, k, want_rank=False):
    tops = []
    rank = jnp.full(s.shape, NOT_TOP, F32)
    for r in range(k):
        m = jnp.max(s, axis=0, keepdims=True)
        tops.append(m)
        hit = s == m
        if want_rank:
            rank = jnp.where(hit, float(r), rank)
        s = jnp.where(hit, -jnp.inf, s)
    return (tops, rank) if want_rank else tops


def _peer_kernel(x2_ref, q_ref, p_ref, k1_ref, k2_ref, u_ref, vt_ref, nf_ref, npl_ref, wg_ref, wp_ref, nfin_ref,
                 y_ref, hb_s, n1_s, c_s, rk2_s, d_s, act0_s, act1_s, w0_s, w1_s, acc_s, *, n_heads, n_e):
    j = pl.program_id(1)
    tt = x2_ref.shape[0]
    ec = u_ref.shape[0]
    qd = N_KEYS
    assert ec == 8 * N_KEYS

    @pl.when(j == 0)
    def _():
        hb_s[...] = _rms(x2_ref[...], nf_ref[...]).astype(BF16)
        acc_s[...] = jnp.zeros(acc_s.shape, F32)
        act1_s[...] = jnp.zeros(act1_s.shape, F32)
        w0_s[...] = jnp.zeros(w0_s.shape, BF16)
        for h in range(n_heads):
            for ts in range(tt // LANES):
                tok = slice(ts * LANES, (ts + 1) * LANES)
                q1 = q_ref[tok, (2 * h) * qd:(2 * h + 1) * qd]
                q2 = q_ref[tok, (2 * h + 1) * qd:(2 * h + 2) * qd]
                s1 = _dot_nt(k1_ref[h], q1)
                s2 = _dot_nt(k2_ref[h], q2)
                v1 = _kth_largest(s1, PEER_TOPK)
                v2, rank2 = _kth_largest(s2, PEER_TOPK, want_rank=True)
                v2a = jnp.concatenate(v2, axis=0)
                cand = jnp.concatenate([v + v2a for v in v1], axis=0)
                top = _kth_largest(cand, PEER_TOPK)
                z = sum(jnp.exp(t - top[0]) for t in top)
                th = top[PEER_TOPK - 1]
                n1_s[h, :, tok] = sum(jnp.where(s1 + v >= th, 1.0, 0.0) for v in v2)
                c_s[h, :, tok] = jnp.exp(s1 - v1[0]) / z
                rk2_s[h, :, tok] = rank2.astype(BF16)
                d_s[h, :, tok] = jnp.exp(s2 - v2[0]).astype(BF16)

    def step(act_w, act_r, w_w, w_r):
        act_w[...] = _dot_nt(u_ref[...], hb_s[...])
        b_valid = (j >= 1) & (j <= n_e)
        i_rows = pl.ds(pl.multiple_of(jnp.clip(j - 1, 0, n_e - 1) * 8, 8), 8)
        for ts in range(tt // LANES):
            tok = slice(ts * LANES, (ts + 1) * LANES)
            for il in range(8):
                rows = slice(il * N_KEYS, (il + 1) * N_KEYS)
                w = jnp.zeros((N_KEYS, LANES), BF16)
                for h in range(n_heads):
                    n1_row = n1_s[h, i_rows, tok][il:il + 1].astype(BF16)
                    c_row = c_s[h, i_rows, tok][il:il + 1].astype(BF16)
                    w = w + jnp.where(rk2_s[h, :, tok] < n1_row, d_s[h, :, tok], 0.0) * c_row
                w = w * _erf_gelu(act_r[rows, tok]).astype(BF16)
                w_w[rows, tok] = jnp.where(b_valid, w, 0.0)
        acc_s[...] += _dot(vt_ref[...], w_r[...])

    @pl.when(j % 2 == 0)
    def _():
        step(act0_s, act1_s, w1_s, w0_s)

    @pl.when(j % 2 == 1)
    def _():
        step(act1_s, act0_s, w0_s, w1_s)

    @pl.when(j == n_e + 1)
    def _():
        x3 = x2_ref[...] + acc_s[...].T
        gate = _sigmoid(_dot(_rms(x3, npl_ref[...]).astype(BF16), wg_ref[...]))
        x4 = x3 + gate * _dot(p_ref[...].astype(BF16), wp_ref[...])
        y_ref[...] = _rms(x4, nfin_ref[...])


def peer_ple_final(x2, q, p, peer_k1, peer_k2, peer_u, peer_vt, norm_ffn, norm_ple, w_ple_gate, w_ple_proj,
                   norm_final, tt, ec):
    t, d = x2.shape
    n_heads = peer_k1.shape[0]
    n_exp = peer_u.shape[0]
    tt = min(tt, t)
    assert t % tt == 0 and n_exp % ec == 0 and tt % LANES == 0 and ec % N_KEYS == 0
    assert peer_k1.shape[1:] == (N_KEYS, N_KEYS) and n_exp == N_KEYS * N_KEYS
    rows = lambda n: pl.BlockSpec((tt, n), lambda i, j: (i, 0))
    full = lambda a: pl.BlockSpec(a.shape, lambda i, j: (0,) * a.ndim)
    row = lambda v: v.reshape(1, -1)
    consts = (row(norm_ffn), row(norm_ple), w_ple_gate, w_ple_proj, row(norm_final))
    n_e = n_exp // ec
    hs = lambda dt: pltpu.VMEM((n_heads, N_KEYS, tt), dt)
    return pl.pallas_call(
        functools.partial(_peer_kernel, n_heads=n_heads, n_e=n_e),
        grid=(t // tt, n_e + 2),
        in_specs=[rows(d), rows(q.shape[1]), rows(p.shape[1]), full(peer_k1), full(peer_k2),
                  pl.BlockSpec((ec, d), lambda i, j: (jnp.minimum(j, n_e - 1), 0)),
                  pl.BlockSpec((d, ec), lambda i, j: (0, jnp.clip(j - 2, 0, n_e - 1)))]
                 + [full(a) for a in consts],
        out_specs=rows(d),
        out_shape=jax.ShapeDtypeStruct((t, d), F32),
        scratch_shapes=[pltpu.VMEM((tt, d), BF16), hs(F32), hs(F32), hs(BF16), hs(BF16),
                        pltpu.VMEM((ec, tt), F32), pltpu.VMEM((ec, tt), F32),
                        pltpu.VMEM((ec, tt), BF16), pltpu.VMEM((ec, tt), BF16), pltpu.VMEM((d, tt), F32)],
        compiler_params=_cparams(2),
        name="peer_ple_final",
    )(x2, q, p, peer_k1, peer_k2, peer_u, peer_vt, *consts)


def _as_column(x_row):
    n = x_row.shape[1]
    eye = lax.broadcasted_iota(jnp.int32, (n, n), 0) == lax.broadcasted_iota(jnp.int32, (n, n), 1)
    return jnp.sum(jnp.where(eye, jnp.broadcast_to(x_row, (n, n)), 0.0), axis=1, keepdims=True)


def _rows8(x_row):
    return jnp.broadcast_to(x_row, (8, x_row.shape[1]))


def _ssd_step_kernel(xbc_ref, z_ref, dt_ref, conv_ref, h_ref, convw_ref, convb_ref, dtb_ref, alog_ref, dskip_ref,
                     normg_ref, e_ref, y_ref, ho_ref, *, n_heads):
    d_inner = n_heads * HEAD
    gw = d_inner // SSM_GROUPS
    cs = conv_ref[0]
    conv = convb_ref[...] + xbc_ref[0] * convw_ref[CONV_WIDTH - 1:CONV_WIDTH, :]
    for k in range(CONV_WIDTH - 1):
        conv = conv + cs[k:k + 1, :] * convw_ref[k:k + 1, :]
    act = conv * _sigmoid(conv)
    dt = _softplus(dt_ref[0] + dtb_ref[...])
    da = jnp.exp(dt * -jnp.exp(alog_ref[...]))
    e = e_ref[...]
    dt_x = _dot_exact_rhs(_rows8(dt), e)[0:1]
    da_x = _dot_exact_rhs(_rows8(da), e)[0:1]
    for g in range(SSM_GROUPS):
        cols = slice(g * gw, (g + 1) * gw)
        xg = act[:, cols]
        bg = act[:, d_inner + g * SSM_STATE:d_inner + (g + 1) * SSM_STATE]
        cg = act[:, d_inner + (SSM_GROUPS + g) * SSM_STATE:d_inner + (SSM_GROUPS + g + 1) * SSM_STATE]
        hn = _as_column(da_x[:, cols]) * h_ref[0, cols, :] + _as_column(xg * dt_x[:, cols]) * bg
        ho_ref[0, cols, :] = hn
        y = _dot_nt(_rows8(cg).astype(BF16), hn.astype(BF16))[0:1] + xg * dskip_ref[:, cols]
        zg = z_ref[0, :, cols]
        y = y * (zg * _sigmoid(zg))
        y_ref[0, :, cols] = _rms(y, normg_ref[:, cols])


def ssd_step(u, state_conv, state_ssm, conv_w, conv_b, dt_bias, a_log, d_skip, ssm_norm, n_heads):
    b = u.shape[0]
    d_inner = n_heads * HEAD
    conv_dim = d_inner + 2 * SSM_GROUPS * SSM_STATE
    pad = lambda v: jnp.pad(v.reshape(1, -1), ((0, 0), (0, LANES - v.size)))
    e_bf = jnp.asarray(np.arange(d_inner)[None, :] // HEAD == np.arange(LANES)[:, None], BF16)
    full = lambda shape: pl.BlockSpec(shape, lambda i: (0,) * len(shape))
    u3 = u.reshape(b, 1, -1)
    y, h = pl.pallas_call(
        functools.partial(_ssd_step_kernel, n_heads=n_heads),
        grid=(b,),
        in_specs=[pl.BlockSpec((1, 1, conv_dim), lambda i: (i, 0, 0)),
                  pl.BlockSpec((1, 1, d_inner), lambda i: (i, 0, conv_dim // d_inner)),
                  pl.BlockSpec((1, 1, LANES), lambda i: (i, 0, (conv_dim + d_inner) // LANES)),
                  pl.BlockSpec((1, CONV_WIDTH - 1, conv_dim), lambda i: (i, 0, 0)),
                  pl.BlockSpec((1, d_inner, SSM_STATE), lambda i: (i, 0, 0)),
                  full((CONV_WIDTH, conv_dim)), full((1, conv_dim)), full((1, LANES)), full((1, LANES)),
                  full((1, d_inner)), full((1, d_inner)), full((LANES, d_inner))],
        out_specs=[pl.BlockSpec((1, 1, d_inner), lambda i: (i, 0, 0)),
                   pl.BlockSpec((1, d_inner, SSM_STATE), lambda i: (i, 0, 0))],
        out_shape=[jax.ShapeDtypeStruct((b, 1, d_inner), F32),
                   jax.ShapeDtypeStruct((b, d_inner, SSM_STATE), F32)],
        compiler_params=_cparams(1),
        name="ssd_step",
    )(u3, u3, u3, state_conv, state_ssm.reshape(b, d_inner, SSM_STATE), conv_w, conv_b.reshape(1, -1),
      pad(dt_bias), pad(a_log), jnp.repeat(d_skip, HEAD).reshape(1, -1), ssm_norm.reshape(1, -1), e_bf)
    return y.reshape(b, d_inner), h


def _rwkv_step_features_kernel(p_ref, prev_ref, mu_ref, w0_ref, w2_ref, a0_ref, a2_ref, g2_ref, kk_ref, ka_ref,
                               ones2_ref, o_ref, *, dim):
    r, k, v, kkf, a, lw, g = _rwkv_features(p_ref[...], prev_ref[...], mu_ref, w0_ref, w2_ref, a0_ref, a2_ref,
                                            g2_ref, kk_ref, ka_ref, dim)
    for j in range(dim // LANES):
        cols = slice(j * LANES, (j + 1) * LANES)
        kj = kkf[:, cols]
        o_ref[3, :, cols] = kj * lax.rsqrt(_dot_exact_rhs(kj * kj, ones2_ref[...]) + 1e-12)
    o_ref[0], o_ref[1], o_ref[2], o_ref[4], o_ref[5], o_ref[6] = r, k, v, a, jnp.exp(lw), g


def _rwkv_step_kernel(f_ref, s_ref, rk_ref, lng_ref, lnb_ref, y_ref, so_ref, y_s):
    n_heads = s_ref.shape[1]
    eye = lax.broadcasted_iota(jnp.int32, (HEAD, HEAD), 0) == lax.broadcasted_iota(jnp.int32, (HEAD, HEAD), 1)
    for h in range(n_heads):
        row = lambda i: f_ref[i, 0, h:h + 1, :]
        r, k, v, kk, a, w = (row(i) for i in range(6))
        s = s_ref[0, h]
        sa = jnp.sum(s * -kk, axis=1, keepdims=True)
        v_col = jnp.sum(jnp.where(eye, jnp.broadcast_to(v, (HEAD, HEAD)), 0.0), axis=1, keepdims=True)
        sn = s * w + sa * (kk * a) + v_col * k
        so_ref[0, h] = sn
        y_s[h:h + 1, :] = _dot_nt(_rows8(r).astype(BF16), sn.astype(BF16))[0:1]
    y = y_s[...]
    r, k, v, g = f_ref[0, 0], f_ref[1, 0], f_ref[2, 0], f_ref[6, 0]
    d = y - jnp.mean(y, axis=-1, keepdims=True)
    var = jnp.mean(d * d, axis=-1, keepdims=True)
    yn = d * lax.rsqrt(var + LN_X_EPS) * lng_ref[...] + lnb_ref[...]
    bonus = jnp.sum(r * k * rk_ref[...], axis=-1, keepdims=True) * v
    y_ref[0] = (yn + bonus) * g


def rwkv_step(proj, shift_prev, state_wkv, shift_mu, decay_w0, decay_w2, aaa_a0, aaa_a2, gate_g2, k_k, k_a, r_k,
              lnx_g, lnx_b):
    b, sd = proj.shape
    dim = decay_w0.size
    n_heads = dim // HEAD
    dl, al, gl = decay_w2.shape[0], aaa_a2.shape[0], gate_g2.shape[0]
    assert dl + al == LANES and gl == LANES and sd == 3 * dim + 2 * LANES
    w2 = jnp.concatenate([decay_w2, jnp.zeros((al, dim), F32)], axis=0).astype(BF16)
    a2 = jnp.concatenate([jnp.zeros((dl, dim), F32), aaa_a2], axis=0).astype(BF16)
    ones2 = jnp.asarray(np.arange(LANES)[:, None] // HEAD == np.arange(LANES)[None, :] // HEAD, BF16)
    row = lambda v: v.reshape(1, -1)
    args = (proj, shift_prev, row(shift_mu), row(decay_w0), w2, row(aaa_a0), a2, gate_g2.astype(BF16), row(k_k),
            row(k_a), ones2)
    feats = pl.pallas_call(
        functools.partial(_rwkv_step_features_kernel, dim=dim),
        grid=(1,),
        in_specs=[pl.BlockSpec(a.shape, lambda i: (0, 0)) for a in args],
        out_specs=pl.BlockSpec((7, b, dim), lambda i: (0, 0, 0)),
        out_shape=jax.ShapeDtypeStruct((7, b, dim), F32),
        compiler_params=_cparams(1),
        name="rwkv_step_features",
    )(*args)
    hv = lambda v: v.reshape(n_heads, HEAD)
    full = pl.BlockSpec((n_heads, HEAD), lambda i: (0, 0))
    y, s = pl.pallas_call(
        _rwkv_step_kernel,
        grid=(b,),
        in_specs=[pl.BlockSpec((7, 1, n_heads, HEAD), lambda i: (0, i, 0, 0)),
                  pl.BlockSpec((1, n_heads, HEAD, HEAD), lambda i: (i, 0, 0, 0)), full, full, full],
        out_specs=[pl.BlockSpec((1, n_heads, HEAD), lambda i: (i, 0, 0)),
                   pl.BlockSpec((1, n_heads, HEAD, HEAD), lambda i: (i, 0, 0, 0))],
        out_shape=[jax.ShapeDtypeStruct((b, n_heads, HEAD), F32),
                   jax.ShapeDtypeStruct((b, n_heads, HEAD, HEAD), F32)],
        scratch_shapes=[pltpu.VMEM((n_heads, HEAD), F32)],
        compiler_params=_cparams(1),
        name="rwkv_step",
    )(feats.reshape(7, b, n_heads, HEAD), state_wkv, hv(r_k), hv(lnx_g), hv(lnx_b))
    return y.reshape(b, dim), s


def _layer(x, p, states, wts, n_ssm_heads):
    b, l, d = x.shape
    xt = x.reshape(b * l, d)
    g_mix = wts['norm_mix'].reshape(1, -1)
    tm = 1024
    u_ssm = norm_matmul(xt, g_mix, wts['w_ssm'], tm, wts['w_ssm'].shape[1] // 7)
    u_rwkv = norm_matmul(xt, g_mix, wts['w_shift'], tm, wts['w_shift'].shape[1] // 2)
    u_gate = norm_matmul(xt, g_mix, wts['w_gates'], tm, wts['w_gates'].shape[1] // 2)
    d_inner = n_ssm_heads * HEAD
    conv_dim = d_inner + 2 * SSM_GROUPS * SSM_STATE
    ssd_w = (wts['conv_w'], wts['conv_b'], wts['dt_bias'], wts['a_log'], wts['d_skip'], wts['ssm_norm'])
    rwkv_w = tuple(wts[k] for k in ('shift_mu', 'decay_w0', 'decay_w2', 'aaa_a0', 'aaa_a2', 'gate_g2', 'k_k', 'k_a',
                                    'r_k', 'lnx_g', 'lnx_b'))
    if states is None:
        ym, ssm_new = ssd_prompt(u_ssm.reshape(b, l, -1), *ssd_w, n_ssm_heads)
        ym = ym.reshape(b * l, d_inner)
        conv_new = u_ssm.reshape(b, l, -1)[:, l - (CONV_WIDTH - 1):, :conv_dim]
        yr, wkv_new = rwkv_prompt(u_rwkv.reshape(b, l, -1), *rwkv_w)
        yr = yr.reshape(b * l, -1)
        shift_new = u_rwkv.reshape(b, l, -1)[:, l - 1]
    else:
        conv_prev, ssm_prev, wkv_prev, shift_prev = states
        ym, ssm_new = ssd_step(u_ssm, conv_prev, ssm_prev, *ssd_w, n_ssm_heads)
        conv_new = jnp.concatenate([conv_prev[:, 1:], u_ssm[:, None, :conv_dim]], axis=1)
        yr, wkv_new = rwkv_step(u_rwkv, shift_prev, wkv_prev, *rwkv_w)
        shift_new = u_rwkv
    x2, q = mix_and_query(xt, ym, yr, u_gate, wts['w_out_ssm'], wts['w_out_rwkv'], wts['w_out'], wts['norm_ffn'],
                          wts['peer_wq'], 512)
    y = peer_ple_final(x2, q, p.reshape(b * l, -1), wts['peer_k1'], wts['peer_k2'], wts['peer_u'], wts['peer_vt'],
                       wts['norm_ffn'], wts['norm_ple'], wts['w_ple_gate'], wts['w_ple_proj'], wts['norm_final'],
                       512, 8 * N_KEYS)
    return (y.reshape(b, l, d), ssm_new.reshape(b, n_ssm_heads, HEAD, SSM_STATE), conv_new, wkv_new, shift_new)


def kernel(x_prompt, x_sample, p_prompt, p_sample, state_ssm, state_conv, state_wkv, state_shift, norm_mix, w_in,
           conv_w, conv_b, dt_bias, a_log, d_skip, ssm_norm, w_out_ssm, shift_mu, decay_w0, decay_w2, aaa_a0, aaa_a2,
           gate_g2, k_k, k_a, r_k, lnx_g, lnx_b, w_out_rwkv, w_out, norm_ffn, peer_wq, peer_k1, peer_k2, peer_u,
           peer_v, norm_ple, w_ple_gate, w_ple_proj, norm_final):
    depth = w_in.shape[0]
    assert depth == 1, "single-layer trunk"
    d_model = x_prompt.shape[-1]
    n_ssm_heads = dt_bias.shape[1]
    d_inner = n_ssm_heads * HEAD
    conv_dim = conv_w.shape[2]
    shift_dim = shift_mu.shape[1]
    bf = lambda a: a.astype(BF16)
    o = np.cumsum([0, d_inner, conv_dim, n_ssm_heads, shift_dim, d_model, d_model])
    wi = w_in[0]
    wts = {
        'w_ssm': bf(jnp.concatenate([wi[:, o[1]:o[2]], wi[:, o[0]:o[1]], wi[:, o[2]:o[3]],
                                     jnp.zeros((d_model, LANES - n_ssm_heads), F32)], axis=1)),
        'w_shift': bf(wi[:, o[3]:o[4]]),
        'w_gates': bf(wi[:, o[4]:o[6]]),
        'w_out_ssm': bf(w_out_ssm[0]), 'w_out_rwkv': bf(w_out_rwkv[0]), 'w_out': bf(w_out[0]),
        'peer_wq': bf(peer_wq[0]), 'peer_k1': bf(peer_k1[0]), 'peer_k2': bf(peer_k2[0]),
        'peer_u': bf(peer_u[0]), 'peer_vt': bf(peer_v[0]).T,
        'w_ple_gate': bf(w_ple_gate[0]), 'w_ple_proj': bf(w_ple_proj[0]), 'norm_final': norm_final,
    }
    for name, val in (('norm_mix', norm_mix), ('conv_w', conv_w), ('conv_b', conv_b), ('dt_bias', dt_bias),
                      ('a_log', a_log), ('d_skip', d_skip), ('ssm_norm', ssm_norm), ('shift_mu', shift_mu),
                      ('decay_w0', decay_w0), ('decay_w2', decay_w2), ('aaa_a0', aaa_a0), ('aaa_a2', aaa_a2),
                      ('gate_g2', gate_g2), ('k_k', k_k), ('k_a', k_a), ('r_k', r_k), ('lnx_g', lnx_g),
                      ('lnx_b', lnx_b), ('norm_ffn', norm_ffn), ('norm_ple', norm_ple)):
        wts[name] = val[0]
    yp, ssm_p, conv_p, wkv_p, shift_p = _layer(x_prompt, p_prompt[0], None, wts, n_ssm_heads)
    ys, ssm_s, conv_s, wkv_s, shift_s = _layer(
        x_sample, p_sample[0], (state_conv[0], state_ssm[0], state_wkv[0], state_shift[0]), wts, n_ssm_heads)
    return (yp, ys, ssm_p[None], conv_p[None], wkv_p[None], shift_p[None],
            ssm_s[None], conv_s[None], wkv_s[None], shift_s[None])
```

```python
import functools

import numpy as np
import jax
import jax.numpy as jnp
from jax import lax
from jax.experimental import pallas as pl
from jax.experimental.pallas import tpu as pltpu

F32 = jnp.float32
BF16 = jnp.bfloat16

EPS = 1e-6
LN_X_EPS = 64e-5
HEAD = 64
SSM_STATE = 128
SSM_GROUPS = 8
CONV_WIDTH = 4
CHUNK = 128
HI_LEVELS = 2
LANES = 128
VMEM_LIMIT = 56 * 1024 * 1024


def _cparams(n_axes):
    return pltpu.CompilerParams(dimension_semantics=("arbitrary",) * n_axes,
                                vmem_limit_bytes=VMEM_LIMIT)


def _dot(a, b):
    return jnp.dot(a, b, preferred_element_type=F32)


def _dot_nt(a, b):
    return lax.dot_general(a, b, (((1,), (1,)), ((), ())), preferred_element_type=F32)


def _dot_tn(a, b):
    return lax.dot_general(a, b, (((0,), (0,)), ((), ())), preferred_element_type=F32)


def _split2(x):
    hi = x.astype(BF16)
    lo = (x - hi.astype(F32)).astype(BF16)
    return hi, lo


def _split3(x):
    x1 = x.astype(BF16)
    r = x - x1.astype(F32)
    x2 = r.astype(BF16)
    x3 = (r - x2.astype(F32)).astype(BF16)
    return x1, x2, x3


def _dot_hi(a, b, dot=_dot):
    a1, a2 = _split2(a)
    b1, b2 = _split2(b)
    return dot(a1, b1) + (dot(a1, b2) + dot(a2, b1))


def _dot_exact_rhs(a, e, passes=3):
    if passes == 2:
        a1, a2 = _split2(a)
        return _dot(a1, e) + _dot(a2, e)
    a1, a2, a3 = _split3(a)
    return _dot(a1, e) + (_dot(a2, e) + _dot(a3, e))


def _dot_exact_lhs(e, a):
    a1, a2, a3 = _split3(a)
    return _dot(e, a1) + (_dot(e, a2) + _dot(e, a3))


def _rms(x, g):
    return x * lax.rsqrt(jnp.mean(x * x, axis=-1, keepdims=True) + EPS) * g


def _sigmoid(x):
    return 1.0 / (1.0 + jnp.exp(-x))


def _softplus(x):
    return jnp.maximum(x, 0.0) + jnp.log1p(jnp.exp(-jnp.abs(x)))


def _tril(n, k=0, dtype=F32):
    r = lax.broadcasted_iota(jnp.int32, (n, n), 0)
    c = lax.broadcasted_iota(jnp.int32, (n, n), 1)
    return (c <= r + k)


def _norm_matmul_kernel(x_ref, g_ref, w_ref, o_ref, h_ref):
    @pl.when(pl.program_id(1) == 0)
    def _():
        h_ref[...] = _rms(x_ref[...], g_ref[...]).astype(BF16)

    o_ref[...] = _dot(h_ref[...], w_ref[...])


def norm_matmul(x, g, w, tm, tn):
    m, k = x.shape
    n = w.shape[1]
    tm = min(tm, m)
    assert m % tm == 0 and n % tn == 0, (m, tm, n, tn)
    return pl.pallas_call(
        _norm_matmul_kernel,
        grid=(m // tm, n // tn),
        in_specs=[pl.BlockSpec((tm, k), lambda i, j: (i, 0)),
                  pl.BlockSpec((1, k), lambda i, j: (0, 0)),
                  pl.BlockSpec((k, tn), lambda i, j: (0, j))],
        out_specs=pl.BlockSpec((tm, tn), lambda i, j: (i, j)),
        out_shape=jax.ShapeDtypeStruct((m, n), F32),
        scratch_shapes=[pltpu.VMEM((tm, k), BF16)],
        compiler_params=_cparams(2),
        name="norm_matmul",
    )(x, g, w)


def _ssd_chunk_kernel(xbc_ref, z_ref, dt_ref, convw_ref, convb_ref, dtb_ref, alog_ref, dskip_ref,
                      normg_ref, e_ref, et_ref, y_ref, h_ref, ext_ref, act_ref, *, n_heads):
    c = pl.program_id(1)
    C = xbc_ref.shape[1]
    d_inner = n_heads * HEAD
    gw = d_inner // SSM_GROUPS
    hpg = n_heads // SSM_GROUPS

    @pl.when(c == 0)
    def _():
        ext_ref[0:8, :] = jnp.zeros((8, ext_ref.shape[1]), F32)
        h_ref[...] = jnp.zeros(h_ref.shape, F32)

    @pl.when(c > 0)
    def _():
        ext_ref[0:8, :] = ext_ref[C:C + 8, :]

    ext_ref[8:C + 8, :] = xbc_ref[0]
    conv = convb_ref[...]
    for k in range(CONV_WIDTH):
        off = 8 - (CONV_WIDTH - 1) + k
        conv = conv + ext_ref[off:off + C, :] * convw_ref[k:k + 1, :]
    act_ref[...] = conv * _sigmoid(conv)

    dt = _softplus(dt_ref[0] + dtb_ref[...])
    a = -jnp.exp(alog_ref[...])
    tril = _tril(C)
    acum = _dot_exact_lhs(tril.astype(BF16), dt * a)
    acum_t = acum.T
    dt_t = dt.T
    e = e_ref[...]
    eacum_x = _dot_exact_rhs(jnp.exp(acum), e)
    wdec_x = _dot_exact_rhs(jnp.exp(acum[C - 1:C, :] - acum) * dt, e)
    dec_b = jnp.broadcast_to(jnp.exp(acum_t[:, C - 1:C]), (LANES, SSM_STATE))
    lane = lax.broadcasted_iota(jnp.int32, (C, gw), 1)

    for g in range(SSM_GROUPS):
        cols = slice(g * gw, (g + 1) * gw)
        xg = act_ref[:, cols]
        bg = act_ref[:, d_inner + g * SSM_STATE:d_inner + (g + 1) * SSM_STATE].astype(BF16)
        cg = act_ref[:, d_inner + (SSM_GROUPS + g) * SSM_STATE:
                     d_inner + (SSM_GROUPS + g + 1) * SSM_STATE].astype(BF16)
        cb = _dot_nt(cg, bg)
        y = xg * dskip_ref[:, cols]
        for r in range(hpg):
            h = g * hpg + r
            seg = acum[:, h:h + 1] - acum_t[h:h + 1, :]
            m = cb * jnp.exp(jnp.where(tril, seg, -jnp.inf)) * dt_t[h:h + 1, :]
            xm = jnp.where((lane >= r * HEAD) & (lane < (r + 1) * HEAD), xg, 0.0)
            y = y + _dot(m.astype(BF16), xm.astype(BF16))
        hg = h_ref[0, cols, :]
        y = y + _dot_nt(cg, hg.astype(BF16)) * eacum_x[:, cols]
        zg = z_ref[0, :, cols]
        y = y * (zg * _sigmoid(zg))
        y = _rms(y, normg_ref[:, cols])
        y_ref[0, :, cols] = y.astype(y_ref.dtype)
        dec = _dot_exact_lhs(et_ref[cols, :], dec_b)
        h_ref[0, cols, :] = dec * hg + _dot_tn((xg * wdec_x[:, cols]).astype(BF16), bg)


def ssd_prompt(u, conv_w, conv_b, dt_bias, a_log, d_skip, ssm_norm, n_heads):
    b, l, _ = u.shape
    d_inner = n_heads * HEAD
    conv_dim = d_inner + 2 * SSM_GROUPS * SSM_STATE
    C = CHUNK
    assert l % C == 0 and conv_dim % d_inner == 0
    pad = lambda v: jnp.pad(v.reshape(1, -1), ((0, 0), (0, LANES - v.size)))
    e = (np.arange(d_inner)[None, :] // HEAD == np.arange(LANES)[:, None])
    e_bf = jnp.asarray(e, BF16)
    et_bf = jnp.asarray(e.T, BF16)
    full = lambda shape: pl.BlockSpec(shape, lambda i, j: (0,) * len(shape))
    return pl.pallas_call(
        functools.partial(_ssd_chunk_kernel, n_heads=n_heads),
        grid=(b, l // C),
        in_specs=[pl.BlockSpec((1, C, conv_dim), lambda i, j: (i, j, 0)),
                  pl.BlockSpec((1, C, d_inner), lambda i, j: (i, j, conv_dim // d_inner)),
                  pl.BlockSpec((1, C, LANES), lambda i, j: (i, j, (conv_dim + d_inner) // LANES)),
                  full((CONV_WIDTH, conv_dim)), full((1, conv_dim)), full((1, LANES)), full((1, LANES)),
                  full((1, d_inner)), full((1, d_inner)), full((LANES, d_inner)), full((d_inner, LANES))],
        out_specs=[pl.BlockSpec((1, C, d_inner), lambda i, j: (i, j, 0)),
                   pl.BlockSpec((1, d_inner, SSM_STATE), lambda i, j: (i, 0, 0))],
        out_shape=[jax.ShapeDtypeStruct((b, l, d_inner), BF16),
                   jax.ShapeDtypeStruct((b, d_inner, SSM_STATE), F32)],
        scratch_shapes=[pltpu.VMEM((C + 8, conv_dim), F32), pltpu.VMEM((C, conv_dim), F32)],
        compiler_params=_cparams(2),
        name="ssd_prompt",
    )(u, u, u, conv_w, conv_b.reshape(1, -1), pad(dt_bias), pad(a_log),
      jnp.repeat(d_skip, HEAD).reshape(1, -1), ssm_norm.reshape(1, -1), e_bf, et_bf)


def _head_masks(shape):
    lane = lax.broadcasted_iota(jnp.int32, shape, 1)
    return lane < HEAD, lane >= HEAD


def _rwkv_features(proj, prev, mu_ref, w0_ref, w2_ref, a0_ref, a2_ref, g2_ref, kk_ref, ka_ref, dim):
    xs = proj + (prev - proj) * mu_ref[...]
    k = xs[:, dim:2 * dim]
    t_wa = xs[:, 3 * dim:3 * dim + LANES]
    xg = xs[:, 3 * dim + LANES:3 * dim + 2 * LANES]
    w = -_softplus(-(w0_ref[...] + _dot(jnp.tanh(t_wa).astype(BF16), w2_ref[...]))) - 0.5
    a = _sigmoid(a0_ref[...] + _dot(t_wa.astype(BF16), a2_ref[...]))
    g = _dot(_sigmoid(xg).astype(BF16), g2_ref[...])
    return (xs[:, 0:dim], k * (1.0 + (a - 1.0) * ka_ref[...]), xs[:, 2 * dim:3 * dim], k * kk_ref[...], a,
            -jnp.exp(w), g)


def _rwkv_chunk_kernel(p_ref, mu_ref, w0_ref, w2_ref, a0_ref, a2_ref, g2_ref, kk_ref, ka_ref, rk_ref,
                       lng_ref, lnb_ref, ones2_ref, o_ref, s_ref,
                       ext_ref, r_s, k_s, v_s, kkn_s, a_s, lw_s, g_s, x_s, n_s, q_s, yb_s, bkh_s, pc_s, *, dim):
    c = pl.program_id(1)
    C = p_ref.shape[1]
    n_pairs = dim // LANES

    @pl.when(c == 0)
    def _():
        ext_ref[0:8, :] = jnp.zeros((8, ext_ref.shape[1]), F32)
        s_ref[...] = jnp.zeros(s_ref.shape, F32)

    @pl.when(c > 0)
    def _():
        ext_ref[0:8, :] = ext_ref[C:C + 8, :]

    proj = p_ref[0]
    ext_ref[8:C + 8, :] = proj
    (r_s[...], k_s[...], v_s[...], kkn_s[...], a_s[...], lw_s[...], g_s[...]) = _rwkv_features(
        proj, ext_ref[7:C + 7, :], mu_ref, w0_ref, w2_ref, a0_ref, a2_ref, g2_ref, kk_ref, ka_ref, dim)

    tril = _tril(C)
    tril_strict = _tril(C, -1)
    tril_bf = tril.astype(BF16)
    ones2 = ones2_ref[...]
    m0, m1 = _head_masks((C, LANES))
    row = lax.broadcasted_iota(jnp.int32, (LANES, LANES), 0)
    col = lax.broadcasted_iota(jnp.int32, (LANES, LANES), 1)
    blockdiag = (row < HEAD) == (col < HEAD)
    n_levels = C.bit_length() - 1
    assert 1 << n_levels == C

    def by_head(x):
        return jnp.concatenate([jnp.where(m0, x, 0.0), jnp.where(m1, x, 0.0)], axis=0)

    for j in range(n_pairs):
        cols = slice(j * LANES, (j + 1) * LANES)
        lw = lw_s[:, cols]
        logp = _dot_exact_lhs(tril_bf, lw)
        logpc = logp[C - 1:C, :]
        ep, epinv = jnp.exp(logp), jnp.exp(-logp)
        epc = jnp.exp(logpc - logp)
        kkf = kkn_s[:, cols]
        kk = kkf * lax.rsqrt(_dot_exact_rhs(kkf * kkf, ones2, 2) + 1e-12)
        aj, kj, vj, rj = a_s[:, cols], k_s[:, cols], v_s[:, cols], r_s[:, cols]
        bvec = kk * aj
        at = -kk * jnp.exp(logp - lw)
        rt = rj * ep
        bk = jnp.concatenate([bvec * epinv, kj * epinv], axis=0)
        bk_bf = bk.astype(BF16)
        as_ = _dot_nt(jnp.concatenate([at, rt], axis=0).astype(BF16), s_ref[0, j].astype(BF16))
        mab, mak, qab, qak = [], [], [], []
        for hm in (m0, m1):
            gm = _dot_hi(jnp.where(hm, at, 0.0), bk, _dot_nt)
            mab.append(jnp.where(tril_strict, gm[:, 0:C], 0.0))
            mak.append(jnp.where(tril_strict, gm[:, C:2 * C], 0.0))
            gq = _dot_nt(jnp.where(hm, rt, 0.0).astype(BF16), bk_bf)
            qab.append(jnp.where(tril, gq[:, 0:C], 0.0))
            qak.append(jnp.where(tril, gq[:, C:2 * C], 0.0))
        x_s[j] = as_[0:C] + _dot_hi(jnp.concatenate(mak, axis=1), by_head(vj))
        n_s[j] = jnp.concatenate(mab, axis=1)
        q_s[j] = jnp.concatenate(qab + qak, axis=1).astype(BF16)
        yb_s[j] = as_[C:2 * C]
        bkh_s[j] = jnp.concatenate([bvec * epc, kj * epc], axis=0).astype(BF16)
        pc_s[j] = jnp.broadcast_to(jnp.exp(logpc), (8, LANES))

    zero = jnp.zeros((C, C), F32)
    for lvl in range(n_levels):
        for j in range(n_pairs):
            n = n_s[j]
            xs = by_head(x_s[j])
            if lvl < HI_LEVELS:
                x_s[j] += _dot_hi(n, xs)
            else:
                x_s[j] += _dot(n.astype(BF16), xs.astype(BF16))
            if lvl + 1 < n_levels:
                nd = jnp.concatenate([jnp.concatenate([n[:, 0:C], zero], axis=1),
                                      jnp.concatenate([zero, n[:, C:2 * C]], axis=1)], axis=0)
                if lvl + 1 < HI_LEVELS:
                    n_s[j] = _dot_hi(n, nd)
                else:
                    n_s[j] = _dot(n.astype(BF16), nd.astype(BF16))

    for j in range(n_pairs):
        cols = slice(j * LANES, (j + 1) * LANES)
        kj, vj, rj = k_s[:, cols], v_s[:, cols], r_s[:, cols]
        x = x_s[j]
        y = yb_s[j] + _dot(q_s[j], jnp.concatenate([by_head(x), by_head(vj)], axis=0).astype(BF16))
        upd = _dot_tn(jnp.concatenate([x, vj], axis=0).astype(BF16), bkh_s[j])
        s_ref[0, j] = pc_s[j, 0:1, :] * s_ref[0, j] + jnp.where(blockdiag, upd, 0.0)

        mean = _dot_exact_rhs(y, ones2, 2) * (1.0 / HEAD)
        d = y - mean
        var = _dot_exact_rhs(d * d, ones2, 2) * (1.0 / HEAD)
        yn = d * lax.rsqrt(var + LN_X_EPS) * lng_ref[:, cols] + lnb_ref[:, cols]
        bonus = _dot_exact_rhs(rj * kj * rk_ref[:, cols], ones2, 2) * vj
        o_ref[0, :, cols] = ((yn + bonus) * g_s[:, cols]).astype(o_ref.dtype)


def rwkv_prompt(proj, shift_mu, decay_w0, decay_w2, aaa_a0, aaa_a2, gate_g2, k_k, k_a, r_k, lnx_g, lnx_b):
    b, l, sd = proj.shape
    dim = decay_w0.size
    n_heads = dim // HEAD
    dl, al, gl = decay_w2.shape[0], aaa_a2.shape[0], gate_g2.shape[0]
    assert dl + al == LANES and gl == LANES and sd == 3 * dim + 2 * LANES and l % CHUNK == 0
    C = CHUNK
    n_pairs = dim // LANES
    w2 = jnp.concatenate([decay_w2, jnp.zeros((al, dim), F32)], axis=0).astype(BF16)
    a2 = jnp.concatenate([jnp.zeros((dl, dim), F32), aaa_a2], axis=0).astype(BF16)
    ones2 = jnp.asarray(np.arange(LANES)[:, None] // HEAD == np.arange(LANES)[None, :] // HEAD, BF16)
    row = lambda v: v.reshape(1, -1)
    full = lambda shape: pl.BlockSpec(shape, lambda i, j: (0,) * len(shape))
    vec = full((1, dim))
    y, s = pl.pallas_call(
        functools.partial(_rwkv_chunk_kernel, dim=dim),
        grid=(b, l // C),
        in_specs=[pl.BlockSpec((1, C, sd), lambda i, j: (i, j, 0)), full((1, sd)), vec, full((LANES, dim)), vec,
                  full((LANES, dim)), full((LANES, dim)), vec, vec, vec, vec, vec, full((LANES, LANES))],
        out_specs=[pl.BlockSpec((1, C, dim), lambda i, j: (i, j, 0)),
                   pl.BlockSpec((1, dim // LANES, LANES, LANES), lambda i, j: (i, 0, 0, 0))],
        out_shape=[jax.ShapeDtypeStruct((b, l, dim), BF16),
                   jax.ShapeDtypeStruct((b, dim // LANES, LANES, LANES), F32)],
        scratch_shapes=[pltpu.VMEM((C + 8, sd), F32)] + [pltpu.VMEM((C, dim), F32)] * 7
                       + [pltpu.VMEM((n_pairs, C, LANES), F32), pltpu.VMEM((n_pairs, C, 2 * C), F32),
                          pltpu.VMEM((n_pairs, C, 4 * C), BF16), pltpu.VMEM((n_pairs, C, LANES), F32),
                          pltpu.VMEM((n_pairs, 2 * C, LANES), BF16), pltpu.VMEM((n_pairs, 8, LANES), F32)],
        compiler_params=_cparams(2),
        name="rwkv_prompt",
    )(proj, row(shift_mu), row(decay_w0), w2, row(aaa_a0), a2, gate_g2.astype(BF16), row(k_k), row(k_a),
      row(r_k), row(lnx_g), row(lnx_b), ones2)
    s = s.reshape(b, dim // LANES, 2, HEAD, 2, HEAD)
    s = jnp.stack([s[:, :, 0, :, 0, :], s[:, :, 1, :, 1, :]], axis=2).reshape(b, n_heads, HEAD, HEAD)
    return y, s


def _mix_kernel(x_ref, ym_ref, yr_ref, gate_ref, wm_ref, wr_ref, wo_ref, nf_ref, wq_ref, x2_ref, q_ref):
    d = x_ref.shape[1]
    y_m = _dot(ym_ref[...].astype(BF16), wm_ref[...])
    y_r = _dot(yr_ref[...].astype(BF16), wr_ref[...])
    mix = _sigmoid(gate_ref[:, 0:d]) * y_m + _sigmoid(gate_ref[:, d:2 * d]) * y_r
    x2 = x_ref[...] + _dot(mix.astype(BF16), wo_ref[...])
    x2_ref[...] = x2
    q_ref[...] = _dot(_rms(x2, nf_ref[...]).astype(BF16), wq_ref[...]).astype(q_ref.dtype)


def mix_and_query(x, ym, yr, gates, w_out_ssm, w_out_rwkv, w_out, norm_ffn, peer_wq, tm):
    m, d = x.shape
    tm = min(tm, m)
    assert m % tm == 0
    rows = lambda n: pl.BlockSpec((tm, n), lambda i: (i, 0))
    full = lambda a: pl.BlockSpec(a.shape, lambda i: (0, 0))
    nq = peer_wq.shape[1]
    args = (x, ym, yr, gates, w_out_ssm, w_out_rwkv, w_out, norm_ffn.reshape(1, -1), peer_wq)
    return pl.pallas_call(
        _mix_kernel,
        grid=(m // tm,),
        in_specs=[rows(d), rows(ym.shape[1]), rows(yr.shape[1]), rows(2 * d)] + [full(a) for a in args[4:]],
        out_specs=[rows(d), rows(nq)],
        out_shape=[jax.ShapeDtypeStruct((m, d), F32), jax.ShapeDtypeStruct((m, nq), BF16)],
        compiler_params=_cparams(1),
        name="mix_and_query",
    )(*args)


PEER_TOPK = 16
N_KEYS = 128


def _erf_gelu(x):
    return 0.5 * x * (1.0 + lax.erf(x * np.float32(1.0 / np.sqrt(2.0))))


NOT_TOP = 64.0


def _kth_largest(s, k, want_rank=False):
    tops = []
    rank = jnp.full(s.shape, NOT_TOP, F32)
    for r in range(k):
        m = jnp.max(s, axis=0, keepdims=True)
        tops.append(m)
        hit = s == m
        if want_rank:
            rank = jnp.where(hit, float(r), rank)
        s = jnp.where(hit, -jnp.inf, s)
    return (tops, rank) if want_rank else tops


def _peer_kernel(x2_ref, q_ref, p_ref, k1_ref, k2_ref, u_ref, vt_ref, nf_ref, npl_ref, wg_ref, wp_ref, nfin_ref,
                 y_ref, hb_s, n1_s, c_s, rk2_s, d_s, act0_s, act1_s, w0_s, w1_s, acc_s, *, n_heads, n_e):
    j = pl.program_id(1)
    tt = x2_ref.shape[0]
    ec = u_ref.shape[0]
    qd = N_KEYS
    assert ec == 8 * N_KEYS

    @pl.when(j == 0)
    def _():
        hb_s[...] = _rms(x2_ref[...], nf_ref[...]).T.astype(BF16)
        acc_s[...] = jnp.zeros(acc_s.shape, F32)
        act1_s[...] = jnp.zeros(act1_s.shape, F32)
        w0_s[...] = jnp.zeros(w0_s.shape, BF16)
        sub8 = lax.broadcasted_iota(jnp.int32, (8, LANES), 0)
        for h in range(n_heads):
            for ts in range(tt // LANES):
                tok = slice(ts * LANES, (ts + 1) * LANES)
                q1 = q_ref[tok, (2 * h) * qd:(2 * h + 1) * qd]
                q2 = q_ref[tok, (2 * h + 1) * qd:(2 * h + 2) * qd]
                s1 = _dot_nt(k1_ref[h], q1)
                s2 = _dot_nt(k2_ref[h], q2)
                v1 = _kth_largest(s1, PEER_TOPK)
                v2, rank2 = _kth_largest(s2, PEER_TOPK, want_rank=True)
                v2lo = jnp.concatenate(v2[0:8], axis=0)
                pieces = [v1[0] + v2lo, v1[0] + jnp.concatenate(v2[8:16], axis=0), v1[1] + v2lo]
                for k1 in range(2, 8):
                    pieces.append(jnp.where(sub8 < PEER_TOPK // (k1 + 1), v1[k1] + v2lo, -jnp.inf))
                pieces.append(jnp.concatenate(v1[8:16], axis=0) + v2[0])
                cand = jnp.concatenate(pieces, axis=0)
                top = _kth_largest(cand, PEER_TOPK)
                z = sum(jnp.exp(t - top[0]) for t in top)
                th = top[PEER_TOPK - 1]
                n1_s[h, :, tok] = sum(jnp.where(s1 + v >= th, 1.0, 0.0) for v in v2)
                c_s[h, :, tok] = jnp.exp(s1 - v1[0]) / z
                rk2_s[h, :, tok] = rank2.astype(BF16)
                d_s[h, :, tok] = jnp.exp(s2 - v2[0]).astype(BF16)

    def step(act_w, act_r, w_w, w_r):
        b_valid = (j >= 1) & (j <= n_e)
        i_rows = pl.ds(pl.multiple_of(jnp.clip(j - 1, 0, n_e - 1) * 8, 8), 8)
        jh = N_KEYS // 2
        n_blocks = (tt // LANES) * (N_KEYS // jh)
        d = vt_ref.shape[0]
        for ts in range(tt // LANES):
            tok = slice(ts * LANES, (ts + 1) * LANES)
            for jb in range(N_KEYS // jh):
                blk = ts * (N_KEYS // jh) + jb
                ra = slice(blk * (ec // n_blocks), (blk + 1) * (ec // n_blocks))
                rc = slice(blk * (d // n_blocks), (blk + 1) * (d // n_blocks))
                act_w[ra, :] = _dot(u_ref[ra, :], hb_s[...])
                acc_s[rc, :] += _dot(vt_ref[rc, :], w_r[...])
                jrows = slice(jb * jh, (jb + 1) * jh)
                acc = [jnp.zeros((jh, LANES), BF16)] * 8
                for h in range(n_heads):
                    rk, dd = rk2_s[h, jrows, tok], d_s[h, jrows, tok]
                    n1b, cb = n1_s[h, i_rows, tok], c_s[h, i_rows, tok]
                    for il in range(8):
                        sel = rk < n1b[il:il + 1].astype(BF16)
                        acc[il] = acc[il] + jnp.where(sel, dd, 0.0) * cb[il:il + 1].astype(BF16)
                for il in range(8):
                    rows = slice(il * N_KEYS + jb * jh, il * N_KEYS + (jb + 1) * jh)
                    w = acc[il] * _erf_gelu(act_r[rows, tok]).astype(BF16)
                    w_w[rows, tok] = jnp.where(b_valid, w, 0.0)

    @pl.when(j % 2 == 0)
    def _():
        step(act0_s, act1_s, w1_s, w0_s)

    @pl.when(j % 2 == 1)
    def _():
        step(act1_s, act0_s, w0_s, w1_s)

    @pl.when(j == n_e + 1)
    def _():
        x3 = x2_ref[...] + acc_s[...].T
        gate = _sigmoid(_dot(_rms(x3, npl_ref[...]).astype(BF16), wg_ref[...]))
        x4 = x3 + gate * _dot(p_ref[...].astype(BF16), wp_ref[...])
        y_ref[...] = _rms(x4, nfin_ref[...])


def peer_ple_final(x2, q, p, peer_k1, peer_k2, peer_u, peer_vt, norm_ffn, norm_ple, w_ple_gate, w_ple_proj,
                   norm_final, tt, ec):
    t, d = x2.shape
    n_heads = peer_k1.shape[0]
    n_exp = peer_u.shape[0]
    tt = min(tt, t)
    assert t % tt == 0 and n_exp % ec == 0 and tt % LANES == 0 and ec % N_KEYS == 0
    assert peer_k1.shape[1:] == (N_KEYS, N_KEYS) and n_exp == N_KEYS * N_KEYS
    rows = lambda n: pl.BlockSpec((tt, n), lambda i, j: (i, 0))
    full = lambda a: pl.BlockSpec(a.shape, lambda i, j: (0,) * a.ndim)
    row = lambda v: v.reshape(1, -1)
    consts = (row(norm_ffn), row(norm_ple), w_ple_gate, w_ple_proj, row(norm_final))
    n_e = n_exp // ec
    hs = lambda dt: pltpu.VMEM((n_heads, N_KEYS, tt), dt)
    return pl.pallas_call(
        functools.partial(_peer_kernel, n_heads=n_heads, n_e=n_e),
        grid=(t // tt, n_e + 2),
        in_specs=[rows(d), rows(q.shape[1]), rows(p.shape[1]), full(peer_k1), full(peer_k2),
                  pl.BlockSpec((ec, d), lambda i, j: (jnp.minimum(j, n_e - 1), 0)),
                  pl.BlockSpec((d, ec), lambda i, j: (0, jnp.clip(j - 2, 0, n_e - 1)))]
                 + [full(a) for a in consts],
        out_specs=rows(d),
        out_shape=jax.ShapeDtypeStruct((t, d), F32),
        scratch_shapes=[pltpu.VMEM((d, tt), BF16), hs(F32), hs(F32), hs(BF16), hs(BF16),
                        pltpu.VMEM((ec, tt), F32), pltpu.VMEM((ec, tt), F32),
                        pltpu.VMEM((ec, tt), BF16), pltpu.VMEM((ec, tt), BF16), pltpu.VMEM((d, tt), F32)],
        compiler_params=_cparams(2),
        name="peer_ple_final",
    )(x2, q, p, peer_k1, peer_k2, peer_u, peer_vt, *consts)


def _as_column(x_row):
    n = x_row.shape[1]
    eye = lax.broadcasted_iota(jnp.int32, (n, n), 0) == lax.broadcasted_iota(jnp.int32, (n, n), 1)
    return jnp.sum(jnp.where(eye, jnp.broadcast_to(x_row, (n, n)), 0.0), axis=1, keepdims=True)


def _rows8(x_row):
    return jnp.broadcast_to(x_row, (8, x_row.shape[1]))


def _ssd_step_kernel(xbc_ref, z_ref, dt_ref, conv_ref, h_ref, convw_ref, convb_ref, dtb_ref, alog_ref, dskip_ref,
                     normg_ref, e_ref, y_ref, ho_ref, *, n_heads):
    d_inner = n_heads * HEAD
    gw = d_inner // SSM_GROUPS
    cs = conv_ref[0]
    conv = convb_ref[...] + xbc_ref[0] * convw_ref[CONV_WIDTH - 1:CONV_WIDTH, :]
    for k in range(CONV_WIDTH - 1):
        conv = conv + cs[k:k + 1, :] * convw_ref[k:k + 1, :]
    act = conv * _sigmoid(conv)
    dt = _softplus(dt_ref[0] + dtb_ref[...])
    da = jnp.exp(dt * -jnp.exp(alog_ref[...]))
    e = e_ref[...]
    dt_x = _dot_exact_rhs(_rows8(dt), e)[0:1]
    da_x = _dot_exact_rhs(_rows8(da), e)[0:1]
    for g in range(SSM_GROUPS):
        cols = slice(g * gw, (g + 1) * gw)
        xg = act[:, cols]
        bg = act[:, d_inner + g * SSM_STATE:d_inner + (g + 1) * SSM_STATE]
        cg = act[:, d_inner + (SSM_GROUPS + g) * SSM_STATE:d_inner + (SSM_GROUPS + g + 1) * SSM_STATE]
        hn = _as_column(da_x[:, cols]) * h_ref[0, cols, :] + _as_column(xg * dt_x[:, cols]) * bg
        ho_ref[0, cols, :] = hn
        y = _dot_nt(_rows8(cg).astype(BF16), hn.astype(BF16))[0:1] + xg * dskip_ref[:, cols]
        zg = z_ref[0, :, cols]
        y = y * (zg * _sigmoid(zg))
        y_ref[0, :, cols] = _rms(y, normg_ref[:, cols])


def ssd_step(u, state_conv, state_ssm, conv_w, conv_b, dt_bias, a_log, d_skip, ssm_norm, n_heads):
    b = u.shape[0]
    d_inner = n_heads * HEAD
    conv_dim = d_inner + 2 * SSM_GROUPS * SSM_STATE
    pad = lambda v: jnp.pad(v.reshape(1, -1), ((0, 0), (0, LANES - v.size)))
    e_bf = jnp.asarray(np.arange(d_inner)[None, :] // HEAD == np.arange(LANES)[:, None], BF16)
    full = lambda shape: pl.BlockSpec(shape, lambda i: (0,) * len(shape))
    u3 = u.reshape(b, 1, -1)
    y, h = pl.pallas_call(
        functools.partial(_ssd_step_kernel, n_heads=n_heads),
        grid=(b,),
        in_specs=[pl.BlockSpec((1, 1, conv_dim), lambda i: (i, 0, 0)),
                  pl.BlockSpec((1, 1, d_inner), lambda i: (i, 0, conv_dim // d_inner)),
                  pl.BlockSpec((1, 1, LANES), lambda i: (i, 0, (conv_dim + d_inner) // LANES)),
                  pl.BlockSpec((1, CONV_WIDTH - 1, conv_dim), lambda i: (i, 0, 0)),
                  pl.BlockSpec((1, d_inner, SSM_STATE), lambda i: (i, 0, 0)),
                  full((CONV_WIDTH, conv_dim)), full((1, conv_dim)), full((1, LANES)), full((1, LANES)),
                  full((1, d_inner)), full((1, d_inner)), full((LANES, d_inner))],
        out_specs=[pl.BlockSpec((1, 1, d_inner), lambda i: (i, 0, 0)),
                   pl.BlockSpec((1, d_inner, SSM_STATE), lambda i: (i, 0, 0))],
        out_shape=[jax.ShapeDtypeStruct((b, 1, d_inner), F32),
                   jax.ShapeDtypeStruct((b, d_inner, SSM_STATE), F32)],
        compiler_params=_cparams(1),
        name="ssd_step",
    )(u3, u3, u3, state_conv, state_ssm.reshape(b, d_inner, SSM_STATE), conv_w, conv_b.reshape(1, -1),
      pad(dt_bias), pad(a_log), jnp.repeat(d_skip, HEAD).reshape(1, -1), ssm_norm.reshape(1, -1), e_bf)
    return y.reshape(b, d_inner), h


def _rwkv_step_features_kernel(p_ref, prev_ref, mu_ref, w0_ref, w2_ref, a0_ref, a2_ref, g2_ref, kk_ref, ka_ref,
                               ones2_ref, o_ref, *, dim):
    r, k, v, kkf, a, lw, g = _rwkv_features(p_ref[...], prev_ref[...], mu_ref, w0_ref, w2_ref, a0_ref, a2_ref,
                                            g2_ref, kk_ref, ka_ref, dim)
    for j in range(dim // LANES):
        cols = slice(j * LANES, (j + 1) * LANES)
        kj = kkf[:, cols]
        o_ref[3, :, cols] = kj * lax.rsqrt(_dot_exact_rhs(kj * kj, ones2_ref[...]) + 1e-12)
    o_ref[0], o_ref[1], o_ref[2], o_ref[4], o_ref[5], o_ref[6] = r, k, v, a, jnp.exp(lw), g


def _rwkv_step_kernel(f_ref, s_ref, rk_ref, lng_ref, lnb_ref, y_ref, so_ref, y_s):
    n_heads = s_ref.shape[1]
    eye = lax.broadcasted_iota(jnp.int32, (HEAD, HEAD), 0) == lax.broadcasted_iota(jnp.int32, (HEAD, HEAD), 1)
    for h in range(n_heads):
        row = lambda i: f_ref[i, 0, h:h + 1, :]
        r, k, v, kk, a, w = (row(i) for i in range(6))
        s = s_ref[0, h]
        sa = jnp.sum(s * -kk, axis=1, keepdims=True)
        v_col = jnp.sum(jnp.where(eye, jnp.broadcast_to(v, (HEAD, HEAD)), 0.0), axis=1, keepdims=True)
        sn = s * w + sa * (kk * a) + v_col * k
        so_ref[0, h] = sn
        y_s[h:h + 1, :] = _dot_nt(_rows8(r).astype(BF16), sn.astype(BF16))[0:1]
    y = y_s[...]
    r, k, v, g = f_ref[0, 0], f_ref[1, 0], f_ref[2, 0], f_ref[6, 0]
    d = y - jnp.mean(y, axis=-1, keepdims=True)
    var = jnp.mean(d * d, axis=-1, keepdims=True)
    yn = d * lax.rsqrt(var + LN_X_EPS) * lng_ref[...] + lnb_ref[...]
    bonus = jnp.sum(r * k * rk_ref[...], axis=-1, keepdims=True) * v
    y_ref[0] = (yn + bonus) * g


def rwkv_step(proj, shift_prev, state_wkv, shift_mu, decay_w0, decay_w2, aaa_a0, aaa_a2, gate_g2, k_k, k_a, r_k,
              lnx_g, lnx_b):
    b, sd = proj.shape
    dim = decay_w0.size
    n_heads = dim // HEAD
    dl, al, gl = decay_w2.shape[0], aaa_a2.shape[0], gate_g2.shape[0]
    assert dl + al == LANES and gl == LANES and sd == 3 * dim + 2 * LANES
    w2 = jnp.concatenate([decay_w2, jnp.zeros((al, dim), F32)], axis=0).astype(BF16)
    a2 = jnp.concatenate([jnp.zeros((dl, dim), F32), aaa_a2], axis=0).astype(BF16)
    ones2 = jnp.asarray(np.arange(LANES)[:, None] // HEAD == np.arange(LANES)[None, :] // HEAD, BF16)
    row = lambda v: v.reshape(1, -1)
    args = (proj, shift_prev, row(shift_mu), row(decay_w0), w2, row(aaa_a0), a2, gate_g2.astype(BF16), row(k_k),
            row(k_a), ones2)
    feats = pl.pallas_call(
        functools.partial(_rwkv_step_features_kernel, dim=dim),
        grid=(1,),
        in_specs=[pl.BlockSpec(a.shape, lambda i: (0, 0)) for a in args],
        out_specs=pl.BlockSpec((7, b, dim), lambda i: (0, 0, 0)),
        out_shape=jax.ShapeDtypeStruct((7, b, dim), F32),
        compiler_params=_cparams(1),
        name="rwkv_step_features",
    )(*args)
    hv = lambda v: v.reshape(n_heads, HEAD)
    full = pl.BlockSpec((n_heads, HEAD), lambda i: (0, 0))
    y, s = pl.pallas_call(
        _rwkv_step_kernel,
        grid=(b,),
        in_specs=[pl.BlockSpec((7, 1, n_heads, HEAD), lambda i: (0, i, 0, 0)),
                  pl.BlockSpec((1, n_heads, HEAD, HEAD), lambda i: (i, 0, 0, 0)), full, full, full],
        out_specs=[pl.BlockSpec((1, n_heads, HEAD), lambda i: (i, 0, 0)),
                   pl.BlockSpec((1, n_heads, HEAD, HEAD), lambda i: (i, 0, 0, 0))],
        out_shape=[jax.ShapeDtypeStruct((b, n_heads, HEAD), F32),
                   jax.ShapeDtypeStruct((b, n_heads, HEAD, HEAD), F32)],
        scratch_shapes=[pltpu.VMEM((n_heads, HEAD), F32)],
        compiler_params=_cparams(1),
        name="rwkv_step",
    )(feats.reshape(7, b, n_heads, HEAD), state_wkv, hv(r_k), hv(lnx_g), hv(lnx_b))
    return y.reshape(b, dim), s


def _layer(x, p, states, wts, n_ssm_heads):
    b, l, d = x.shape
    xt = x.reshape(b * l, d)
    g_mix = wts['norm_mix'].reshape(1, -1)
    tm = 1024
    u_ssm = norm_matmul(xt, g_mix, wts['w_ssm'], tm, wts['w_ssm'].shape[1] // 7)
    u_rwkv = norm_matmul(xt, g_mix, wts['w_shift'], tm, wts['w_shift'].shape[1] // 2)
    u_gate = norm_matmul(xt, g_mix, wts['w_gates'], tm, wts['w_gates'].shape[1] // 2)
    d_inner = n_ssm_heads * HEAD
    conv_dim = d_inner + 2 * SSM_GROUPS * SSM_STATE
    ssd_w = (wts['conv_w'], wts['conv_b'], wts['dt_bias'], wts['a_log'], wts['d_skip'], wts['ssm_norm'])
    rwkv_w = tuple(wts[k] for k in ('shift_mu', 'decay_w0', 'decay_w2', 'aaa_a0', 'aaa_a2', 'gate_g2', 'k_k', 'k_a',
                                    'r_k', 'lnx_g', 'lnx_b'))
    if states is None:
        ym, ssm_new = ssd_prompt(u_ssm.reshape(b, l, -1), *ssd_w, n_ssm_heads)
        ym = ym.reshape(b * l, d_inner)
        conv_new = u_ssm.reshape(b, l, -1)[:, l - (CONV_WIDTH - 1):, :conv_dim]
        yr, wkv_new = rwkv_prompt(u_rwkv.reshape(b, l, -1), *rwkv_w)
        yr = yr.reshape(b * l, -1)
        shift_new = u_rwkv.reshape(b, l, -1)[:, l - 1]
    else:
        conv_prev, ssm_prev, wkv_prev, shift_prev = states
        ym, ssm_new = ssd_step(u_ssm, conv_prev, ssm_prev, *ssd_w, n_ssm_heads)
        conv_new = jnp.concatenate([conv_prev[:, 1:], u_ssm[:, None, :conv_dim]], axis=1)
        yr, wkv_new = rwkv_step(u_rwkv, shift_prev, wkv_prev, *rwkv_w)
        shift_new = u_rwkv
    x2, q = mix_and_query(xt, ym, yr, u_gate, wts['w_out_ssm'], wts['w_out_rwkv'], wts['w_out'], wts['norm_ffn'],
                          wts['peer_wq'], 512)
    y = peer_ple_final(x2, q, p.reshape(b * l, -1), wts['peer_k1'], wts['peer_k2'], wts['peer_u'], wts['peer_vt'],
                       wts['norm_ffn'], wts['norm_ple'], wts['w_ple_gate'], wts['w_ple_proj'], wts['norm_final'],
                       512, 8 * N_KEYS)
    return (y.reshape(b, l, d), ssm_new.reshape(b, n_ssm_heads, HEAD, SSM_STATE), conv_new, wkv_new, shift_new)


def kernel(x_prompt, x_sample, p_prompt, p_sample, state_ssm, state_conv, state_wkv, state_shift, norm_mix, w_in,
           conv_w, conv_b, dt_bias, a_log, d_skip, ssm_norm, w_out_ssm, shift_mu, decay_w0, decay_w2, aaa_a0, aaa_a2,
           gate_g2, k_k, k_a, r_k, lnx_g, lnx_b, w_out_rwkv, w_out, norm_ffn, peer_wq, peer_k1, peer_k2, peer_u,
           peer_v, norm_ple, w_ple_gate, w_ple_proj, norm_final):
    depth = w_in.shape[0]
    assert depth == 1, "single-layer trunk"
    d_model = x_prompt.shape[-1]
    n_ssm_heads = dt_bias.shape[1]
    d_inner = n_ssm_heads * HEAD
    conv_dim = conv_w.shape[2]
    shift_dim = shift_mu.shape[1]
    bf = lambda a: a.astype(BF16)
    o = np.cumsum([0, d_inner, conv_dim, n_ssm_heads, shift_dim, d_model, d_model])
    wi = w_in[0]
    wts = {
        'w_ssm': bf(jnp.concatenate([wi[:, o[1]:o[2]], wi[:, o[0]:o[1]], wi[:, o[2]:o[3]],
                                     jnp.zeros((d_model, LANES - n_ssm_heads), F32)], axis=1)),
        'w_shift': bf(wi[:, o[3]:o[4]]),
        'w_gates': bf(wi[:, o[4]:o[6]]),
        'w_out_ssm': bf(w_out_ssm[0]), 'w_out_rwkv': bf(w_out_rwkv[0]), 'w_out': bf(w_out[0]),
        'peer_wq': bf(peer_wq[0]), 'peer_k1': bf(peer_k1[0]), 'peer_k2': bf(peer_k2[0]),
        'peer_u': bf(peer_u[0]), 'peer_vt': bf(peer_v[0]).T,
        'w_ple_gate': bf(w_ple_gate[0]), 'w_ple_proj': bf(w_ple_proj[0]), 'norm_final': norm_final,
    }
    for name, val in (('norm_mix', norm_mix), ('conv_w', conv_w), ('conv_b', conv_b), ('dt_bias', dt_bias),
                      ('a_log', a_log), ('d_skip', d_skip), ('ssm_norm', ssm_norm), ('shift_mu', shift_mu),
                      ('decay_w0', decay_w0), ('decay_w2', decay_w2), ('aaa_a0', aaa_a0), ('aaa_a2', aaa_a2),
                      ('gate_g2', gate_g2), ('k_k', k_k), ('k_a', k_a), ('r_k', r_k), ('lnx_g', lnx_g),
                      ('lnx_b', lnx_b), ('norm_ffn', norm_ffn), ('norm_ple', norm_ple)):
        wts[name] = val[0]
    yp, ssm_p, conv_p, wkv_p, shift_p = _layer(x_prompt, p_prompt[0], None, wts, n_ssm_heads)
    ys, ssm_s, conv_s, wkv_s, shift_s = _layer(
        x_sample, p_sample[0], (state_conv[0], state_ssm[0], state_wkv[0], state_shift[0]), wts, n_ssm_heads)
    return (yp, ys, ssm_p[None], conv_p[None], wkv_p[None], shift_p[None],
            ssm_s[None], conv_s[None], wkv_s[None], shift_s[None])
```

```python
import functools

import numpy as np
import jax
import jax.numpy as jnp
from jax import lax
from jax.experimental import pallas as pl
from jax.experimental.pallas import tpu as pltpu

F32 = jnp.float32
BF16 = jnp.bfloat16

EPS = 1e-6
LN_X_EPS = 64e-5
HEAD = 64
SSM_STATE = 128
SSM_GROUPS = 8
CONV_WIDTH = 4
CHUNK = 128
HI_LEVELS = 1
LANES = 128
VMEM_LIMIT = 56 * 1024 * 1024


def _cparams(n_axes):
    return pltpu.CompilerParams(dimension_semantics=("arbitrary",) * n_axes,
                                vmem_limit_bytes=VMEM_LIMIT)


def _dot(a, b):
    return jnp.dot(a, b, preferred_element_type=F32)


def _dot_nt(a, b):
    return lax.dot_general(a, b, (((1,), (1,)), ((), ())), preferred_element_type=F32)


def _dot_tn(a, b):
    return lax.dot_general(a, b, (((0,), (0,)), ((), ())), preferred_element_type=F32)


def _split2(x):
    hi = x.astype(BF16)
    lo = (x - hi.astype(F32)).astype(BF16)
    return hi, lo


def _split3(x):
    x1 = x.astype(BF16)
    r = x - x1.astype(F32)
    x2 = r.astype(BF16)
    x3 = (r - x2.astype(F32)).astype(BF16)
    return x1, x2, x3


def _dot_hi(a, b, dot=_dot):
    a1, a2 = _split2(a)
    b1, b2 = _split2(b)
    return dot(a1, b1) + (dot(a1, b2) + dot(a2, b1))


def _dot_exact_rhs(a, e, passes=3):
    if passes == 2:
        a1, a2 = _split2(a)
        return _dot(a1, e) + _dot(a2, e)
    a1, a2, a3 = _split3(a)
    return _dot(a1, e) + (_dot(a2, e) + _dot(a3, e))


def _dot_exact_lhs(e, a):
    a1, a2, a3 = _split3(a)
    return _dot(e, a1) + (_dot(e, a2) + _dot(e, a3))


def _rms(x, g):
    return x * lax.rsqrt(jnp.mean(x * x, axis=-1, keepdims=True) + EPS) * g


def _sigmoid(x):
    return 1.0 / (1.0 + jnp.exp(-x))


def _softplus(x):
    return jnp.maximum(x, 0.0) + jnp.log1p(jnp.exp(-jnp.abs(x)))


def _tril(n, k=0, dtype=F32):
    r = lax.broadcasted_iota(jnp.int32, (n, n), 0)
    c = lax.broadcasted_iota(jnp.int32, (n, n), 1)
    return (c <= r + k)


def _norm_matmul_kernel(x_ref, g_ref, w_ref, o_ref, h_ref):
    @pl.when(pl.program_id(1) == 0)
    def _():
        h_ref[...] = _rms(x_ref[...], g_ref[...]).astype(BF16)

    o_ref[...] = _dot(h_ref[...], w_ref[...])


def norm_matmul(x, g, w, tm, tn):
    m, k = x.shape
    n = w.shape[1]
    tm = min(tm, m)
    assert m % tm == 0 and n % tn == 0, (m, tm, n, tn)
    return pl.pallas_call(
        _norm_matmul_kernel,
        grid=(m // tm, n // tn),
        in_specs=[pl.BlockSpec((tm, k), lambda i, j: (i, 0)),
                  pl.BlockSpec((1, k), lambda i, j: (0, 0)),
                  pl.BlockSpec((k, tn), lambda i, j: (0, j))],
        out_specs=pl.BlockSpec((tm, tn), lambda i, j: (i, j)),
        out_shape=jax.ShapeDtypeStruct((m, n), F32),
        scratch_shapes=[pltpu.VMEM((tm, k), BF16)],
        compiler_params=_cparams(2),
        name="norm_matmul",
    )(x, g, w)


def _ssd_chunk_kernel(xbc_ref, z_ref, dt_ref, convw_ref, convb_ref, dtb_ref, alog_ref, dskip_ref,
                      normg_ref, e_ref, et_ref, y_ref, h_ref, ext_ref, act_ref, *, n_heads):
    c = pl.program_id(1)
    C = xbc_ref.shape[1]
    d_inner = n_heads * HEAD
    gw = d_inner // SSM_GROUPS
    hpg = n_heads // SSM_GROUPS

    @pl.when(c == 0)
    def _():
        ext_ref[0:8, :] = jnp.zeros((8, ext_ref.shape[1]), F32)
        h_ref[...] = jnp.zeros(h_ref.shape, F32)

    @pl.when(c > 0)
    def _():
        ext_ref[0:8, :] = ext_ref[C:C + 8, :]

    ext_ref[8:C + 8, :] = xbc_ref[0]
    conv = convb_ref[...]
    for k in range(CONV_WIDTH):
        off = 8 - (CONV_WIDTH - 1) + k
        conv = conv + ext_ref[off:off + C, :] * convw_ref[k:k + 1, :]
    act_ref[...] = conv * _sigmoid(conv)

    dt = _softplus(dt_ref[0] + dtb_ref[...])
    a = -jnp.exp(alog_ref[...])
    tril = _tril(C)
    acum = _dot_exact_lhs(tril.astype(BF16), dt * a)
    acum_t = acum.T
    dt_t = dt.T
    e = e_ref[...]
    eacum_x = _dot_exact_rhs(jnp.exp(acum), e)
    wdec_x = _dot_exact_rhs(jnp.exp(acum[C - 1:C, :] - acum) * dt, e)
    dec_b = jnp.broadcast_to(jnp.exp(acum_t[:, C - 1:C]), (LANES, SSM_STATE))
    lane = lax.broadcasted_iota(jnp.int32, (C, gw), 1)

    for g in range(SSM_GROUPS):
        cols = slice(g * gw, (g + 1) * gw)
        xg = act_ref[:, cols]
        bg = act_ref[:, d_inner + g * SSM_STATE:d_inner + (g + 1) * SSM_STATE].astype(BF16)
        cg = act_ref[:, d_inner + (SSM_GROUPS + g) * SSM_STATE:
                     d_inner + (SSM_GROUPS + g + 1) * SSM_STATE].astype(BF16)
        cb = _dot_nt(cg, bg)
        y = xg * dskip_ref[:, cols]
        for r in range(hpg):
            h = g * hpg + r
            seg = acum[:, h:h + 1] - acum_t[h:h + 1, :]
            m = cb * jnp.exp(jnp.where(tril, seg, -jnp.inf)) * dt_t[h:h + 1, :]
            xm = jnp.where((lane >= r * HEAD) & (lane < (r + 1) * HEAD), xg, 0.0)
            y = y + _dot(m.astype(BF16), xm.astype(BF16))
        hg = h_ref[0, cols, :]
        y = y + _dot_nt(cg, hg.astype(BF16)) * eacum_x[:, cols]
        zg = z_ref[0, :, cols]
        y = y * (zg * _sigmoid(zg))
        y = _rms(y, normg_ref[:, cols])
        y_ref[0, :, cols] = y.astype(y_ref.dtype)
        dec = _dot_exact_lhs(et_ref[cols, :], dec_b)
        h_ref[0, cols, :] = dec * hg + _dot_tn((xg * wdec_x[:, cols]).astype(BF16), bg)


def ssd_prompt(u, conv_w, conv_b, dt_bias, a_log, d_skip, ssm_norm, n_heads):
    b, l, _ = u.shape
    d_inner = n_heads * HEAD
    conv_dim = d_inner + 2 * SSM_GROUPS * SSM_STATE
    C = CHUNK
    assert l % C == 0 and conv_dim % d_inner == 0
    pad = lambda v: jnp.pad(v.reshape(1, -1), ((0, 0), (0, LANES - v.size)))
    e = (np.arange(d_inner)[None, :] // HEAD == np.arange(LANES)[:, None])
    e_bf = jnp.asarray(e, BF16)
    et_bf = jnp.asarray(e.T, BF16)
    full = lambda shape: pl.BlockSpec(shape, lambda i, j: (0,) * len(shape))
    return pl.pallas_call(
        functools.partial(_ssd_chunk_kernel, n_heads=n_heads),
        grid=(b, l // C),
        in_specs=[pl.BlockSpec((1, C, conv_dim), lambda i, j: (i, j, 0)),
                  pl.BlockSpec((1, C, d_inner), lambda i, j: (i, j, conv_dim // d_inner)),
                  pl.BlockSpec((1, C, LANES), lambda i, j: (i, j, (conv_dim + d_inner) // LANES)),
                  full((CONV_WIDTH, conv_dim)), full((1, conv_dim)), full((1, LANES)), full((1, LANES)),
                  full((1, d_inner)), full((1, d_inner)), full((LANES, d_inner)), full((d_inner, LANES))],
        out_specs=[pl.BlockSpec((1, C, d_inner), lambda i, j: (i, j, 0)),
                   pl.BlockSpec((1, d_inner, SSM_STATE), lambda i, j: (i, 0, 0))],
        out_shape=[jax.ShapeDtypeStruct((b, l, d_inner), BF16),
                   jax.ShapeDtypeStruct((b, d_inner, SSM_STATE), F32)],
        scratch_shapes=[pltpu.VMEM((C + 8, conv_dim), F32), pltpu.VMEM((C, conv_dim), F32)],
        compiler_params=_cparams(2),
        name="ssd_prompt",
    )(u, u, u, conv_w, conv_b.reshape(1, -1), pad(dt_bias), pad(a_log),
      jnp.repeat(d_skip, HEAD).reshape(1, -1), ssm_norm.reshape(1, -1), e_bf, et_bf)


def _head_masks(shape):
    lane = lax.broadcasted_iota(jnp.int32, shape, 1)
    return lane < HEAD, lane >= HEAD


def _rwkv_features(proj, prev, mu_ref, w0_ref, w2_ref, a0_ref, a2_ref, g2_ref, kk_ref, ka_ref, dim):
    xs = proj + (prev - proj) * mu_ref[...]
    k = xs[:, dim:2 * dim]
    t_wa = xs[:, 3 * dim:3 * dim + LANES]
    xg = xs[:, 3 * dim + LANES:3 * dim + 2 * LANES]
    w = -_softplus(-(w0_ref[...] + _dot(jnp.tanh(t_wa).astype(BF16), w2_ref[...]))) - 0.5
    a = _sigmoid(a0_ref[...] + _dot(t_wa.astype(BF16), a2_ref[...]))
    g = _dot(_sigmoid(xg).astype(BF16), g2_ref[...])
    return (xs[:, 0:dim], k * (1.0 + (a - 1.0) * ka_ref[...]), xs[:, 2 * dim:3 * dim], k * kk_ref[...], a,
            -jnp.exp(w), g)


def _rwkv_chunk_kernel(p_ref, mu_ref, w0_ref, w2_ref, a0_ref, a2_ref, g2_ref, kk_ref, ka_ref, rk_ref,
                       lng_ref, lnb_ref, ones2_ref, o_ref, s_ref,
                       ext_ref, r_s, k_s, v_s, kkn_s, a_s, lw_s, g_s, x_s, n_s, q_s, yb_s, bkh_s, pc_s, *, dim):
    c = pl.program_id(1)
    C = p_ref.shape[1]
    n_pairs = dim // LANES

    @pl.when(c == 0)
    def _():
        ext_ref[0:8, :] = jnp.zeros((8, ext_ref.shape[1]), F32)
        s_ref[...] = jnp.zeros(s_ref.shape, F32)

    @pl.when(c > 0)
    def _():
        ext_ref[0:8, :] = ext_ref[C:C + 8, :]

    proj = p_ref[0]
    ext_ref[8:C + 8, :] = proj
    (r_s[...], k_s[...], v_s[...], kkn_s[...], a_s[...], lw_s[...], g_s[...]) = _rwkv_features(
        proj, ext_ref[7:C + 7, :], mu_ref, w0_ref, w2_ref, a0_ref, a2_ref, g2_ref, kk_ref, ka_ref, dim)

    tril = _tril(C)
    tril_strict = _tril(C, -1)
    tril_bf = tril.astype(BF16)
    ones2 = ones2_ref[...]
    m0, m1 = _head_masks((C, LANES))
    row = lax.broadcasted_iota(jnp.int32, (LANES, LANES), 0)
    col = lax.broadcasted_iota(jnp.int32, (LANES, LANES), 1)
    blockdiag = (row < HEAD) == (col < HEAD)
    n_levels = C.bit_length() - 1
    assert 1 << n_levels == C

    def by_head(x):
        return jnp.concatenate([jnp.where(m0, x, 0.0), jnp.where(m1, x, 0.0)], axis=0)

    def head_sum(x):
        s0 = jnp.sum(jnp.where(m0, x, 0.0), axis=1, keepdims=True)
        s1 = jnp.sum(jnp.where(m1, x, 0.0), axis=1, keepdims=True)
        return jnp.where(m0, s0, s1)

    for j in range(n_pairs):
        cols = slice(j * LANES, (j + 1) * LANES)
        lw = lw_s[:, cols]
        logp = _dot_exact_lhs(tril_bf, lw)
        logpc = logp[C - 1:C, :]
        ep, epinv = jnp.exp(logp), jnp.exp(-logp)
        epc = jnp.exp(logpc - logp)
        kkf = kkn_s[:, cols]
        kk = kkf * lax.rsqrt(head_sum(kkf * kkf) + 1e-12)
        aj, kj, vj, rj = a_s[:, cols], k_s[:, cols], v_s[:, cols], r_s[:, cols]
        bvec = kk * aj
        at = -kk * jnp.exp(logp - lw)
        rt = rj * ep
        bk = jnp.concatenate([bvec * epinv, kj * epinv], axis=0)
        bk_bf = bk.astype(BF16)
        as_ = _dot_nt(jnp.concatenate([at, rt], axis=0).astype(BF16), s_ref[0, j].astype(BF16))
        mab, mak, qab, qak = [], [], [], []
        for hm in (m0, m1):
            gm = _dot_hi(jnp.where(hm, at, 0.0), bk, _dot_nt)
            mab.append(jnp.where(tril_strict, gm[:, 0:C], 0.0))
            mak.append(jnp.where(tril_strict, gm[:, C:2 * C], 0.0))
            gq = _dot_nt(jnp.where(hm, rt, 0.0).astype(BF16), bk_bf)
            qab.append(jnp.where(tril, gq[:, 0:C], 0.0))
            qak.append(jnp.where(tril, gq[:, C:2 * C], 0.0))
        x_s[j] = as_[0:C] + _dot_hi(jnp.concatenate(mak, axis=1), by_head(vj))
        n_s[j] = jnp.concatenate(mab, axis=1)
        q_s[j] = jnp.concatenate(qab + qak, axis=1).astype(BF16)
        yb_s[j] = as_[C:2 * C]
        bkh_s[j] = jnp.concatenate([bvec * epc, kj * epc], axis=0).astype(BF16)
        pc_s[j] = jnp.broadcast_to(jnp.exp(logpc), (8, LANES))

    zero = jnp.zeros((C, C), F32)
    for lvl in range(n_levels):
        for j in range(n_pairs):
            n = n_s[j]
            xs = by_head(x_s[j])
            if lvl < HI_LEVELS:
                x_s[j] += _dot_hi(n, xs)
            else:
                x_s[j] += _dot(n.astype(BF16), xs.astype(BF16))
            if lvl + 1 < n_levels:
                nd = jnp.concatenate([jnp.concatenate([n[:, 0:C], zero], axis=1),
                                      jnp.concatenate([zero, n[:, C:2 * C]], axis=1)], axis=0)
                if lvl + 1 < HI_LEVELS:
                    n_s[j] = _dot_hi(n, nd)
                else:
                    n_s[j] = _dot(n.astype(BF16), nd.astype(BF16))

    for j in range(n_pairs):
        cols = slice(j * LANES, (j + 1) * LANES)
        kj, vj, rj = k_s[:, cols], v_s[:, cols], r_s[:, cols]
        x = x_s[j]
        y = yb_s[j] + _dot(q_s[j], jnp.concatenate([by_head(x), by_head(vj)], axis=0).astype(BF16))
        upd = _dot_tn(jnp.concatenate([x, vj], axis=0).astype(BF16), bkh_s[j])
        s_ref[0, j] = pc_s[j, 0:1, :] * s_ref[0, j] + jnp.where(blockdiag, upd, 0.0)

        mean = head_sum(y) * (1.0 / HEAD)
        d = y - mean
        var = head_sum(d * d) * (1.0 / HEAD)
        yn = d * lax.rsqrt(var + LN_X_EPS) * lng_ref[:, cols] + lnb_ref[:, cols]
        bonus = head_sum(rj * kj * rk_ref[:, cols]) * vj
        o_ref[0, :, cols] = ((yn + bonus) * g_s[:, cols]).astype(o_ref.dtype)


def rwkv_prompt(proj, shift_mu, decay_w0, decay_w2, aaa_a0, aaa_a2, gate_g2, k_k, k_a, r_k, lnx_g, lnx_b):
    b, l, sd = proj.shape
    dim = decay_w0.size
    n_heads = dim // HEAD
    dl, al, gl = decay_w2.shape[0], aaa_a2.shape[0], gate_g2.shape[0]
    assert dl + al == LANES and gl == LANES and sd == 3 * dim + 2 * LANES and l % CHUNK == 0
    C = CHUNK
    n_pairs = dim // LANES
    w2 = jnp.concatenate([decay_w2, jnp.zeros((al, dim), F32)], axis=0).astype(BF16)
    a2 = jnp.concatenate([jnp.zeros((dl, dim), F32), aaa_a2], axis=0).astype(BF16)
    ones2 = jnp.asarray(np.arange(LANES)[:, None] // HEAD == np.arange(LANES)[None, :] // HEAD, BF16)
    row = lambda v: v.reshape(1, -1)
    full = lambda shape: pl.BlockSpec(shape, lambda i, j: (0,) * len(shape))
    vec = full((1, dim))
    y, s = pl.pallas_call(
        functools.partial(_rwkv_chunk_kernel, dim=dim),
        grid=(b, l // C),
        in_specs=[pl.BlockSpec((1, C, sd), lambda i, j: (i, j, 0)), full((1, sd)), vec, full((LANES, dim)), vec,
                  full((LANES, dim)), full((LANES, dim)), vec, vec, vec, vec, vec, full((LANES, LANES))],
        out_specs=[pl.BlockSpec((1, C, dim), lambda i, j: (i, j, 0)),
                   pl.BlockSpec((1, dim // LANES, LANES, LANES), lambda i, j: (i, 0, 0, 0))],
        out_shape=[jax.ShapeDtypeStruct((b, l, dim), BF16),
                   jax.ShapeDtypeStruct((b, dim // LANES, LANES, LANES), F32)],
        scratch_shapes=[pltpu.VMEM((C + 8, sd), F32)] + [pltpu.VMEM((C, dim), F32)] * 7
                       + [pltpu.VMEM((n_pairs, C, LANES), F32), pltpu.VMEM((n_pairs, C, 2 * C), F32),
                          pltpu.VMEM((n_pairs, C, 4 * C), BF16), pltpu.VMEM((n_pairs, C, LANES), F32),
                          pltpu.VMEM((n_pairs, 2 * C, LANES), BF16), pltpu.VMEM((n_pairs, 8, LANES), F32)],
        compiler_params=_cparams(2),
        name="rwkv_prompt",
    )(proj, row(shift_mu), row(decay_w0), w2, row(aaa_a0), a2, gate_g2.astype(BF16), row(k_k), row(k_a),
      row(r_k), row(lnx_g), row(lnx_b), ones2)
    s = s.reshape(b, dim // LANES, 2, HEAD, 2, HEAD)
    s = jnp.stack([s[:, :, 0, :, 0, :], s[:, :, 1, :, 1, :]], axis=2).reshape(b, n_heads, HEAD, HEAD)
    return y, s


def _mix_kernel(x_ref, ym_ref, yr_ref, gate_ref, wm_ref, wr_ref, wo_ref, nf_ref, wq_ref, x2_ref, q_ref):
    d = x_ref.shape[1]
    y_m = _dot(ym_ref[...].astype(BF16), wm_ref[...])
    y_r = _dot(yr_ref[...].astype(BF16), wr_ref[...])
    mix = _sigmoid(gate_ref[:, 0:d]) * y_m + _sigmoid(gate_ref[:, d:2 * d]) * y_r
    x2 = x_ref[...] + _dot(mix.astype(BF16), wo_ref[...])
    x2_ref[...] = x2
    q_ref[...] = _dot(_rms(x2, nf_ref[...]).astype(BF16), wq_ref[...]).astype(q_ref.dtype)


def mix_and_query(x, ym, yr, gates, w_out_ssm, w_out_rwkv, w_out, norm_ffn, peer_wq, tm):
    m, d = x.shape
    tm = min(tm, m)
    assert m % tm == 0
    rows = lambda n: pl.BlockSpec((tm, n), lambda i: (i, 0))
    full = lambda a: pl.BlockSpec(a.shape, lambda i: (0, 0))
    nq = peer_wq.shape[1]
    args = (x, ym, yr, gates, w_out_ssm, w_out_rwkv, w_out, norm_ffn.reshape(1, -1), peer_wq)
    return pl.pallas_call(
        _mix_kernel,
        grid=(m // tm,),
        in_specs=[rows(d), rows(ym.shape[1]), rows(yr.shape[1]), rows(2 * d)] + [full(a) for a in args[4:]],
        out_specs=[rows(d), rows(nq)],
        out_shape=[jax.ShapeDtypeStruct((m, d), F32), jax.ShapeDtypeStruct((m, nq), BF16)],
        compiler_params=_cparams(1),
        name="mix_and_query",
    )(*args)


PEER_TOPK = 16
N_KEYS = 128


def _erf_gelu(x):
    return 0.5 * x * (1.0 + lax.erf(x * np.float32(1.0 / np.sqrt(2.0))))


NOT_TOP = 64.0


def _kth_largest(s, k, want_rank=False):
    tops = []
    rank = jnp.full(s.shape, NOT_TOP, F32)
    for r in range(k):
        m = jnp.max(s, axis=0, keepdims=True)
        tops.append(m)
        hit = s == m
        if want_rank:
            rank = jnp.where(hit, float(r), rank)
        s = jnp.where(hit, -jnp.inf, s)
    return (tops, rank) if want_rank else tops


def _peer_kernel(x2_ref, q_ref, p_ref, k1_ref, k2_ref, u_ref, vt_ref, nf_ref, npl_ref, wg_ref, wp_ref, nfin_ref,
                 y_ref, hb_s, n1_s, c_s, rk2_s, d_s, act0_s, act1_s, w0_s, w1_s, acc_s, *, n_heads, n_e):
    j = pl.program_id(1)
    tt = x2_ref.shape[0]
    ec = u_ref.shape[0]
    qd = N_KEYS
    assert ec == 8 * N_KEYS

    @pl.when(j == 0)
    def _():
        hb_s[...] = _rms(x2_ref[...], nf_ref[...]).T.astype(BF16)
        acc_s[...] = jnp.zeros(acc_s.shape, F32)
        sub8 = lax.broadcasted_iota(jnp.int32, (8, LANES), 0)
        n_ts = tt // LANES

        def select_experts(unit, carry):
            h = unit // n_ts
            tok = pl.ds(pl.multiple_of((unit % n_ts) * LANES, LANES), LANES)
            q1 = q_ref[tok, pl.ds(pl.multiple_of(2 * h * qd, qd), qd)]
            q2 = q_ref[tok, pl.ds(pl.multiple_of((2 * h + 1) * qd, qd), qd)]
            s1 = _dot_nt(k1_ref[h], q1)
            s2 = _dot_nt(k2_ref[h], q2)
            v1 = _kth_largest(s1, PEER_TOPK)
            v2, rank2 = _kth_largest(s2, PEER_TOPK, want_rank=True)
            v2lo = jnp.concatenate(v2[0:8], axis=0)
            pieces = [v1[0] + v2lo, v1[0] + jnp.concatenate(v2[8:16], axis=0), v1[1] + v2lo]
            for k1 in range(2, 8):
                pieces.append(jnp.where(sub8 < PEER_TOPK // (k1 + 1), v1[k1] + v2lo, -jnp.inf))
            pieces.append(jnp.concatenate(v1[8:16], axis=0) + v2[0])
            cand = jnp.concatenate(pieces, axis=0)
            top = _kth_largest(cand, PEER_TOPK)
            z = sum(jnp.exp(t - top[0]) for t in top)
            th = top[PEER_TOPK - 1]
            n1_s[h, :, tok] = sum(jnp.where(s1 + v >= th, 1.0, 0.0) for v in v2)
            c_s[h, :, tok] = jnp.exp(s1 - v1[0]) / z
            rk2_s[h, :, tok] = rank2.astype(BF16)
            d_s[h, :, tok] = jnp.exp(s2 - v2[0]).astype(BF16)
            return carry

        lax.fori_loop(0, n_heads * n_ts, select_experts, 0, unroll=2)

    act = (act0_s, act1_s)
    wts = (w0_s, w1_s)

    def step(par, do_a, do_b, do_c):
        i_rows = pl.ds(pl.multiple_of(jnp.clip(j - 1, 0, n_e - 1) * 8, 8), 8)
        jh = N_KEYS // 2
        n_blocks = (tt // LANES) * (N_KEYS // jh)
        d = vt_ref.shape[0]
        for ts in range(tt // LANES):
            tok = slice(ts * LANES, (ts + 1) * LANES)
            for jb in range(N_KEYS // jh):
                blk = ts * (N_KEYS // jh) + jb
                ra = slice(blk * (ec // n_blocks), (blk + 1) * (ec // n_blocks))
                rc = slice(blk * (d // n_blocks), (blk + 1) * (d // n_blocks))
                if do_a:
                    act[par][ra, :] = _dot(u_ref[ra, :], hb_s[...])
                if do_c:
                    acc_s[rc, :] += _dot(vt_ref[rc, :], wts[par][...])
                if not do_b:
                    continue
                jrows = slice(jb * jh, (jb + 1) * jh)
                acc = [jnp.zeros((jh, LANES), BF16)] * 8
                for h in range(n_heads):
                    rk, dd = rk2_s[h, jrows, tok], d_s[h, jrows, tok]
                    n1b, cb = n1_s[h, i_rows, tok], c_s[h, i_rows, tok]
                    for il in range(8):
                        sel = rk < n1b[il:il + 1].astype(BF16)
                        acc[il] = acc[il] + jnp.where(sel, dd, 0.0) * cb[il:il + 1].astype(BF16)
                for il in range(8):
                    rows = slice(il * N_KEYS + jb * jh, il * N_KEYS + (jb + 1) * jh)
                    wts[1 - par][rows, tok] = acc[il] * _erf_gelu(act[1 - par][rows, tok]).astype(BF16)

    assert n_e % 2 == 0 and n_e >= 4
    pl.when(j == 0)(lambda: step(0, True, False, False))
    pl.when(j == 1)(lambda: step(1, True, True, False))
    pl.when((j >= 2) & (j < n_e) & (j % 2 == 0))(lambda: step(0, True, True, True))
    pl.when((j >= 2) & (j < n_e) & (j % 2 == 1))(lambda: step(1, True, True, True))
    pl.when(j == n_e)(lambda: step(0, False, True, True))
    pl.when(j == n_e + 1)(lambda: step(1, False, False, True))

    @pl.when(j == n_e + 1)
    def _():
        x3 = x2_ref[...] + acc_s[...].T
        gate = _sigmoid(_dot(_rms(x3, npl_ref[...]).astype(BF16), wg_ref[...]))
        x4 = x3 + gate * _dot(p_ref[...].astype(BF16), wp_ref[...])
        y_ref[...] = _rms(x4, nfin_ref[...])


def peer_ple_final(x2, q, p, peer_k1, peer_k2, peer_u, peer_vt, norm_ffn, norm_ple, w_ple_gate, w_ple_proj,
                   norm_final, tt, ec):
    t, d = x2.shape
    n_heads = peer_k1.shape[0]
    n_exp = peer_u.shape[0]
    tt = min(tt, t)
    assert t % tt == 0 and n_exp % ec == 0 and tt % LANES == 0 and ec % N_KEYS == 0
    assert peer_k1.shape[1:] == (N_KEYS, N_KEYS) and n_exp == N_KEYS * N_KEYS
    rows = lambda n: pl.BlockSpec((tt, n), lambda i, j: (i, 0))
    full = lambda a: pl.BlockSpec(a.shape, lambda i, j: (0,) * a.ndim)
    row = lambda v: v.reshape(1, -1)
    consts = (row(norm_ffn), row(norm_ple), w_ple_gate, w_ple_proj, row(norm_final))
    n_e = n_exp // ec
    hs = lambda dt: pltpu.VMEM((n_heads, N_KEYS, tt), dt)
    return pl.pallas_call(
        functools.partial(_peer_kernel, n_heads=n_heads, n_e=n_e),
        grid=(t // tt, n_e + 2),
        in_specs=[rows(d), rows(q.shape[1]), rows(p.shape[1]), full(peer_k1), full(peer_k2),
                  pl.BlockSpec((ec, d), lambda i, j: (jnp.minimum(j, n_e - 1), 0)),
                  pl.BlockSpec((d, ec), lambda i, j: (0, jnp.clip(j - 2, 0, n_e - 1)))]
                 + [full(a) for a in consts],
        out_specs=rows(d),
        out_shape=jax.ShapeDtypeStruct((t, d), F32),
        scratch_shapes=[pltpu.VMEM((d, tt), BF16), hs(F32), hs(F32), hs(BF16), hs(BF16),
                        pltpu.VMEM((ec, tt), F32), pltpu.VMEM((ec, tt), F32),
                        pltpu.VMEM((ec, tt), BF16), pltpu.VMEM((ec, tt), BF16), pltpu.VMEM((d, tt), F32)],
        compiler_params=_cparams(2),
        name="peer_ple_final",
    )(x2, q, p, peer_k1, peer_k2, peer_u, peer_vt, *consts)


def _as_column(x_row):
    n = x_row.shape[1]
    eye = lax.broadcasted_iota(jnp.int32, (n, n), 0) == lax.broadcasted_iota(jnp.int32, (n, n), 1)
    return jnp.sum(jnp.where(eye, jnp.broadcast_to(x_row, (n, n)), 0.0), axis=1, keepdims=True)


def _rows8(x_row):
    return jnp.broadcast_to(x_row, (8, x_row.shape[1]))


def _ssd_step_kernel(xbc_ref, z_ref, dt_ref, conv_ref, h_ref, convw_ref, convb_ref, dtb_ref, alog_ref, dskip_ref,
                     normg_ref, e_ref, y_ref, ho_ref, *, n_heads):
    d_inner = n_heads * HEAD
    gw = d_inner // SSM_GROUPS
    cs = conv_ref[0]
    conv = convb_ref[...] + xbc_ref[0] * convw_ref[CONV_WIDTH - 1:CONV_WIDTH, :]
    for k in range(CONV_WIDTH - 1):
        conv = conv + cs[k:k + 1, :] * convw_ref[k:k + 1, :]
    act = conv * _sigmoid(conv)
    dt = _softplus(dt_ref[0] + dtb_ref[...])
    da = jnp.exp(dt * -jnp.exp(alog_ref[...]))
    e = e_ref[...]
    dt_x = _dot_exact_rhs(_rows8(dt), e)[0:1]
    da_x = _dot_exact_rhs(_rows8(da), e)[0:1]
    for g in range(SSM_GROUPS):
        cols = slice(g * gw, (g + 1) * gw)
        xg = act[:, cols]
        bg = act[:, d_inner + g * SSM_STATE:d_inner + (g + 1) * SSM_STATE]
        cg = act[:, d_inner + (SSM_GROUPS + g) * SSM_STATE:d_inner + (SSM_GROUPS + g + 1) * SSM_STATE]
        hn = _as_column(da_x[:, cols]) * h_ref[0, cols, :] + _as_column(xg * dt_x[:, cols]) * bg
        ho_ref[0, cols, :] = hn
        y = _dot_nt(_rows8(cg).astype(BF16), hn.astype(BF16))[0:1] + xg * dskip_ref[:, cols]
        zg = z_ref[0, :, cols]
        y = y * (zg * _sigmoid(zg))
        y_ref[0, :, cols] = _rms(y, normg_ref[:, cols])


def ssd_step(u, state_conv, state_ssm, conv_w, conv_b, dt_bias, a_log, d_skip, ssm_norm, n_heads):
    b = u.shape[0]
    d_inner = n_heads * HEAD
    conv_dim = d_inner + 2 * SSM_GROUPS * SSM_STATE
    pad = lambda v: jnp.pad(v.reshape(1, -1), ((0, 0), (0, LANES - v.size)))
    e_bf = jnp.asarray(np.arange(d_inner)[None, :] // HEAD == np.arange(LANES)[:, None], BF16)
    full = lambda shape: pl.BlockSpec(shape, lambda i: (0,) * len(shape))
    u3 = u.reshape(b, 1, -1)
    y, h = pl.pallas_call(
        functools.partial(_ssd_step_kernel, n_heads=n_heads),
        grid=(b,),
        in_specs=[pl.BlockSpec((1, 1, conv_dim), lambda i: (i, 0, 0)),
                  pl.BlockSpec((1, 1, d_inner), lambda i: (i, 0, conv_dim // d_inner)),
                  pl.BlockSpec((1, 1, LANES), lambda i: (i, 0, (conv_dim + d_inner) // LANES)),
                  pl.BlockSpec((1, CONV_WIDTH - 1, conv_dim), lambda i: (i, 0, 0)),
                  pl.BlockSpec((1, d_inner, SSM_STATE), lambda i: (i, 0, 0)),
                  full((CONV_WIDTH, conv_dim)), full((1, conv_dim)), full((1, LANES)), full((1, LANES)),
                  full((1, d_inner)), full((1, d_inner)), full((LANES, d_inner))],
        out_specs=[pl.BlockSpec((1, 1, d_inner), lambda i: (i, 0, 0)),
                   pl.BlockSpec((1, d_inner, SSM_STATE), lambda i: (i, 0, 0))],
        out_shape=[jax.ShapeDtypeStruct((b, 1, d_inner), F32),
                   jax.ShapeDtypeStruct((b, d_inner, SSM_STATE), F32)],
        compiler_params=_cparams(1),
        name="ssd_step",
    )(u3, u3, u3, state_conv, state_ssm.reshape(b, d_inner, SSM_STATE), conv_w, conv_b.reshape(1, -1),
      pad(dt_bias), pad(a_log), jnp.repeat(d_skip, HEAD).reshape(1, -1), ssm_norm.reshape(1, -1), e_bf)
    return y.reshape(b, d_inner), h


def _rwkv_step_features_kernel(p_ref, prev_ref, mu_ref, w0_ref, w2_ref, a0_ref, a2_ref, g2_ref, kk_ref, ka_ref,
                               ones2_ref, o_ref, *, dim):
    r, k, v, kkf, a, lw, g = _rwkv_features(p_ref[...], prev_ref[...], mu_ref, w0_ref, w2_ref, a0_ref, a2_ref,
                                            g2_ref, kk_ref, ka_ref, dim)
    for j in range(dim // LANES):
        cols = slice(j * LANES, (j + 1) * LANES)
        kj = kkf[:, cols]
        o_ref[3, :, cols] = kj * lax.rsqrt(_dot_exact_rhs(kj * kj, ones2_ref[...]) + 1e-12)
    o_ref[0], o_ref[1], o_ref[2], o_ref[4], o_ref[5], o_ref[6] = r, k, v, a, jnp.exp(lw), g


def _rwkv_step_kernel(f_ref, s_ref, rk_ref, lng_ref, lnb_ref, y_ref, so_ref, y_s):
    n_heads = s_ref.shape[1]
    eye = lax.broadcasted_iota(jnp.int32, (HEAD, HEAD), 0) == lax.broadcasted_iota(jnp.int32, (HEAD, HEAD), 1)
    for h in range(n_heads):
        row = lambda i: f_ref[i, 0, h:h + 1, :]
        r, k, v, kk, a, w = (row(i) for i in range(6))
        s = s_ref[0, h]
        sa = jnp.sum(s * -kk, axis=1, keepdims=True)
        v_col = jnp.sum(jnp.where(eye, jnp.broadcast_to(v, (HEAD, HEAD)), 0.0), axis=1, keepdims=True)
        sn = s * w + sa * (kk * a) + v_col * k
        so_ref[0, h] = sn
        y_s[h:h + 1, :] = _dot_nt(_rows8(r).astype(BF16), sn.astype(BF16))[0:1]
    y = y_s[...]
    r, k, v, g = f_ref[0, 0], f_ref[1, 0], f_ref[2, 0], f_ref[6, 0]
    d = y - jnp.mean(y, axis=-1, keepdims=True)
    var = jnp.mean(d * d, axis=-1, keepdims=True)
    yn = d * lax.rsqrt(var + LN_X_EPS) * lng_ref[...] + lnb_ref[...]
    bonus = jnp.sum(r * k * rk_ref[...], axis=-1, keepdims=True) * v
    y_ref[0] = (yn + bonus) * g


def rwkv_step(proj, shift_prev, state_wkv, shift_mu, decay_w0, decay_w2, aaa_a0, aaa_a2, gate_g2, k_k, k_a, r_k,
              lnx_g, lnx_b):
    b, sd = proj.shape
    dim = decay_w0.size
    n_heads = dim // HEAD
    dl, al, gl = decay_w2.shape[0], aaa_a2.shape[0], gate_g2.shape[0]
    assert dl + al == LANES and gl == LANES and sd == 3 * dim + 2 * LANES
    w2 = jnp.concatenate([decay_w2, jnp.zeros((al, dim), F32)], axis=0).astype(BF16)
    a2 = jnp.concatenate([jnp.zeros((dl, dim), F32), aaa_a2], axis=0).astype(BF16)
    ones2 = jnp.asarray(np.arange(LANES)[:, None] // HEAD == np.arange(LANES)[None, :] // HEAD, BF16)
    row = lambda v: v.reshape(1, -1)
    args = (proj, shift_prev, row(shift_mu), row(decay_w0), w2, row(aaa_a0), a2, gate_g2.astype(BF16), row(k_k),
            row(k_a), ones2)
    feats = pl.pallas_call(
        functools.partial(_rwkv_step_features_kernel, dim=dim),
        grid=(1,),
        in_specs=[pl.BlockSpec(a.shape, lambda i: (0, 0)) for a in args],
        out_specs=pl.BlockSpec((7, b, dim), lambda i: (0, 0, 0)),
        out_shape=jax.ShapeDtypeStruct((7, b, dim), F32),
        compiler_params=_cparams(1),
        name="rwkv_step_features",
    )(*args)
    hv = lambda v: v.reshape(n_heads, HEAD)
    full = pl.BlockSpec((n_heads, HEAD), lambda i: (0, 0))
    y, s = pl.pallas_call(
        _rwkv_step_kernel,
        grid=(b,),
        in_specs=[pl.BlockSpec((7, 1, n_heads, HEAD), lambda i: (0, i, 0, 0)),
                  pl.BlockSpec((1, n_heads, HEAD, HEAD), lambda i: (i, 0, 0, 0)), full, full, full],
        out_specs=[pl.BlockSpec((1, n_heads, HEAD), lambda i: (i, 0, 0)),
                   pl.BlockSpec((1, n_heads, HEAD, HEAD), lambda i: (i, 0, 0, 0))],
        out_shape=[jax.ShapeDtypeStruct((b, n_heads, HEAD), F32),
                   jax.ShapeDtypeStruct((b, n_heads, HEAD, HEAD), F32)],
        scratch_shapes=[pltpu.VMEM((n_heads, HEAD), F32)],
        compiler_params=_cparams(1),
        name="rwkv_step",
    )(feats.reshape(7, b, n_heads, HEAD), state_wkv, hv(r_k), hv(lnx_g), hv(lnx_b))
    return y.reshape(b, dim), s


def _layer(x, p, states, wts, n_ssm_heads):
    b, l, d = x.shape
    xt = x.reshape(b * l, d)
    g_mix = wts['norm_mix'].reshape(1, -1)
    tm = 1024
    u_ssm = norm_matmul(xt, g_mix, wts['w_ssm'], tm, wts['w_ssm'].shape[1] // 7)
    u_rwkv = norm_matmul(xt, g_mix, wts['w_shift'], tm, wts['w_shift'].shape[1] // 2)
    u_gate = norm_matmul(xt, g_mix, wts['w_gates'], tm, wts['w_gates'].shape[1] // 2)
    d_inner = n_ssm_heads * HEAD
    conv_dim = d_inner + 2 * SSM_GROUPS * SSM_STATE
    ssd_w = (wts['conv_w'], wts['conv_b'], wts['dt_bias'], wts['a_log'], wts['d_skip'], wts['ssm_norm'])
    rwkv_w = tuple(wts[k] for k in ('shift_mu', 'decay_w0', 'decay_w2', 'aaa_a0', 'aaa_a2', 'gate_g2', 'k_k', 'k_a',
                                    'r_k', 'lnx_g', 'lnx_b'))
    if states is None:
        ym, ssm_new = ssd_prompt(u_ssm.reshape(b, l, -1), *ssd_w, n_ssm_heads)
        ym = ym.reshape(b * l, d_inner)
        conv_new = u_ssm.reshape(b, l, -1)[:, l - (CONV_WIDTH - 1):, :conv_dim]
        yr, wkv_new = rwkv_prompt(u_rwkv.reshape(b, l, -1), *rwkv_w)
        yr = yr.reshape(b * l, -1)
        shift_new = u_rwkv.reshape(b, l, -1)[:, l - 1]
    else:
        conv_prev, ssm_prev, wkv_prev, shift_prev = states
        ym, ssm_new = ssd_step(u_ssm, conv_prev, ssm_prev, *ssd_w, n_ssm_heads)
        conv_new = jnp.concatenate([conv_prev[:, 1:], u_ssm[:, None, :conv_dim]], axis=1)
        yr, wkv_new = rwkv_step(u_rwkv, shift_prev, wkv_prev, *rwkv_w)
        shift_new = u_rwkv
    x2, q = mix_and_query(xt, ym, yr, u_gate, wts['w_out_ssm'], wts['w_out_rwkv'], wts['w_out'], wts['norm_ffn'],
                          wts['peer_wq'], 512)
    y = peer_ple_final(x2, q, p.reshape(b * l, -1), wts['peer_k1'], wts['peer_k2'], wts['peer_u'], wts['peer_vt'],
                       wts['norm_ffn'], wts['norm_ple'], wts['w_ple_gate'], wts['w_ple_proj'], wts['norm_final'],
                       512, 8 * N_KEYS)
    return (y.reshape(b, l, d), ssm_new.reshape(b, n_ssm_heads, HEAD, SSM_STATE), conv_new, wkv_new, shift_new)


def kernel(x_prompt, x_sample, p_prompt, p_sample, state_ssm, state_conv, state_wkv, state_shift, norm_mix, w_in,
           conv_w, conv_b, dt_bias, a_log, d_skip, ssm_norm, w_out_ssm, shift_mu, decay_w0, decay_w2, aaa_a0, aaa_a2,
           gate_g2, k_k, k_a, r_k, lnx_g, lnx_b, w_out_rwkv, w_out, norm_ffn, peer_wq, peer_k1, peer_k2, peer_u,
           peer_v, norm_ple, w_ple_gate, w_ple_proj, norm_final):
    depth = w_in.shape[0]
    assert depth == 1, "single-layer trunk"
    d_model = x_prompt.shape[-1]
    n_ssm_heads = dt_bias.shape[1]
    d_inner = n_ssm_heads * HEAD
    conv_dim = conv_w.shape[2]
    shift_dim = shift_mu.shape[1]
    bf = lambda a: a.astype(BF16)
    o = np.cumsum([0, d_inner, conv_dim, n_ssm_heads, shift_dim, d_model, d_model])
    wi = w_in[0]
    wts = {
        'w_ssm': bf(jnp.concatenate([wi[:, o[1]:o[2]], wi[:, o[0]:o[1]], wi[:, o[2]:o[3]],
                                     jnp.zeros((d_model, LANES - n_ssm_heads), F32)], axis=1)),
        'w_shift': bf(wi[:, o[3]:o[4]]),
        'w_gates': bf(wi[:, o[4]:o[6]]),
        'w_out_ssm': bf(w_out_ssm[0]), 'w_out_rwkv': bf(w_out_rwkv[0]), 'w_out': bf(w_out[0]),
        'peer_wq': bf(peer_wq[0]), 'peer_k1': bf(peer_k1[0]), 'peer_k2': bf(peer_k2[0]),
        'peer_u': bf(peer_u[0]), 'peer_vt': bf(peer_v[0]).T,
        'w_ple_gate': bf(w_ple_gate[0]), 'w_ple_proj': bf(w_ple_proj[0]), 'norm_final': norm_final,
    }
    for name, val in (('norm_mix', norm_mix), ('conv_w', conv_w), ('conv_b', conv_b), ('dt_bias', dt_bias),
                      ('a_log', a_log), ('d_skip', d_skip), ('ssm_norm', ssm_norm), ('shift_mu', shift_mu),
                      ('decay_w0', decay_w0), ('decay_w2', decay_w2), ('aaa_a0', aaa_a0), ('aaa_a2', aaa_a2),
                      ('gate_g2', gate_g2), ('k_k', k_k), ('k_a', k_a), ('r_k', r_k), ('lnx_g', lnx_g),
                      ('lnx_b', lnx_b), ('norm_ffn', norm_ffn), ('norm_ple', norm_ple)):
        wts[name] = val[0]
    yp, ssm_p, conv_p, wkv_p, shift_p = _layer(x_prompt, p_prompt[0], None, wts, n_ssm_heads)
    ys, ssm_s, conv_s, wkv_s, shift_s = _layer(
        x_sample, p_sample[0], (state_conv[0], state_ssm[0], state_wkv[0], state_shift[0]), wts, n_ssm_heads)
    return (yp, ys, ssm_p[None], conv_p[None], wkv_p[None], shift_p[None],
            ssm_s[None], conv_s[None], wkv_s[None], shift_s[None])
```

```python
import functools

import numpy as np
import jax
import jax.numpy as jnp
from jax import lax
from jax.experimental import pallas as pl
from jax.experimental.pallas import tpu as pltpu

F32 = jnp.float32
BF16 = jnp.bfloat16

EPS = 1e-6
LN_X_EPS = 64e-5
HEAD = 64
SSM_STATE = 128
SSM_GROUPS = 8
CONV_WIDTH = 4
CHUNK = 128
LANES = 128
VMEM_LIMIT = 56 * 1024 * 1024


def _cparams(n_axes):
    return pltpu.CompilerParams(dimension_semantics=("arbitrary",) * n_axes,
                                vmem_limit_bytes=VMEM_LIMIT)


def _dot(a, b):
    return jnp.dot(a, b, preferred_element_type=F32)


def _dot_nt(a, b):
    return lax.dot_general(a, b, (((1,), (1,)), ((), ())), preferred_element_type=F32)


def _dot_tn(a, b):
    return lax.dot_general(a, b, (((0,), (0,)), ((), ())), preferred_element_type=F32)


def _split2(x):
    hi = x.astype(BF16)
    lo = (x - hi.astype(F32)).astype(BF16)
    return hi, lo


def _split3(x):
    x1 = x.astype(BF16)
    r = x - x1.astype(F32)
    x2 = r.astype(BF16)
    x3 = (r - x2.astype(F32)).astype(BF16)
    return x1, x2, x3


def _dot_exact_rhs(a, e, passes=3):
    if passes == 2:
        a1, a2 = _split2(a)
        return _dot(a1, e) + _dot(a2, e)
    a1, a2, a3 = _split3(a)
    return _dot(a1, e) + (_dot(a2, e) + _dot(a3, e))


def _dot_exact_lhs(e, a):
    a1, a2, a3 = _split3(a)
    return _dot(e, a1) + (_dot(e, a2) + _dot(e, a3))


def _rms(x, g):
    return x * lax.rsqrt(jnp.mean(x * x, axis=-1, keepdims=True) + EPS) * g


def _sigmoid(x):
    return 1.0 / (1.0 + jnp.exp(-x))


def _softplus(x):
    return jnp.maximum(x, 0.0) + jnp.log1p(jnp.exp(-jnp.abs(x)))


def _tril(n, k=0, dtype=F32):
    r = lax.broadcasted_iota(jnp.int32, (n, n), 0)
    c = lax.broadcasted_iota(jnp.int32, (n, n), 1)
    return (c <= r + k)


def _norm_matmul_kernel(x_ref, g_ref, w_ref, o_ref, h_ref):
    @pl.when(pl.program_id(1) == 0)
    def _():
        h_ref[...] = _rms(x_ref[...], g_ref[...]).astype(BF16)

    o_ref[...] = _dot(h_ref[...], w_ref[...])


def norm_matmul(x, g, w, tm, tn):
    m, k = x.shape
    n = w.shape[1]
    tm = min(tm, m)
    assert m % tm == 0 and n % tn == 0, (m, tm, n, tn)
    return pl.pallas_call(
        _norm_matmul_kernel,
        grid=(m // tm, n // tn),
        in_specs=[pl.BlockSpec((tm, k), lambda i, j: (i, 0)),
                  pl.BlockSpec((1, k), lambda i, j: (0, 0)),
                  pl.BlockSpec((k, tn), lambda i, j: (0, j))],
        out_specs=pl.BlockSpec((tm, tn), lambda i, j: (i, j)),
        out_shape=jax.ShapeDtypeStruct((m, n), F32),
        scratch_shapes=[pltpu.VMEM((tm, k), BF16)],
        compiler_params=_cparams(2),
        name="norm_matmul",
    )(x, g, w)


def _ssd_chunk_kernel(xbc_ref, z_ref, dt_ref, convw_ref, convb_ref, dtb_ref, alog_ref, dskip_ref,
                      normg_ref, e_ref, et_ref, y_ref, h_ref, ext_ref, act_ref, *, n_heads):
    c = pl.program_id(1)
    C = xbc_ref.shape[1]
    d_inner = n_heads * HEAD
    gw = d_inner // SSM_GROUPS
    hpg = n_heads // SSM_GROUPS

    @pl.when(c == 0)
    def _():
        ext_ref[0:8, :] = jnp.zeros((8, ext_ref.shape[1]), F32)
        h_ref[...] = jnp.zeros(h_ref.shape, F32)

    @pl.when(c > 0)
    def _():
        ext_ref[0:8, :] = ext_ref[C:C + 8, :]

    ext_ref[8:C + 8, :] = xbc_ref[0]
    conv = convb_ref[...]
    for k in range(CONV_WIDTH):
        off = 8 - (CONV_WIDTH - 1) + k
        conv = conv + ext_ref[off:off + C, :] * convw_ref[k:k + 1, :]
    act_ref[...] = conv * _sigmoid(conv)

    dt = _softplus(dt_ref[0] + dtb_ref[...])
    a = -jnp.exp(alog_ref[...])
    tril = _tril(C)
    acum = _dot_exact_lhs(tril.astype(BF16), dt * a)
    acum_t = acum.T
    dt_t = dt.T
    e = e_ref[...]
    eacum_x = _dot_exact_rhs(jnp.exp(acum), e)
    wdec_x = _dot_exact_rhs(jnp.exp(acum[C - 1:C, :] - acum) * dt, e)
    dec_b = jnp.broadcast_to(jnp.exp(acum_t[:, C - 1:C]), (LANES, SSM_STATE))
    lane = lax.broadcasted_iota(jnp.int32, (C, gw), 1)

    for g in range(SSM_GROUPS):
        cols = slice(g * gw, (g + 1) * gw)
        xg = act_ref[:, cols]
        bg = act_ref[:, d_inner + g * SSM_STATE:d_inner + (g + 1) * SSM_STATE].astype(BF16)
        cg = act_ref[:, d_inner + (SSM_GROUPS + g) * SSM_STATE:
                     d_inner + (SSM_GROUPS + g + 1) * SSM_STATE].astype(BF16)
        cb = _dot_nt(cg, bg)
        y = xg * dskip_ref[:, cols]
        for r in range(hpg):
            h = g * hpg + r
            seg = acum[:, h:h + 1] - acum_t[h:h + 1, :]
            m = cb * jnp.exp(jnp.where(tril, seg, -jnp.inf)) * dt_t[h:h + 1, :]
            xm = jnp.where((lane >= r * HEAD) & (lane < (r + 1) * HEAD), xg, 0.0)
            y = y + _dot(m.astype(BF16), xm.astype(BF16))
        hg = h_ref[0, cols, :]
        y = y + _dot_nt(cg, hg.astype(BF16)) * eacum_x[:, cols]
        zg = z_ref[0, :, cols]
        y = y * (zg * _sigmoid(zg))
        y = _rms(y, normg_ref[:, cols])
        y_ref[0, :, cols] = y.astype(y_ref.dtype)
        dec = _dot_exact_lhs(et_ref[cols, :], dec_b)
        h_ref[0, cols, :] = dec * hg + _dot_tn((xg * wdec_x[:, cols]).astype(BF16), bg)


def ssd_prompt(u, conv_w, conv_b, dt_bias, a_log, d_skip, ssm_norm, n_heads):
    b, l, _ = u.shape
    d_inner = n_heads * HEAD
    conv_dim = d_inner + 2 * SSM_GROUPS * SSM_STATE
    C = CHUNK
    assert l % C == 0 and conv_dim % d_inner == 0
    pad = lambda v: jnp.pad(v.reshape(1, -1), ((0, 0), (0, LANES - v.size)))
    e = (np.arange(d_inner)[None, :] // HEAD == np.arange(LANES)[:, None])
    e_bf = jnp.asarray(e, BF16)
    et_bf = jnp.asarray(e.T, BF16)
    full = lambda shape: pl.BlockSpec(shape, lambda i, j: (0,) * len(shape))
    return pl.pallas_call(
        functools.partial(_ssd_chunk_kernel, n_heads=n_heads),
        grid=(b, l // C),
        in_specs=[pl.BlockSpec((1, C, conv_dim), lambda i, j: (i, j, 0)),
                  pl.BlockSpec((1, C, d_inner), lambda i, j: (i, j, conv_dim // d_inner)),
                  pl.BlockSpec((1, C, LANES), lambda i, j: (i, j, (conv_dim + d_inner) // LANES)),
                  full((CONV_WIDTH, conv_dim)), full((1, conv_dim)), full((1, LANES)), full((1, LANES)),
                  full((1, d_inner)), full((1, d_inner)), full((LANES, d_inner)), full((d_inner, LANES))],
        out_specs=[pl.BlockSpec((1, C, d_inner), lambda i, j: (i, j, 0)),
                   pl.BlockSpec((1, d_inner, SSM_STATE), lambda i, j: (i, 0, 0))],
        out_shape=[jax.ShapeDtypeStruct((b, l, d_inner), BF16),
                   jax.ShapeDtypeStruct((b, d_inner, SSM_STATE), F32)],
        scratch_shapes=[pltpu.VMEM((C + 8, conv_dim), F32), pltpu.VMEM((C, conv_dim), F32)],
        compiler_params=_cparams(2),
        name="ssd_prompt",
    )(u, u, u, conv_w, conv_b.reshape(1, -1), pad(dt_bias), pad(a_log),
      jnp.repeat(d_skip, HEAD).reshape(1, -1), ssm_norm.reshape(1, -1), e_bf, et_bf)


def _head_masks(shape):
    lane = lax.broadcasted_iota(jnp.int32, shape, 1)
    return lane < HEAD, lane >= HEAD


def _rwkv_features(proj, prev, mu_ref, w0_ref, w2_ref, a0_ref, a2_ref, g2_ref, kk_ref, ka_ref, dim):
    xs = proj + (prev - proj) * mu_ref[...]
    k = xs[:, dim:2 * dim]
    t_wa = xs[:, 3 * dim:3 * dim + LANES]
    xg = xs[:, 3 * dim + LANES:3 * dim + 2 * LANES]
    w = -_softplus(-(w0_ref[...] + _dot(jnp.tanh(t_wa).astype(BF16), w2_ref[...]))) - 0.5
    a = _sigmoid(a0_ref[...] + _dot(t_wa.astype(BF16), a2_ref[...]))
    g = _dot(_sigmoid(xg).astype(BF16), g2_ref[...])
    return (xs[:, 0:dim], k * (1.0 + (a - 1.0) * ka_ref[...]), xs[:, 2 * dim:3 * dim], k * kk_ref[...], a,
            -jnp.exp(w), g)


def _rwkv_chunk_kernel(p_ref, mu_ref, w0_ref, w2_ref, a0_ref, a2_ref, g2_ref, kk_ref, ka_ref, rk_ref,
                       lng_ref, lnb_ref, ones2_ref, o_ref, s_ref,
                       ext_ref, r_s, k_s, v_s, kkn_s, a_s, lw_s, g_s, x_s, n_s, q_s, yb_s, bkh_s, pc_s, *, dim):
    c = pl.program_id(1)
    C = p_ref.shape[1]
    n_pairs = dim // LANES

    @pl.when(c == 0)
    def _():
        ext_ref[0:8, :] = jnp.zeros((8, ext_ref.shape[1]), F32)
        s_ref[...] = jnp.zeros(s_ref.shape, F32)

    @pl.when(c > 0)
    def _():
        ext_ref[0:8, :] = ext_ref[C:C + 8, :]

    proj = p_ref[0]
    ext_ref[8:C + 8, :] = proj
    (r_s[...], k_s[...], v_s[...], kkn_s[...], a_s[...], lw_s[...], g_s[...]) = _rwkv_features(
        proj, ext_ref[7:C + 7, :], mu_ref, w0_ref, w2_ref, a0_ref, a2_ref, g2_ref, kk_ref, ka_ref, dim)

    tril = _tril(C)
    tril_strict = _tril(C, -1)
    tril_bf = tril.astype(BF16)
    ones2 = ones2_ref[...]
    m0, m1 = _head_masks((C, LANES))
    row = lax.broadcasted_iota(jnp.int32, (LANES, LANES), 0)
    col = lax.broadcasted_iota(jnp.int32, (LANES, LANES), 1)
    blockdiag = (row < HEAD) == (col < HEAD)
    n_levels = C.bit_length() - 1
    assert 1 << n_levels == C

    def by_head(x):
        return jnp.concatenate([jnp.where(m0, x, 0.0), jnp.where(m1, x, 0.0)], axis=0)

    def head_sum(x):
        s0 = jnp.sum(jnp.where(m0, x, 0.0), axis=1, keepdims=True)
        s1 = jnp.sum(jnp.where(m1, x, 0.0), axis=1, keepdims=True)
        return jnp.where(m0, s0, s1)

    for j in range(n_pairs):
        cols = slice(j * LANES, (j + 1) * LANES)
        lw = lw_s[:, cols]
        logp = _dot_exact_lhs(tril_bf, lw)
        logpc = logp[C - 1:C, :]
        ep, epinv = jnp.exp(logp), jnp.exp(-logp)
        epc = jnp.exp(logpc - logp)
        kkf = kkn_s[:, cols]
        kk = kkf * lax.rsqrt(head_sum(kkf * kkf) + 1e-12)
        aj, kj, vj, rj = a_s[:, cols], k_s[:, cols], v_s[:, cols], r_s[:, cols]
        bvec = kk * aj
        at = -kk * jnp.exp(logp - lw)
        rt = rj * ep
        bk_bf = jnp.concatenate([bvec * epinv, kj * epinv], axis=0).astype(BF16)
        ar = jnp.concatenate([at, rt], axis=0)
        as_ = _dot_nt(ar.astype(BF16), s_ref[0, j].astype(BF16))
        mab, mak, qab, qak = [], [], [], []
        for hm in (m0, m1):
            g = _dot_nt(jnp.where(jnp.concatenate([hm, hm], axis=0), ar, 0.0).astype(BF16), bk_bf)
            mab.append(jnp.where(tril_strict, g[0:C, 0:C], 0.0))
            mak.append(jnp.where(tril_strict, g[0:C, C:2 * C], 0.0))
            qab.append(jnp.where(tril, g[C:2 * C, 0:C], 0.0))
            qak.append(jnp.where(tril, g[C:2 * C, C:2 * C], 0.0))
        x_s[j] = as_[0:C] + _dot(jnp.concatenate(mak, axis=1).astype(BF16), by_head(vj).astype(BF16))
        n_s[j] = jnp.concatenate(mab, axis=1).astype(BF16)
        q_s[j] = jnp.concatenate(qab + qak, axis=1).astype(BF16)
        yb_s[j] = as_[C:2 * C]
        bkh_s[j] = jnp.concatenate([bvec * epc, kj * epc], axis=0).astype(BF16)
        pc_s[j] = jnp.broadcast_to(jnp.exp(logpc), (8, LANES))

    zero = jnp.zeros((C, C), BF16)
    for lvl in range(n_levels):
        for j in range(n_pairs):
            n = n_s[j]
            x_s[j] += _dot(n, by_head(x_s[j]).astype(BF16))
            if lvl + 1 < n_levels:
                nd = jnp.concatenate([jnp.concatenate([n[:, 0:C], zero], axis=1),
                                      jnp.concatenate([zero, n[:, C:2 * C]], axis=1)], axis=0)
                n_s[j] = _dot(n, nd).astype(BF16)

    for j in range(n_pairs):
        cols = slice(j * LANES, (j + 1) * LANES)
        kj, vj, rj = k_s[:, cols], v_s[:, cols], r_s[:, cols]
        x = x_s[j]
        y = yb_s[j] + _dot(q_s[j], jnp.concatenate([by_head(x), by_head(vj)], axis=0).astype(BF16))
        upd = _dot_tn(jnp.concatenate([x, vj], axis=0).astype(BF16), bkh_s[j])
        s_ref[0, j] = pc_s[j, 0:1, :] * s_ref[0, j] + jnp.where(blockdiag, upd, 0.0)

        mean = head_sum(y) * (1.0 / HEAD)
        d = y - mean
        var = head_sum(d * d) * (1.0 / HEAD)
        yn = d * lax.rsqrt(var + LN_X_EPS) * lng_ref[:, cols] + lnb_ref[:, cols]
        bonus = head_sum(rj * kj * rk_ref[:, cols]) * vj
        o_ref[0, :, cols] = ((yn + bonus) * g_s[:, cols]).astype(o_ref.dtype)


def rwkv_prompt(proj, shift_mu, decay_w0, decay_w2, aaa_a0, aaa_a2, gate_g2, k_k, k_a, r_k, lnx_g, lnx_b):
    b, l, sd = proj.shape
    dim = decay_w0.size
    n_heads = dim // HEAD
    dl, al, gl = decay_w2.shape[0], aaa_a2.shape[0], gate_g2.shape[0]
    assert dl + al == LANES and gl == LANES and sd == 3 * dim + 2 * LANES and l % CHUNK == 0
    C = CHUNK
    n_pairs = dim // LANES
    w2 = jnp.concatenate([decay_w2, jnp.zeros((al, dim), F32)], axis=0).astype(BF16)
    a2 = jnp.concatenate([jnp.zeros((dl, dim), F32), aaa_a2], axis=0).astype(BF16)
    ones2 = jnp.asarray(np.arange(LANES)[:, None] // HEAD == np.arange(LANES)[None, :] // HEAD, BF16)
    row = lambda v: v.reshape(1, -1)
    full = lambda shape: pl.BlockSpec(shape, lambda i, j: (0,) * len(shape))
    vec = full((1, dim))
    y, s = pl.pallas_call(
        functools.partial(_rwkv_chunk_kernel, dim=dim),
        grid=(b, l // C),
        in_specs=[pl.BlockSpec((1, C, sd), lambda i, j: (i, j, 0)), full((1, sd)), vec, full((LANES, dim)), vec,
                  full((LANES, dim)), full((LANES, dim)), vec, vec, vec, vec, vec, full((LANES, LANES))],
        out_specs=[pl.BlockSpec((1, C, dim), lambda i, j: (i, j, 0)),
                   pl.BlockSpec((1, dim // LANES, LANES, LANES), lambda i, j: (i, 0, 0, 0))],
        out_shape=[jax.ShapeDtypeStruct((b, l, dim), BF16),
                   jax.ShapeDtypeStruct((b, dim // LANES, LANES, LANES), F32)],
        scratch_shapes=[pltpu.VMEM((C + 8, sd), F32)] + [pltpu.VMEM((C, dim), F32)] * 7
                       + [pltpu.VMEM((n_pairs, C, LANES), F32), pltpu.VMEM((n_pairs, C, 2 * C), BF16),
                          pltpu.VMEM((n_pairs, C, 4 * C), BF16), pltpu.VMEM((n_pairs, C, LANES), F32),
                          pltpu.VMEM((n_pairs, 2 * C, LANES), BF16), pltpu.VMEM((n_pairs, 8, LANES), F32)],
        compiler_params=_cparams(2),
        name="rwkv_prompt",
    )(proj, row(shift_mu), row(decay_w0), w2, row(aaa_a0), a2, gate_g2.astype(BF16), row(k_k), row(k_a),
      row(r_k), row(lnx_g), row(lnx_b), ones2)
    s = s.reshape(b, dim // LANES, 2, HEAD, 2, HEAD)
    s = jnp.stack([s[:, :, 0, :, 0, :], s[:, :, 1, :, 1, :]], axis=2).reshape(b, n_heads, HEAD, HEAD)
    return y, s


def _mix_kernel(x_ref, ym_ref, yr_ref, gate_ref, wm_ref, wr_ref, wo_ref, nf_ref, wq_ref, x2_ref, q_ref):
    d = x_ref.shape[1]
    y_m = _dot(ym_ref[...].astype(BF16), wm_ref[...])
    y_r = _dot(yr_ref[...].astype(BF16), wr_ref[...])
    mix = _sigmoid(gate_ref[:, 0:d]) * y_m + _sigmoid(gate_ref[:, d:2 * d]) * y_r
    x2 = x_ref[...] + _dot(mix.astype(BF16), wo_ref[...])
    x2_ref[...] = x2
    q_ref[...] = _dot(_rms(x2, nf_ref[...]).astype(BF16), wq_ref[...]).astype(q_ref.dtype)


def mix_and_query(x, ym, yr, gates, w_out_ssm, w_out_rwkv, w_out, norm_ffn, peer_wq, tm):
    m, d = x.shape
    tm = min(tm, m)
    assert m % tm == 0
    rows = lambda n: pl.BlockSpec((tm, n), lambda i: (i, 0))
    full = lambda a: pl.BlockSpec(a.shape, lambda i: (0, 0))
    nq = peer_wq.shape[1]
    args = (x, ym, yr, gates, w_out_ssm, w_out_rwkv, w_out, norm_ffn.reshape(1, -1), peer_wq)
    return pl.pallas_call(
        _mix_kernel,
        grid=(m // tm,),
        in_specs=[rows(d), rows(ym.shape[1]), rows(yr.shape[1]), rows(2 * d)] + [full(a) for a in args[4:]],
        out_specs=[rows(d), rows(nq)],
        out_shape=[jax.ShapeDtypeStruct((m, d), F32), jax.ShapeDtypeStruct((m, nq), BF16)],
        compiler_params=_cparams(1),
        name="mix_and_query",
    )(*args)


PEER_TOPK = 16
N_KEYS = 128


def _erf_gelu(x):
    return 0.5 * x * (1.0 + lax.erf(x * np.float32(1.0 / np.sqrt(2.0))))


NOT_TOP = 64.0


def _kth_largest(s, k, want_rank=False):
    tops = []
    rank = jnp.full(s.shape, NOT_TOP, F32)
    for r in range(k):
        m = jnp.max(s, axis=0, keepdims=True)
        tops.append(m)
        hit = s == m
        if want_rank:
            rank = jnp.where(hit, float(r), rank)
        s = jnp.where(hit, -jnp.inf, s)
    return (tops, rank) if want_rank else tops


def _peer_kernel(x2_ref, q_ref, p_ref, k1_ref, k2_ref, u_ref, vt_ref, nf_ref, npl_ref, wg_ref, wp_ref, nfin_ref,
                 y_ref, hb_s, n1_s, c_s, rk2_s, d_s, act0_s, act1_s, w0_s, w1_s, acc_s, *, n_heads, n_e):
    j = pl.program_id(1)
    tt = x2_ref.shape[0]
    ec = u_ref.shape[0]
    qd = N_KEYS
    assert ec == 8 * N_KEYS

    @pl.when(j == 0)
    def _():
        hb_s[...] = _rms(x2_ref[...], nf_ref[...]).T.astype(BF16)
        acc_s[...] = jnp.zeros(acc_s.shape, F32)
        sub8 = lax.broadcasted_iota(jnp.int32, (8, LANES), 0)
        n_ts = tt // LANES

        def select_experts(unit, carry):
            h = unit // n_ts
            tok = pl.ds(pl.multiple_of((unit % n_ts) * LANES, LANES), LANES)
            q1 = q_ref[tok, pl.ds(pl.multiple_of(2 * h * qd, qd), qd)]
            q2 = q_ref[tok, pl.ds(pl.multiple_of((2 * h + 1) * qd, qd), qd)]
            s1 = _dot_nt(k1_ref[h], q1)
            s2 = _dot_nt(k2_ref[h], q2)
            v1 = _kth_largest(s1, PEER_TOPK)
            v2, rank2 = _kth_largest(s2, PEER_TOPK, want_rank=True)
            v2lo = jnp.concatenate(v2[0:8], axis=0)
            pieces = [v1[0] + v2lo, v1[0] + jnp.concatenate(v2[8:16], axis=0), v1[1] + v2lo]
            for k1 in range(2, 8):
                pieces.append(jnp.where(sub8 < PEER_TOPK // (k1 + 1), v1[k1] + v2lo, -jnp.inf))
            pieces.append(jnp.concatenate(v1[8:16], axis=0) + v2[0])
            cand = jnp.concatenate(pieces, axis=0)
            top = _kth_largest(cand, PEER_TOPK)
            z = sum(jnp.exp(t - top[0]) for t in top)
            th = top[PEER_TOPK - 1]
            n1_s[h, :, tok] = sum(jnp.where(s1 + v >= th, 1.0, 0.0) for v in v2)
            c_s[h, :, tok] = jnp.exp(s1 - v1[0]) / z
            rk2_s[h, :, tok] = rank2.astype(BF16)
            d_s[h, :, tok] = jnp.exp(s2 - v2[0]).astype(BF16)
            return carry

        lax.fori_loop(0, n_heads * n_ts, select_experts, 0, unroll=2)

    act = (act0_s, act1_s)
    wts = (w0_s, w1_s)

    def step(par, do_a, do_b, do_c):
        i_rows = pl.ds(pl.multiple_of(jnp.clip(j - 1, 0, n_e - 1) * 8, 8), 8)
        jh = N_KEYS // 2
        n_blocks = (tt // LANES) * (N_KEYS // jh)
        d = vt_ref.shape[1]
        for ts in range(tt // LANES):
            tok = slice(ts * LANES, (ts + 1) * LANES)
            for jb in range(N_KEYS // jh):
                blk = ts * (N_KEYS // jh) + jb
                ra = slice(blk * (ec // n_blocks), (blk + 1) * (ec // n_blocks))
                rc = slice(blk * (d // n_blocks), (blk + 1) * (d // n_blocks))
                if do_a:
                    act[par][ra, :] = _dot(u_ref[ra, :], hb_s[...])
                if do_c:
                    acc_s[rc, :] += _dot(vt_ref[0, rc, :], wts[par][...])
                if not do_b:
                    continue
                jrows = slice(jb * jh, (jb + 1) * jh)
                acc = [jnp.zeros((jh, LANES), BF16)] * 8
                for h in range(n_heads):
                    rk, dd = rk2_s[h, jrows, tok], d_s[h, jrows, tok]
                    n1b, cb = n1_s[h, i_rows, tok], c_s[h, i_rows, tok]
                    for il in range(8):
                        sel = rk < n1b[il:il + 1].astype(BF16)
                        acc[il] = acc[il] + jnp.where(sel, dd, 0.0) * cb[il:il + 1].astype(BF16)
                for il in range(8):
                    rows = slice(il * N_KEYS + jb * jh, il * N_KEYS + (jb + 1) * jh)
                    wts[1 - par][rows, tok] = acc[il] * _erf_gelu(act[1 - par][rows, tok]).astype(BF16)

    assert n_e % 2 == 0 and n_e >= 4
    pl.when(j == 0)(lambda: step(0, True, False, False))
    pl.when(j == 1)(lambda: step(1, True, True, False))
    pl.when((j >= 2) & (j < n_e) & (j % 2 == 0))(lambda: step(0, True, True, True))
    pl.when((j >= 2) & (j < n_e) & (j % 2 == 1))(lambda: step(1, True, True, True))
    pl.when(j == n_e)(lambda: step(0, False, True, True))
    pl.when(j == n_e + 1)(lambda: step(1, False, False, True))

    @pl.when(j == n_e + 1)
    def _():
        x3 = x2_ref[...] + acc_s[...].T
        gate = _sigmoid(_dot(_rms(x3, npl_ref[...]).astype(BF16), wg_ref[...]))
        x4 = x3 + gate * _dot(p_ref[...].astype(BF16), wp_ref[...])
        y_ref[...] = _rms(x4, nfin_ref[...])


def peer_ple_final(x2, q, p, peer_k1, peer_k2, peer_u, peer_vt, norm_ffn, norm_ple, w_ple_gate, w_ple_proj,
                   norm_final, tt, ec):
    t, d = x2.shape
    n_heads = peer_k1.shape[0]
    n_exp = peer_u.shape[0]
    tt = min(tt, t)
    assert t % tt == 0 and n_exp % ec == 0 and tt % LANES == 0 and ec % N_KEYS == 0
    assert peer_k1.shape[1:] == (N_KEYS, N_KEYS) and n_exp == N_KEYS * N_KEYS
    rows = lambda n: pl.BlockSpec((tt, n), lambda i, j: (i, 0))
    full = lambda a: pl.BlockSpec(a.shape, lambda i, j: (0,) * a.ndim)
    row = lambda v: v.reshape(1, -1)
    consts = (row(norm_ffn), row(norm_ple), w_ple_gate, w_ple_proj, row(norm_final))
    n_e = n_exp // ec
    assert peer_vt.shape == (n_e, d, ec)
    hs = lambda dt: pltpu.VMEM((n_heads, N_KEYS, tt), dt)
    return pl.pallas_call(
        functools.partial(_peer_kernel, n_heads=n_heads, n_e=n_e),
        grid=(t // tt, n_e + 2),
        in_specs=[rows(d), rows(q.shape[1]), rows(p.shape[1]), full(peer_k1), full(peer_k2),
                  pl.BlockSpec((ec, d), lambda i, j: (jnp.minimum(j, n_e - 1), 0)),
                  pl.BlockSpec((1, d, ec), lambda i, j: (jnp.clip(j - 2, 0, n_e - 1), 0, 0))]
                 + [full(a) for a in consts],
        out_specs=rows(d),
        out_shape=jax.ShapeDtypeStruct((t, d), F32),
        scratch_shapes=[pltpu.VMEM((d, tt), BF16), hs(F32), hs(F32), hs(BF16), hs(BF16),
                        pltpu.VMEM((ec, tt), F32), pltpu.VMEM((ec, tt), F32),
                        pltpu.VMEM((ec, tt), BF16), pltpu.VMEM((ec, tt), BF16), pltpu.VMEM((d, tt), F32)],
        compiler_params=_cparams(2),
        name="peer_ple_final",
    )(x2, q, p, peer_k1, peer_k2, peer_u, peer_vt, *consts)


def _as_column(x_row):
    n = x_row.shape[1]
    eye = lax.broadcasted_iota(jnp.int32, (n, n), 0) == lax.broadcasted_iota(jnp.int32, (n, n), 1)
    return jnp.sum(jnp.where(eye, jnp.broadcast_to(x_row, (n, n)), 0.0), axis=1, keepdims=True)


def _rows8(x_row):
    return jnp.broadcast_to(x_row, (8, x_row.shape[1]))


def _ssd_step_kernel(xbc_ref, z_ref, dt_ref, conv_ref, h_ref, convw_ref, convb_ref, dtb_ref, alog_ref, dskip_ref,
                     normg_ref, e_ref, y_ref, ho_ref, *, n_heads):
    d_inner = n_heads * HEAD
    gw = d_inner // SSM_GROUPS
    cs = conv_ref[0]
    conv = convb_ref[...] + xbc_ref[0] * convw_ref[CONV_WIDTH - 1:CONV_WIDTH, :]
    for k in range(CONV_WIDTH - 1):
        conv = conv + cs[k:k + 1, :] * convw_ref[k:k + 1, :]
    act = conv * _sigmoid(conv)
    dt = _softplus(dt_ref[0] + dtb_ref[...])
    da = jnp.exp(dt * -jnp.exp(alog_ref[...]))
    e = e_ref[...]
    dt_x = _dot_exact_rhs(_rows8(dt), e)[0:1]
    da_x = _dot_exact_rhs(_rows8(da), e)[0:1]
    for g in range(SSM_GROUPS):
        cols = slice(g * gw, (g + 1) * gw)
        xg = act[:, cols]
        bg = act[:, d_inner + g * SSM_STATE:d_inner + (g + 1) * SSM_STATE]
        cg = act[:, d_inner + (SSM_GROUPS + g) * SSM_STATE:d_inner + (SSM_GROUPS + g + 1) * SSM_STATE]
        hn = _as_column(da_x[:, cols]) * h_ref[0, cols, :] + _as_column(xg * dt_x[:, cols]) * bg
        ho_ref[0, cols, :] = hn
        y = _dot_nt(_rows8(cg).astype(BF16), hn.astype(BF16))[0:1] + xg * dskip_ref[:, cols]
        zg = z_ref[0, :, cols]
        y = y * (zg * _sigmoid(zg))
        y_ref[0, :, cols] = _rms(y, normg_ref[:, cols])


def ssd_step(u, state_conv, state_ssm, conv_w, conv_b, dt_bias, a_log, d_skip, ssm_norm, n_heads):
    b = u.shape[0]
    d_inner = n_heads * HEAD
    conv_dim = d_inner + 2 * SSM_GROUPS * SSM_STATE
    pad = lambda v: jnp.pad(v.reshape(1, -1), ((0, 0), (0, LANES - v.size)))
    e_bf = jnp.asarray(np.arange(d_inner)[None, :] // HEAD == np.arange(LANES)[:, None], BF16)
    full = lambda shape: pl.BlockSpec(shape, lambda i: (0,) * len(shape))
    u3 = u.reshape(b, 1, -1)
    y, h = pl.pallas_call(
        functools.partial(_ssd_step_kernel, n_heads=n_heads),
        grid=(b,),
        in_specs=[pl.BlockSpec((1, 1, conv_dim), lambda i: (i, 0, 0)),
                  pl.BlockSpec((1, 1, d_inner), lambda i: (i, 0, conv_dim // d_inner)),
                  pl.BlockSpec((1, 1, LANES), lambda i: (i, 0, (conv_dim + d_inner) // LANES)),
                  pl.BlockSpec((1, CONV_WIDTH - 1, conv_dim), lambda i: (i, 0, 0)),
                  pl.BlockSpec((1, d_inner, SSM_STATE), lambda i: (i, 0, 0)),
                  full((CONV_WIDTH, conv_dim)), full((1, conv_dim)), full((1, LANES)), full((1, LANES)),
                  full((1, d_inner)), full((1, d_inner)), full((LANES, d_inner))],
        out_specs=[pl.BlockSpec((1, 1, d_inner), lambda i: (i, 0, 0)),
                   pl.BlockSpec((1, d_inner, SSM_STATE), lambda i: (i, 0, 0))],
        out_shape=[jax.ShapeDtypeStruct((b, 1, d_inner), F32),
                   jax.ShapeDtypeStruct((b, d_inner, SSM_STATE), F32)],
        compiler_params=_cparams(1),
        name="ssd_step",
    )(u3, u3, u3, state_conv, state_ssm.reshape(b, d_inner, SSM_STATE), conv_w, conv_b.reshape(1, -1),
      pad(dt_bias), pad(a_log), jnp.repeat(d_skip, HEAD).reshape(1, -1), ssm_norm.reshape(1, -1), e_bf)
    return y.reshape(b, d_inner), h


def _rwkv_step_features_kernel(p_ref, prev_ref, mu_ref, w0_ref, w2_ref, a0_ref, a2_ref, g2_ref, kk_ref, ka_ref,
                               ones2_ref, o_ref, *, dim):
    r, k, v, kkf, a, lw, g = _rwkv_features(p_ref[...], prev_ref[...], mu_ref, w0_ref, w2_ref, a0_ref, a2_ref,
                                            g2_ref, kk_ref, ka_ref, dim)
    for j in range(dim // LANES):
        cols = slice(j * LANES, (j + 1) * LANES)
        kj = kkf[:, cols]
        o_ref[3, :, cols] = kj * lax.rsqrt(_dot_exact_rhs(kj * kj, ones2_ref[...]) + 1e-12)
    o_ref[0], o_ref[1], o_ref[2], o_ref[4], o_ref[5], o_ref[6] = r, k, v, a, jnp.exp(lw), g


def _rwkv_step_kernel(f_ref, s_ref, rk_ref, lng_ref, lnb_ref, y_ref, so_ref, y_s):
    n_heads = s_ref.shape[1]
    eye = lax.broadcasted_iota(jnp.int32, (HEAD, HEAD), 0) == lax.broadcasted_iota(jnp.int32, (HEAD, HEAD), 1)
    for h in range(n_heads):
        row = lambda i: f_ref[i, 0, h:h + 1, :]
        r, k, v, kk, a, w = (row(i) for i in range(6))
        s = s_ref[0, h]
        sa = jnp.sum(s * -kk, axis=1, keepdims=True)
        v_col = jnp.sum(jnp.where(eye, jnp.broadcast_to(v, (HEAD, HEAD)), 0.0), axis=1, keepdims=True)
        sn = s * w + sa * (kk * a) + v_col * k
        so_ref[0, h] = sn
        y_s[h:h + 1, :] = _dot_nt(_rows8(r).astype(BF16), sn.astype(BF16))[0:1]
    y = y_s[...]
    r, k, v, g = f_ref[0, 0], f_ref[1, 0], f_ref[2, 0], f_ref[6, 0]
    d = y - jnp.mean(y, axis=-1, keepdims=True)
    var = jnp.mean(d * d, axis=-1, keepdims=True)
    yn = d * lax.rsqrt(var + LN_X_EPS) * lng_ref[...] + lnb_ref[...]
    bonus = jnp.sum(r * k * rk_ref[...], axis=-1, keepdims=True) * v
    y_ref[0] = (yn + bonus) * g


def rwkv_step(proj, shift_prev, state_wkv, shift_mu, decay_w0, decay_w2, aaa_a0, aaa_a2, gate_g2, k_k, k_a, r_k,
              lnx_g, lnx_b):
    b, sd = proj.shape
    dim = decay_w0.size
    n_heads = dim // HEAD
    dl, al, gl = decay_w2.shape[0], aaa_a2.shape[0], gate_g2.shape[0]
    assert dl + al == LANES and gl == LANES and sd == 3 * dim + 2 * LANES
    w2 = jnp.concatenate([decay_w2, jnp.zeros((al, dim), F32)], axis=0).astype(BF16)
    a2 = jnp.concatenate([jnp.zeros((dl, dim), F32), aaa_a2], axis=0).astype(BF16)
    ones2 = jnp.asarray(np.arange(LANES)[:, None] // HEAD == np.arange(LANES)[None, :] // HEAD, BF16)
    row = lambda v: v.reshape(1, -1)
    args = (proj, shift_prev, row(shift_mu), row(decay_w0), w2, row(aaa_a0), a2, gate_g2.astype(BF16), row(k_k),
            row(k_a), ones2)
    feats = pl.pallas_call(
        functools.partial(_rwkv_step_features_kernel, dim=dim),
        grid=(1,),
        in_specs=[pl.BlockSpec(a.shape, lambda i: (0, 0)) for a in args],
        out_specs=pl.BlockSpec((7, b, dim), lambda i: (0, 0, 0)),
        out_shape=jax.ShapeDtypeStruct((7, b, dim), F32),
        compiler_params=_cparams(1),
        name="rwkv_step_features",
    )(*args)
    hv = lambda v: v.reshape(n_heads, HEAD)
    full = pl.BlockSpec((n_heads, HEAD), lambda i: (0, 0))
    y, s = pl.pallas_call(
        _rwkv_step_kernel,
        grid=(b,),
        in_specs=[pl.BlockSpec((7, 1, n_heads, HEAD), lambda i: (0, i, 0, 0)),
                  pl.BlockSpec((1, n_heads, HEAD, HEAD), lambda i: (i, 0, 0, 0)), full, full, full],
        out_specs=[pl.BlockSpec((1, n_heads, HEAD), lambda i: (i, 0, 0)),
                   pl.BlockSpec((1, n_heads, HEAD, HEAD), lambda i: (i, 0, 0, 0))],
        out_shape=[jax.ShapeDtypeStruct((b, n_heads, HEAD), F32),
                   jax.ShapeDtypeStruct((b, n_heads, HEAD, HEAD), F32)],
        scratch_shapes=[pltpu.VMEM((n_heads, HEAD), F32)],
        compiler_params=_cparams(1),
        name="rwkv_step",
    )(feats.reshape(7, b, n_heads, HEAD), state_wkv, hv(r_k), hv(lnx_g), hv(lnx_b))
    return y.reshape(b, dim), s


def _layer(x, p, states, wts, n_ssm_heads):
    b, l, d = x.shape
    xt = x.reshape(b * l, d)
    g_mix = wts['norm_mix'].reshape(1, -1)
    tm = 1024
    u_ssm = norm_matmul(xt, g_mix, wts['w_ssm'], tm, wts['w_ssm'].shape[1] // 7)
    u_rwkv = norm_matmul(xt, g_mix, wts['w_shift'], tm, wts['w_shift'].shape[1] // 2)
    u_gate = norm_matmul(xt, g_mix, wts['w_gates'], tm, wts['w_gates'].shape[1] // 2)
    d_inner = n_ssm_heads * HEAD
    conv_dim = d_inner + 2 * SSM_GROUPS * SSM_STATE
    ssd_w = (wts['conv_w'], wts['conv_b'], wts['dt_bias'], wts['a_log'], wts['d_skip'], wts['ssm_norm'])
    rwkv_w = tuple(wts[k] for k in ('shift_mu', 'decay_w0', 'decay_w2', 'aaa_a0', 'aaa_a2', 'gate_g2', 'k_k', 'k_a',
                                    'r_k', 'lnx_g', 'lnx_b'))
    if states is None:
        ym, ssm_new = ssd_prompt(u_ssm.reshape(b, l, -1), *ssd_w, n_ssm_heads)
        ym = ym.reshape(b * l, d_inner)
        conv_new = u_ssm.reshape(b, l, -1)[:, l - (CONV_WIDTH - 1):, :conv_dim]
        yr, wkv_new = rwkv_prompt(u_rwkv.reshape(b, l, -1), *rwkv_w)
        yr = yr.reshape(b * l, -1)
        shift_new = u_rwkv.reshape(b, l, -1)[:, l - 1]
    else:
        conv_prev, ssm_prev, wkv_prev, shift_prev = states
        ym, ssm_new = ssd_step(u_ssm, conv_prev, ssm_prev, *ssd_w, n_ssm_heads)
        conv_new = jnp.concatenate([conv_prev[:, 1:], u_ssm[:, None, :conv_dim]], axis=1)
        yr, wkv_new = rwkv_step(u_rwkv, shift_prev, wkv_prev, *rwkv_w)
        shift_new = u_rwkv
    x2, q = mix_and_query(xt, ym, yr, u_gate, wts['w_out_ssm'], wts['w_out_rwkv'], wts['w_out'], wts['norm_ffn'],
                          wts['peer_wq'], 512)
    y = peer_ple_final(x2, q, p.reshape(b * l, -1), wts['peer_k1'], wts['peer_k2'], wts['peer_u'], wts['peer_vt'],
                       wts['norm_ffn'], wts['norm_ple'], wts['w_ple_gate'], wts['w_ple_proj'], wts['norm_final'],
                       512, 8 * N_KEYS)
    return (y.reshape(b, l, d), ssm_new.reshape(b, n_ssm_heads, HEAD, SSM_STATE), conv_new, wkv_new, shift_new)


def kernel(x_prompt, x_sample, p_prompt, p_sample, state_ssm, state_conv, state_wkv, state_shift, norm_mix, w_in,
           conv_w, conv_b, dt_bias, a_log, d_skip, ssm_norm, w_out_ssm, shift_mu, decay_w0, decay_w2, aaa_a0, aaa_a2,
           gate_g2, k_k, k_a, r_k, lnx_g, lnx_b, w_out_rwkv, w_out, norm_ffn, peer_wq, peer_k1, peer_k2, peer_u,
           peer_v, norm_ple, w_ple_gate, w_ple_proj, norm_final):
    depth = w_in.shape[0]
    assert depth == 1, "single-layer trunk"
    d_model = x_prompt.shape[-1]
    n_ssm_heads = dt_bias.shape[1]
    d_inner = n_ssm_heads * HEAD
    conv_dim = conv_w.shape[2]
    shift_dim = shift_mu.shape[1]
    bf = lambda a: a.astype(BF16)
    o = np.cumsum([0, d_inner, conv_dim, n_ssm_heads, shift_dim, d_model, d_model])
    wi = w_in[0]
    wts = {
        'w_ssm': bf(jnp.concatenate([wi[:, o[1]:o[2]], wi[:, o[0]:o[1]], wi[:, o[2]:o[3]],
                                     jnp.zeros((d_model, LANES - n_ssm_heads), F32)], axis=1)),
        'w_shift': bf(wi[:, o[3]:o[4]]),
        'w_gates': bf(wi[:, o[4]:o[6]]),
        'w_out_ssm': bf(w_out_ssm[0]), 'w_out_rwkv': bf(w_out_rwkv[0]), 'w_out': bf(w_out[0]),
        'peer_wq': bf(peer_wq[0]), 'peer_k1': bf(peer_k1[0]), 'peer_k2': bf(peer_k2[0]),
        'peer_u': bf(peer_u[0]),
        'peer_vt': bf(peer_v[0]).reshape(-1, 8 * N_KEYS, d_model).transpose(0, 2, 1),
        'w_ple_gate': bf(w_ple_gate[0]), 'w_ple_proj': bf(w_ple_proj[0]), 'norm_final': norm_final,
    }
    for name, val in (('norm_mix', norm_mix), ('conv_w', conv_w), ('conv_b', conv_b), ('dt_bias', dt_bias),
                      ('a_log', a_log), ('d_skip', d_skip), ('ssm_norm', ssm_norm), ('shift_mu', shift_mu),
                      ('decay_w0', decay_w0), ('decay_w2', decay_w2), ('aaa_a0', aaa_a0), ('aaa_a2', aaa_a2),
                      ('gate_g2', gate_g2), ('k_k', k_k), ('k_a', k_a), ('r_k', r_k), ('lnx_g', lnx_g),
                      ('lnx_b', lnx_b), ('norm_ffn', norm_ffn), ('norm_ple', norm_ple)):
        wts[name] = val[0]
    yp, ssm_p, conv_p, wkv_p, shift_p = _layer(x_prompt, p_prompt[0], None, wts, n_ssm_heads)
    ys, ssm_s, conv_s, wkv_s, shift_s = _layer(
        x_sample, p_sample[0], (state_conv[0], state_ssm[0], state_wkv[0], state_shift[0]), wts, n_ssm_heads)
    return (yp, ys, ssm_p[None], conv_p[None], wkv_p[None], shift_p[None],
            ssm_s[None], conv_s[None], wkv_s[None], shift_s[None])
```

```python
import functools

import numpy as np
import jax
import jax.numpy as jnp
from jax import lax
from jax.experimental import pallas as pl
from jax.experimental.pallas import tpu as pltpu

F32 = jnp.float32
BF16 = jnp.bfloat16

EPS = 1e-6
LN_X_EPS = 64e-5
HEAD = 64
SSM_STATE = 128
SSM_GROUPS = 8
CONV_WIDTH = 4
CHUNK = 128
LANES = 128
VMEM_LIMIT = 56 * 1024 * 1024


def _cparams(n_axes):
    return pltpu.CompilerParams(dimension_semantics=("arbitrary",) * n_axes,
                                vmem_limit_bytes=VMEM_LIMIT)


def _dot(a, b):
    return jnp.dot(a, b, preferred_element_type=F32)


def _dot_nt(a, b):
    return lax.dot_general(a, b, (((1,), (1,)), ((), ())), preferred_element_type=F32)


def _dot_tn(a, b):
    return lax.dot_general(a, b, (((0,), (0,)), ((), ())), preferred_element_type=F32)


def _split2(x):
    hi = x.astype(BF16)
    lo = (x - hi.astype(F32)).astype(BF16)
    return hi, lo


def _split3(x):
    x1 = x.astype(BF16)
    r = x - x1.astype(F32)
    x2 = r.astype(BF16)
    x3 = (r - x2.astype(F32)).astype(BF16)
    return x1, x2, x3


def _dot_exact_rhs(a, e, passes=3):
    if passes == 2:
        a1, a2 = _split2(a)
        return _dot(a1, e) + _dot(a2, e)
    a1, a2, a3 = _split3(a)
    return _dot(a1, e) + (_dot(a2, e) + _dot(a3, e))


def _dot_exact_lhs(e, a):
    a1, a2, a3 = _split3(a)
    return _dot(e, a1) + (_dot(e, a2) + _dot(e, a3))


def _rms(x, g):
    return x * lax.rsqrt(jnp.mean(x * x, axis=-1, keepdims=True) + EPS) * g


def _sigmoid(x):
    return 1.0 / (1.0 + jnp.exp(-x))


def _softplus(x):
    return jnp.maximum(x, 0.0) + jnp.log1p(jnp.exp(-jnp.abs(x)))


def _tril(n, k=0, dtype=F32):
    r = lax.broadcasted_iota(jnp.int32, (n, n), 0)
    c = lax.broadcasted_iota(jnp.int32, (n, n), 1)
    return (c <= r + k)


def _norm_matmul_kernel(x_ref, g_ref, w_ref, o_ref, h_ref):
    @pl.when(pl.program_id(1) == 0)
    def _():
        h_ref[...] = _rms(x_ref[...], g_ref[...]).astype(BF16)

    o_ref[...] = _dot(h_ref[...], w_ref[...]).astype(o_ref.dtype)


def norm_matmul(x, g, w, tm, tn, out_dtype=F32):
    m, k = x.shape
    n = w.shape[1]
    tm = min(tm, m)
    assert m % tm == 0 and n % tn == 0, (m, tm, n, tn)
    return pl.pallas_call(
        _norm_matmul_kernel,
        grid=(m // tm, n // tn),
        in_specs=[pl.BlockSpec((tm, k), lambda i, j: (i, 0)),
                  pl.BlockSpec((1, k), lambda i, j: (0, 0)),
                  pl.BlockSpec((k, tn), lambda i, j: (0, j))],
        out_specs=pl.BlockSpec((tm, tn), lambda i, j: (i, j)),
        out_shape=jax.ShapeDtypeStruct((m, n), out_dtype),
        scratch_shapes=[pltpu.VMEM((tm, k), BF16)],
        compiler_params=_cparams(2),
        name="norm_matmul",
    )(x, g, w)


def _ssd_chunk_kernel(xbc_ref, z_ref, dt_ref, convw_ref, convb_ref, dtb_ref, alog_ref, dskip_ref,
                      normg_ref, e_ref, et_ref, y_ref, h_ref, ext_ref, act_ref, *, n_heads):
    c = pl.program_id(1)
    C = xbc_ref.shape[1]
    d_inner = n_heads * HEAD
    gw = d_inner // SSM_GROUPS
    hpg = n_heads // SSM_GROUPS

    @pl.when(c == 0)
    def _():
        ext_ref[0:8, :] = jnp.zeros((8, ext_ref.shape[1]), F32)
        h_ref[...] = jnp.zeros(h_ref.shape, F32)

    @pl.when(c > 0)
    def _():
        ext_ref[0:8, :] = ext_ref[C:C + 8, :]

    ext_ref[8:C + 8, :] = xbc_ref[0].astype(F32)
    conv = convb_ref[...]
    for k in range(CONV_WIDTH):
        off = 8 - (CONV_WIDTH - 1) + k
        conv = conv + ext_ref[off:off + C, :] * convw_ref[k:k + 1, :]
    act_ref[...] = conv * _sigmoid(conv)

    dt = _softplus(dt_ref[0].astype(F32) + dtb_ref[...])
    a = -jnp.exp(alog_ref[...])
    tril = _tril(C)
    acum = _dot_exact_lhs(tril.astype(BF16), dt * a)
    acum_t = acum.T
    dt_t = dt.T
    e = e_ref[...]
    eacum_x = _dot_exact_rhs(jnp.exp(acum), e)
    wdec_x = _dot_exact_rhs(jnp.exp(acum[C - 1:C, :] - acum) * dt, e)
    dec_b = jnp.broadcast_to(jnp.exp(acum_t[:, C - 1:C]), (LANES, SSM_STATE))
    lane = lax.broadcasted_iota(jnp.int32, (C, gw), 1)

    for g in range(SSM_GROUPS):
        cols = slice(g * gw, (g + 1) * gw)
        xg = act_ref[:, cols]
        bg = act_ref[:, d_inner + g * SSM_STATE:d_inner + (g + 1) * SSM_STATE].astype(BF16)
        cg = act_ref[:, d_inner + (SSM_GROUPS + g) * SSM_STATE:
                     d_inner + (SSM_GROUPS + g + 1) * SSM_STATE].astype(BF16)
        cb = _dot_nt(cg, bg)
        y = xg * dskip_ref[:, cols]
        for r in range(hpg):
            h = g * hpg + r
            seg = acum[:, h:h + 1] - acum_t[h:h + 1, :]
            m = cb * jnp.exp(jnp.where(tril, seg, -jnp.inf)) * dt_t[h:h + 1, :]
            xm = jnp.where((lane >= r * HEAD) & (lane < (r + 1) * HEAD), xg, 0.0)
            y = y + _dot(m.astype(BF16), xm.astype(BF16))
        hg = h_ref[0, cols, :]
        y = y + _dot_nt(cg, hg.astype(BF16)) * eacum_x[:, cols]
        zg = z_ref[0, :, cols].astype(F32)
        y = y * (zg * _sigmoid(zg))
        y = _rms(y, normg_ref[:, cols])
        y_ref[0, :, cols] = y.astype(y_ref.dtype)
        dec = _dot_exact_lhs(et_ref[cols, :], dec_b)
        h_ref[0, cols, :] = dec * hg + _dot_tn((xg * wdec_x[:, cols]).astype(BF16), bg)


def ssd_prompt(u, conv_w, conv_b, dt_bias, a_log, d_skip, ssm_norm, n_heads):
    b, l, _ = u.shape
    d_inner = n_heads * HEAD
    conv_dim = d_inner + 2 * SSM_GROUPS * SSM_STATE
    C = CHUNK
    assert l % C == 0 and conv_dim % d_inner == 0
    pad = lambda v: jnp.pad(v.reshape(1, -1), ((0, 0), (0, LANES - v.size)))
    e = (np.arange(d_inner)[None, :] // HEAD == np.arange(LANES)[:, None])
    e_bf = jnp.asarray(e, BF16)
    et_bf = jnp.asarray(e.T, BF16)
    full = lambda shape: pl.BlockSpec(shape, lambda i, j: (0,) * len(shape))
    return pl.pallas_call(
        functools.partial(_ssd_chunk_kernel, n_heads=n_heads),
        grid=(b, l // C),
        in_specs=[pl.BlockSpec((1, C, conv_dim), lambda i, j: (i, j, 0)),
                  pl.BlockSpec((1, C, d_inner), lambda i, j: (i, j, conv_dim // d_inner)),
                  pl.BlockSpec((1, C, LANES), lambda i, j: (i, j, (conv_dim + d_inner) // LANES)),
                  full((CONV_WIDTH, conv_dim)), full((1, conv_dim)), full((1, LANES)), full((1, LANES)),
                  full((1, d_inner)), full((1, d_inner)), full((LANES, d_inner)), full((d_inner, LANES))],
        out_specs=[pl.BlockSpec((1, C, d_inner), lambda i, j: (i, j, 0)),
                   pl.BlockSpec((1, d_inner, SSM_STATE), lambda i, j: (i, 0, 0))],
        out_shape=[jax.ShapeDtypeStruct((b, l, d_inner), BF16),
                   jax.ShapeDtypeStruct((b, d_inner, SSM_STATE), F32)],
        scratch_shapes=[pltpu.VMEM((C + 8, conv_dim), F32), pltpu.VMEM((C, conv_dim), F32)],
        compiler_params=_cparams(2),
        name="ssd_prompt",
    )(u, u, u, conv_w, conv_b.reshape(1, -1), pad(dt_bias), pad(a_log),
      jnp.repeat(d_skip, HEAD).reshape(1, -1), ssm_norm.reshape(1, -1), e_bf, et_bf)


def _head_masks(shape):
    lane = lax.broadcasted_iota(jnp.int32, shape, 1)
    return lane < HEAD, lane >= HEAD


def _rwkv_features(proj, prev, mu_ref, w0_ref, w2_ref, a0_ref, a2_ref, g2_ref, kk_ref, ka_ref, dim):
    xs = proj + (prev - proj) * mu_ref[...]
    k = xs[:, dim:2 * dim]
    t_wa = xs[:, 3 * dim:3 * dim + LANES]
    xg = xs[:, 3 * dim + LANES:3 * dim + 2 * LANES]
    w = -_softplus(-(w0_ref[...] + _dot(jnp.tanh(t_wa).astype(BF16), w2_ref[...]))) - 0.5
    a = _sigmoid(a0_ref[...] + _dot(t_wa.astype(BF16), a2_ref[...]))
    g = _dot(_sigmoid(xg).astype(BF16), g2_ref[...])
    return (xs[:, 0:dim], k * (1.0 + (a - 1.0) * ka_ref[...]), xs[:, 2 * dim:3 * dim], k * kk_ref[...], a,
            -jnp.exp(w), g)


def _rwkv_chunk_kernel(p_ref, mu_ref, w0_ref, w2_ref, a0_ref, a2_ref, g2_ref, kk_ref, ka_ref, rk_ref,
                       lng_ref, lnb_ref, ones2_ref, o_ref, s_ref,
                       ext_ref, r_s, k_s, v_s, kkn_s, a_s, lw_s, g_s, x_s, n_s, q_s, yb_s, bkh_s, pc_s, *, dim):
    c = pl.program_id(1)
    C = p_ref.shape[1]
    n_pairs = dim // LANES

    @pl.when(c == 0)
    def _():
        ext_ref[0:8, :] = jnp.zeros((8, ext_ref.shape[1]), F32)
        s_ref[...] = jnp.zeros(s_ref.shape, F32)

    @pl.when(c > 0)
    def _():
        ext_ref[0:8, :] = ext_ref[C:C + 8, :]

    proj = p_ref[0].astype(F32)
    ext_ref[8:C + 8, :] = proj
    (r_s[...], k_s[...], v_s[...], kkn_s[...], a_s[...], lw_s[...], g_s[...]) = _rwkv_features(
        proj, ext_ref[7:C + 7, :], mu_ref, w0_ref, w2_ref, a0_ref, a2_ref, g2_ref, kk_ref, ka_ref, dim)

    tril = _tril(C)
    tril_strict = _tril(C, -1)
    tril_bf = tril.astype(BF16)
    ones2 = ones2_ref[...]
    m0, m1 = _head_masks((C, LANES))
    row = lax.broadcasted_iota(jnp.int32, (LANES, LANES), 0)
    col = lax.broadcasted_iota(jnp.int32, (LANES, LANES), 1)
    blockdiag = (row < HEAD) == (col < HEAD)
    n_levels = C.bit_length() - 1
    assert 1 << n_levels == C

    def by_head(x):
        return jnp.concatenate([jnp.where(m0, x, 0.0), jnp.where(m1, x, 0.0)], axis=0)

    def head_sum(x):
        s0 = jnp.sum(jnp.where(m0, x, 0.0), axis=1, keepdims=True)
        s1 = jnp.sum(jnp.where(m1, x, 0.0), axis=1, keepdims=True)
        return jnp.where(m0, s0, s1)

    for j in range(n_pairs):
        cols = slice(j * LANES, (j + 1) * LANES)
        lw = lw_s[:, cols]
        logp = _dot_exact_lhs(tril_bf, lw)
        logpc = logp[C - 1:C, :]
        ep, epinv = jnp.exp(logp), jnp.exp(-logp)
        epc = jnp.exp(logpc - logp)
        kkf = kkn_s[:, cols]
        kk = kkf * lax.rsqrt(head_sum(kkf * kkf) + 1e-12)
        aj, kj, vj, rj = a_s[:, cols], k_s[:, cols], v_s[:, cols], r_s[:, cols]
        bvec = kk * aj
        at = -kk * jnp.exp(logp - lw)
        rt = rj * ep
        bk_bf = jnp.concatenate([bvec * epinv, kj * epinv], axis=0).astype(BF16)
        ar = jnp.concatenate([at, rt], axis=0)
        as_ = _dot_nt(ar.astype(BF16), s_ref[0, j].astype(BF16))
        mab, mak, qab, qak = [], [], [], []
        for hm in (m0, m1):
            g = _dot_nt(jnp.where(jnp.concatenate([hm, hm], axis=0), ar, 0.0).astype(BF16), bk_bf)
            mab.append(jnp.where(tril_strict, g[0:C, 0:C], 0.0))
            mak.append(jnp.where(tril_strict, g[0:C, C:2 * C], 0.0))
            qab.append(jnp.where(tril, g[C:2 * C, 0:C], 0.0))
            qak.append(jnp.where(tril, g[C:2 * C, C:2 * C], 0.0))
        x_s[j] = as_[0:C] + _dot(jnp.concatenate(mak, axis=1).astype(BF16), by_head(vj).astype(BF16))
        n_s[j] = jnp.concatenate(mab, axis=1).astype(BF16)
        q_s[j] = jnp.concatenate(qab + qak, axis=1).astype(BF16)
        yb_s[j] = as_[C:2 * C]
        bkh_s[j] = jnp.concatenate([bvec * epc, kj * epc], axis=0).astype(BF16)
        pc_s[j] = jnp.broadcast_to(jnp.exp(logpc), (8, LANES))

    zero = jnp.zeros((C, C), BF16)
    for lvl in range(n_levels):
        for j in range(n_pairs):
            n = n_s[j]
            x_s[j] += _dot(n, by_head(x_s[j]).astype(BF16))
            if lvl + 1 < n_levels:
                nd = jnp.concatenate([jnp.concatenate([n[:, 0:C], zero], axis=1),
                                      jnp.concatenate([zero, n[:, C:2 * C]], axis=1)], axis=0)
                n_s[j] = _dot(n, nd).astype(BF16)

    for j in range(n_pairs):
        cols = slice(j * LANES, (j + 1) * LANES)
        kj, vj, rj = k_s[:, cols], v_s[:, cols], r_s[:, cols]
        x = x_s[j]
        y = yb_s[j] + _dot(q_s[j], jnp.concatenate([by_head(x), by_head(vj)], axis=0).astype(BF16))
        upd = _dot_tn(jnp.concatenate([x, vj], axis=0).astype(BF16), bkh_s[j])
        s_ref[0, j] = pc_s[j, 0:1, :] * s_ref[0, j] + jnp.where(blockdiag, upd, 0.0)

        mean = head_sum(y) * (1.0 / HEAD)
        d = y - mean
        var = head_sum(d * d) * (1.0 / HEAD)
        yn = d * lax.rsqrt(var + LN_X_EPS) * lng_ref[:, cols] + lnb_ref[:, cols]
        bonus = head_sum(rj * kj * rk_ref[:, cols]) * vj
        o_ref[0, :, cols] = ((yn + bonus) * g_s[:, cols]).astype(o_ref.dtype)


def rwkv_prompt(proj, shift_mu, decay_w0, decay_w2, aaa_a0, aaa_a2, gate_g2, k_k, k_a, r_k, lnx_g, lnx_b):
    b, l, sd = proj.shape
    dim = decay_w0.size
    n_heads = dim // HEAD
    dl, al, gl = decay_w2.shape[0], aaa_a2.shape[0], gate_g2.shape[0]
    assert dl + al == LANES and gl == LANES and sd == 3 * dim + 2 * LANES and l % CHUNK == 0
    C = CHUNK
    n_pairs = dim // LANES
    w2 = jnp.concatenate([decay_w2, jnp.zeros((al, dim), F32)], axis=0).astype(BF16)
    a2 = jnp.concatenate([jnp.zeros((dl, dim), F32), aaa_a2], axis=0).astype(BF16)
    ones2 = jnp.asarray(np.arange(LANES)[:, None] // HEAD == np.arange(LANES)[None, :] // HEAD, BF16)
    row = lambda v: v.reshape(1, -1)
    full = lambda shape: pl.BlockSpec(shape, lambda i, j: (0,) * len(shape))
    vec = full((1, dim))
    y, s = pl.pallas_call(
        functools.partial(_rwkv_chunk_kernel, dim=dim),
        grid=(b, l // C),
        in_specs=[pl.BlockSpec((1, C, sd), lambda i, j: (i, j, 0)), full((1, sd)), vec, full((LANES, dim)), vec,
                  full((LANES, dim)), full((LANES, dim)), vec, vec, vec, vec, vec, full((LANES, LANES))],
        out_specs=[pl.BlockSpec((1, C, dim), lambda i, j: (i, j, 0)),
                   pl.BlockSpec((1, dim // LANES, LANES, LANES), lambda i, j: (i, 0, 0, 0))],
        out_shape=[jax.ShapeDtypeStruct((b, l, dim), BF16),
                   jax.ShapeDtypeStruct((b, dim // LANES, LANES, LANES), F32)],
        scratch_shapes=[pltpu.VMEM((C + 8, sd), F32)] + [pltpu.VMEM((C, dim), F32)] * 7
                       + [pltpu.VMEM((n_pairs, C, LANES), F32), pltpu.VMEM((n_pairs, C, 2 * C), BF16),
                          pltpu.VMEM((n_pairs, C, 4 * C), BF16), pltpu.VMEM((n_pairs, C, LANES), F32),
                          pltpu.VMEM((n_pairs, 2 * C, LANES), BF16), pltpu.VMEM((n_pairs, 8, LANES), F32)],
        compiler_params=_cparams(2),
        name="rwkv_prompt",
    )(proj, row(shift_mu), row(decay_w0), w2, row(aaa_a0), a2, gate_g2.astype(BF16), row(k_k), row(k_a),
      row(r_k), row(lnx_g), row(lnx_b), ones2)
    s = s.reshape(b, dim // LANES, 2, HEAD, 2, HEAD)
    s = jnp.stack([s[:, :, 0, :, 0, :], s[:, :, 1, :, 1, :]], axis=2).reshape(b, n_heads, HEAD, HEAD)
    return y, s


def _mix_kernel(x_ref, ym_ref, yr_ref, gate_ref, wm_ref, wr_ref, wo_ref, nf_ref, wq_ref, x2_ref, q_ref):
    d = x_ref.shape[1]
    y_m = _dot(ym_ref[...].astype(BF16), wm_ref[...])
    y_r = _dot(yr_ref[...].astype(BF16), wr_ref[...])
    mix = (_sigmoid(gate_ref[:, 0:d].astype(F32)) * y_m
           + _sigmoid(gate_ref[:, d:2 * d].astype(F32)) * y_r)
    x2 = x_ref[...] + _dot(mix.astype(BF16), wo_ref[...])
    x2_ref[...] = x2
    q_ref[...] = _dot(_rms(x2, nf_ref[...]).astype(BF16), wq_ref[...]).astype(q_ref.dtype)


def mix_and_query(x, ym, yr, gates, w_out_ssm, w_out_rwkv, w_out, norm_ffn, peer_wq, tm):
    m, d = x.shape
    tm = min(tm, m)
    assert m % tm == 0
    rows = lambda n: pl.BlockSpec((tm, n), lambda i: (i, 0))
    full = lambda a: pl.BlockSpec(a.shape, lambda i: (0, 0))
    nq = peer_wq.shape[1]
    args = (x, ym, yr, gates, w_out_ssm, w_out_rwkv, w_out, norm_ffn.reshape(1, -1), peer_wq)
    return pl.pallas_call(
        _mix_kernel,
        grid=(m // tm,),
        in_specs=[rows(d), rows(ym.shape[1]), rows(yr.shape[1]), rows(2 * d)] + [full(a) for a in args[4:]],
        out_specs=[rows(d), rows(nq)],
        out_shape=[jax.ShapeDtypeStruct((m, d), F32), jax.ShapeDtypeStruct((m, nq), BF16)],
        compiler_params=_cparams(1),
        name="mix_and_query",
    )(*args)


PEER_TOPK = 16
N_KEYS = 128


def _erf_gelu(x):
    return 0.5 * x * (1.0 + lax.erf(x * np.float32(1.0 / np.sqrt(2.0))))


NOT_TOP = 64.0


def _kth_largest(s, k, want_rank=False):
    tops = []
    rank = jnp.full(s.shape, NOT_TOP, F32)
    for r in range(k):
        m = jnp.max(s, axis=0, keepdims=True)
        tops.append(m)
        hit = s == m
        if want_rank:
            rank = jnp.where(hit, float(r), rank)
        s = jnp.where(hit, -jnp.inf, s)
    return (tops, rank) if want_rank else tops


def _peer_kernel(x2_ref, q_ref, p_ref, k1_ref, k2_ref, *rest, n_heads, n_e, n_split):
    u_refs, vt_refs = rest[:n_split], rest[n_split:2 * n_split]
    (nf_ref, npl_ref, wg_ref, wp_ref, nfin_ref, y_ref, hb_s, n1_s, c_s, rk2_s, d_s, act0_s, act1_s, w0_s, w1_s,
     acc_s) = rest[2 * n_split:]
    j = pl.program_id(1)
    tt = x2_ref.shape[0]
    ec = n_split * u_refs[0].shape[0]
    qd = N_KEYS
    assert ec == 8 * N_KEYS

    @pl.when(j == 0)
    def _():
        hb_s[...] = _rms(x2_ref[...], nf_ref[...]).T.astype(BF16)
        acc_s[...] = jnp.zeros(acc_s.shape, F32)
        sub8 = lax.broadcasted_iota(jnp.int32, (8, LANES), 0)
        n_ts = tt // LANES

        def select_experts(unit, carry):
            h = unit // n_ts
            tok = pl.ds(pl.multiple_of((unit % n_ts) * LANES, LANES), LANES)
            q1 = q_ref[tok, pl.ds(pl.multiple_of(2 * h * qd, qd), qd)]
            q2 = q_ref[tok, pl.ds(pl.multiple_of((2 * h + 1) * qd, qd), qd)]
            s1 = _dot_nt(k1_ref[h], q1)
            s2 = _dot_nt(k2_ref[h], q2)
            v1 = _kth_largest(s1, PEER_TOPK)
            v2, rank2 = _kth_largest(s2, PEER_TOPK, want_rank=True)
            v2lo = jnp.concatenate(v2[0:8], axis=0)
            pieces = [v1[0] + v2lo, v1[0] + jnp.concatenate(v2[8:16], axis=0), v1[1] + v2lo]
            for k1 in range(2, 8):
                pieces.append(jnp.where(sub8 < PEER_TOPK // (k1 + 1), v1[k1] + v2lo, -jnp.inf))
            pieces.append(jnp.concatenate(v1[8:16], axis=0) + v2[0])
            cand = jnp.concatenate(pieces, axis=0)
            top = _kth_largest(cand, PEER_TOPK)
            z = sum(jnp.exp(t - top[0]) for t in top)
            th = top[PEER_TOPK - 1]
            n1_s[h, :, tok] = sum(jnp.where(s1 + v >= th, 1.0, 0.0) for v in v2)
            c_s[h, :, tok] = jnp.exp(s1 - v1[0]) / z
            rk2_s[h, :, tok] = rank2.astype(BF16)
            d_s[h, :, tok] = jnp.exp(s2 - v2[0]).astype(BF16)
            return carry

        lax.fori_loop(0, n_heads * n_ts, select_experts, 0, unroll=2)

    act = (act0_s, act1_s)
    wts = (w0_s, w1_s)

    def step(par, do_a, do_b, do_c):
        i_rows = pl.ds(pl.multiple_of(jnp.clip(j - 1, 0, n_e - 1) * 8, 8), 8)
        jh = N_KEYS // 2
        n_blocks = (tt // LANES) * (N_KEYS // jh)
        d = n_split * vt_refs[0].shape[1]
        for ts in range(tt // LANES):
            tok = slice(ts * LANES, (ts + 1) * LANES)
            for jb in range(N_KEYS // jh):
                blk = ts * (N_KEYS // jh) + jb
                ra = slice(blk * (ec // n_blocks), (blk + 1) * (ec // n_blocks))
                rc = slice(blk * (d // n_blocks), (blk + 1) * (d // n_blocks))
                per = n_blocks // n_split
                la = slice((blk % per) * (ec // n_blocks), (blk % per + 1) * (ec // n_blocks))
                lc = slice((blk % per) * (d // n_blocks), (blk % per + 1) * (d // n_blocks))
                if do_a:
                    act[par][ra, :] = _dot(u_refs[blk // per][la, :], hb_s[...])
                if do_c:
                    acc_s[rc, :] += _dot(vt_refs[blk // per][0, lc, :], wts[par][...])
                if not do_b:
                    continue
                jrows = slice(jb * jh, (jb + 1) * jh)
                acc = [jnp.zeros((jh, LANES), BF16)] * 8
                for h in range(n_heads):
                    rk, dd = rk2_s[h, jrows, tok], d_s[h, jrows, tok]
                    n1b, cb = n1_s[h, i_rows, tok], c_s[h, i_rows, tok]
                    for il in range(8):
                        sel = rk < n1b[il:il + 1].astype(BF16)
                        acc[il] = acc[il] + jnp.where(sel, dd, 0.0) * cb[il:il + 1].astype(BF16)
                for il in range(8):
                    rows = slice(il * N_KEYS + jb * jh, il * N_KEYS + (jb + 1) * jh)
                    wts[1 - par][rows, tok] = acc[il] * _erf_gelu(act[1 - par][rows, tok]).astype(BF16)

    assert n_e % 2 == 0 and n_e >= 4
    pl.when(j == 0)(lambda: step(0, True, False, False))
    pl.when(j == 1)(lambda: step(1, True, True, False))
    pl.when((j >= 2) & (j < n_e) & (j % 2 == 0))(lambda: step(0, True, True, True))
    pl.when((j >= 2) & (j < n_e) & (j % 2 == 1))(lambda: step(1, True, True, True))
    pl.when(j == n_e)(lambda: step(0, False, True, True))
    pl.when(j == n_e + 1)(lambda: step(1, False, False, True))

    @pl.when(j == n_e + 1)
    def _():
        x3 = x2_ref[...] + acc_s[...].T
        gate = _sigmoid(_dot(_rms(x3, npl_ref[...]).astype(BF16), wg_ref[...]))
        x4 = x3 + gate * _dot(p_ref[...].astype(BF16), wp_ref[...])
        y_ref[...] = _rms(x4, nfin_ref[...])


def peer_ple_final(x2, q, p, peer_k1, peer_k2, peer_u, peer_vt, norm_ffn, norm_ple, w_ple_gate, w_ple_proj,
                   norm_final, tt, ec):
    t, d = x2.shape
    n_heads = peer_k1.shape[0]
    n_exp = peer_u.shape[0]
    tt = min(tt, t)
    assert t % tt == 0 and n_exp % ec == 0 and tt % LANES == 0 and ec % N_KEYS == 0
    assert peer_k1.shape[1:] == (N_KEYS, N_KEYS) and n_exp == N_KEYS * N_KEYS
    rows = lambda n: pl.BlockSpec((tt, n), lambda i, j: (i, 0))
    full = lambda a: pl.BlockSpec(a.shape, lambda i, j: (0,) * a.ndim)
    row = lambda v: v.reshape(1, -1)
    consts = (row(norm_ffn), row(norm_ple), w_ple_gate, w_ple_proj, row(norm_final))
    n_e = n_exp // ec
    assert peer_vt.shape == (n_e, d, ec)
    hs = lambda dt: pltpu.VMEM((n_heads, N_KEYS, tt), dt)
    n_split = min(4, 2 * (tt // LANES))
    u_map = lambda s, i, j: (jnp.minimum(j, n_e - 1) * n_split + s, 0)
    vt_map = lambda s, i, j: (jnp.clip(j - 2, 0, n_e - 1), s, 0)
    return pl.pallas_call(
        functools.partial(_peer_kernel, n_heads=n_heads, n_e=n_e, n_split=n_split),
        grid=(t // tt, n_e + 2),
        in_specs=[rows(d), rows(q.shape[1]), rows(p.shape[1]), full(peer_k1), full(peer_k2),
                  *[pl.BlockSpec((ec // n_split, d), functools.partial(u_map, s)) for s in range(n_split)],
                  *[pl.BlockSpec((1, d // n_split, ec), functools.partial(vt_map, s)) for s in range(n_split)]]
                 + [full(a) for a in consts],
        out_specs=rows(d),
        out_shape=jax.ShapeDtypeStruct((t, d), F32),
        scratch_shapes=[pltpu.VMEM((d, tt), BF16), hs(F32), hs(F32), hs(BF16), hs(BF16),
                        pltpu.VMEM((ec, tt), F32), pltpu.VMEM((ec, tt), F32),
                        pltpu.VMEM((ec, tt), BF16), pltpu.VMEM((ec, tt), BF16), pltpu.VMEM((d, tt), F32)],
        compiler_params=_cparams(2),
        name="peer_ple_final",
    )(x2, q, p, peer_k1, peer_k2, *([peer_u] * n_split), *([peer_vt] * n_split), *consts)


def _as_column(x_row):
    n = x_row.shape[1]
    eye = lax.broadcasted_iota(jnp.int32, (n, n), 0) == lax.broadcasted_iota(jnp.int32, (n, n), 1)
    return jnp.sum(jnp.where(eye, jnp.broadcast_to(x_row, (n, n)), 0.0), axis=1, keepdims=True)


def _rows8(x_row):
    return jnp.broadcast_to(x_row, (8, x_row.shape[1]))


def _ssd_step_kernel(xbc_ref, z_ref, dt_ref, conv_ref, *rest, n_heads, n_split):
    h_refs = rest[:n_split]
    convw_ref, convb_ref, dtb_ref, alog_ref, dskip_ref, normg_ref, e_ref, y_ref, ho_ref = rest[n_split:]
    d_inner = n_heads * HEAD
    gw = d_inner // SSM_GROUPS
    cs = conv_ref[0]
    conv = convb_ref[...] + xbc_ref[0] * convw_ref[CONV_WIDTH - 1:CONV_WIDTH, :]
    for k in range(CONV_WIDTH - 1):
        conv = conv + cs[k:k + 1, :] * convw_ref[k:k + 1, :]
    act = conv * _sigmoid(conv)
    dt = _softplus(dt_ref[0] + dtb_ref[...])
    da = jnp.exp(dt * -jnp.exp(alog_ref[...]))
    e = e_ref[...]
    dt_x = _dot_exact_rhs(_rows8(dt), e)[0:1]
    da_x = _dot_exact_rhs(_rows8(da), e)[0:1]
    for g in range(SSM_GROUPS):
        cols = slice(g * gw, (g + 1) * gw)
        xg = act[:, cols]
        bg = act[:, d_inner + g * SSM_STATE:d_inner + (g + 1) * SSM_STATE]
        cg = act[:, d_inner + (SSM_GROUPS + g) * SSM_STATE:d_inner + (SSM_GROUPS + g + 1) * SSM_STATE]
        hp = h_refs[g // (SSM_GROUPS // n_split)]
        lrows = slice((g % (SSM_GROUPS // n_split)) * gw, (g % (SSM_GROUPS // n_split) + 1) * gw)
        hn = _as_column(da_x[:, cols]) * hp[0, lrows, :] + _as_column(xg * dt_x[:, cols]) * bg
        ho_ref[0, cols, :] = hn
        y = _dot_nt(_rows8(cg).astype(BF16), hn.astype(BF16))[0:1] + xg * dskip_ref[:, cols]
        zg = z_ref[0, :, cols]
        y = y * (zg * _sigmoid(zg))
        y_ref[0, :, cols] = _rms(y, normg_ref[:, cols])


def ssd_step(u, state_conv, state_ssm, conv_w, conv_b, dt_bias, a_log, d_skip, ssm_norm, n_heads):
    b = u.shape[0]
    d_inner = n_heads * HEAD
    conv_dim = d_inner + 2 * SSM_GROUPS * SSM_STATE
    pad = lambda v: jnp.pad(v.reshape(1, -1), ((0, 0), (0, LANES - v.size)))
    e_bf = jnp.asarray(np.arange(d_inner)[None, :] // HEAD == np.arange(LANES)[:, None], BF16)
    full = lambda shape: pl.BlockSpec(shape, lambda i: (0,) * len(shape))
    u3 = u.reshape(b, 1, -1)
    n_split = 4
    y, h = pl.pallas_call(
        functools.partial(_ssd_step_kernel, n_heads=n_heads, n_split=n_split),
        grid=(b,),
        in_specs=[pl.BlockSpec((1, 1, conv_dim), lambda i: (i, 0, 0)),
                  pl.BlockSpec((1, 1, d_inner), lambda i: (i, 0, conv_dim // d_inner)),
                  pl.BlockSpec((1, 1, LANES), lambda i: (i, 0, (conv_dim + d_inner) // LANES)),
                  pl.BlockSpec((1, CONV_WIDTH - 1, conv_dim), lambda i: (i, 0, 0)),
                  *[pl.BlockSpec((1, d_inner // n_split, SSM_STATE), functools.partial(lambda s, i: (i, s, 0), s))
                    for s in range(n_split)],
                  full((CONV_WIDTH, conv_dim)), full((1, conv_dim)), full((1, LANES)), full((1, LANES)),
                  full((1, d_inner)), full((1, d_inner)), full((LANES, d_inner))],
        out_specs=[pl.BlockSpec((1, 1, d_inner), lambda i: (i, 0, 0)),
                   pl.BlockSpec((1, d_inner, SSM_STATE), lambda i: (i, 0, 0))],
        out_shape=[jax.ShapeDtypeStruct((b, 1, d_inner), F32),
                   jax.ShapeDtypeStruct((b, d_inner, SSM_STATE), F32)],
        compiler_params=_cparams(1),
        name="ssd_step",
    )(u3, u3, u3, state_conv, *([state_ssm.reshape(b, d_inner, SSM_STATE)] * n_split), conv_w, conv_b.reshape(1, -1),
      pad(dt_bias), pad(a_log), jnp.repeat(d_skip, HEAD).reshape(1, -1), ssm_norm.reshape(1, -1), e_bf)
    return y.reshape(b, d_inner), h


def _rwkv_step_features_kernel(p_ref, prev_ref, mu_ref, w0_ref, w2_ref, a0_ref, a2_ref, g2_ref, kk_ref, ka_ref,
                               ones2_ref, o_ref, *, dim):
    r, k, v, kkf, a, lw, g = _rwkv_features(p_ref[...], prev_ref[...], mu_ref, w0_ref, w2_ref, a0_ref, a2_ref,
                                            g2_ref, kk_ref, ka_ref, dim)
    for j in range(dim // LANES):
        cols = slice(j * LANES, (j + 1) * LANES)
        kj = kkf[:, cols]
        o_ref[3, :, cols] = kj * lax.rsqrt(_dot_exact_rhs(kj * kj, ones2_ref[...]) + 1e-12)
    o_ref[0], o_ref[1], o_ref[2], o_ref[4], o_ref[5], o_ref[6] = r, k, v, a, jnp.exp(lw), g


def _rwkv_step_kernel(f_ref, s_ref, rk_ref, lng_ref, lnb_ref, y_ref, so_ref, y_s):
    n_heads = s_ref.shape[1]
    eye = lax.broadcasted_iota(jnp.int32, (HEAD, HEAD), 0) == lax.broadcasted_iota(jnp.int32, (HEAD, HEAD), 1)
    for h in range(n_heads):
        row = lambda i: f_ref[i, 0, h:h + 1, :]
        r, k, v, kk, a, w = (row(i) for i in range(6))
        s = s_ref[0, h]
        sa = jnp.sum(s * -kk, axis=1, keepdims=True)
        v_col = jnp.sum(jnp.where(eye, jnp.broadcast_to(v, (HEAD, HEAD)), 0.0), axis=1, keepdims=True)
        sn = s * w + sa * (kk * a) + v_col * k
        so_ref[0, h] = sn
        y_s[h:h + 1, :] = _dot_nt(_rows8(r).astype(BF16), sn.astype(BF16))[0:1]
    y = y_s[...]
    r, k, v, g = f_ref[0, 0], f_ref[1, 0], f_ref[2, 0], f_ref[6, 0]
    d = y - jnp.mean(y, axis=-1, keepdims=True)
    var = jnp.mean(d * d, axis=-1, keepdims=True)
    yn = d * lax.rsqrt(var + LN_X_EPS) * lng_ref[...] + lnb_ref[...]
    bonus = jnp.sum(r * k * rk_ref[...], axis=-1, keepdims=True) * v
    y_ref[0] = (yn + bonus) * g


def rwkv_step(proj, shift_prev, state_wkv, shift_mu, decay_w0, decay_w2, aaa_a0, aaa_a2, gate_g2, k_k, k_a, r_k,
              lnx_g, lnx_b):
    b, sd = proj.shape
    dim = decay_w0.size
    n_heads = dim // HEAD
    dl, al, gl = decay_w2.shape[0], aaa_a2.shape[0], gate_g2.shape[0]
    assert dl + al == LANES and gl == LANES and sd == 3 * dim + 2 * LANES
    w2 = jnp.concatenate([decay_w2, jnp.zeros((al, dim), F32)], axis=0).astype(BF16)
    a2 = jnp.concatenate([jnp.zeros((dl, dim), F32), aaa_a2], axis=0).astype(BF16)
    ones2 = jnp.asarray(np.arange(LANES)[:, None] // HEAD == np.arange(LANES)[None, :] // HEAD, BF16)
    row = lambda v: v.reshape(1, -1)
    args = (proj, shift_prev, row(shift_mu), row(decay_w0), w2, row(aaa_a0), a2, gate_g2.astype(BF16), row(k_k),
            row(k_a), ones2)
    feats = pl.pallas_call(
        functools.partial(_rwkv_step_features_kernel, dim=dim),
        grid=(1,),
        in_specs=[pl.BlockSpec(a.shape, lambda i: (0, 0)) for a in args],
        out_specs=pl.BlockSpec((7, b, dim), lambda i: (0, 0, 0)),
        out_shape=jax.ShapeDtypeStruct((7, b, dim), F32),
        compiler_params=_cparams(1),
        name="rwkv_step_features",
    )(*args)
    hv = lambda v: v.reshape(n_heads, HEAD)
    full = pl.BlockSpec((n_heads, HEAD), lambda i: (0, 0))
    y, s = pl.pallas_call(
        _rwkv_step_kernel,
        grid=(b,),
        in_specs=[pl.BlockSpec((7, 1, n_heads, HEAD), lambda i: (0, i, 0, 0)),
                  pl.BlockSpec((1, n_heads, HEAD, HEAD), lambda i: (i, 0, 0, 0)), full, full, full],
        out_specs=[pl.BlockSpec((1, n_heads, HEAD), lambda i: (i, 0, 0)),
                   pl.BlockSpec((1, n_heads, HEAD, HEAD), lambda i: (i, 0, 0, 0))],
        out_shape=[jax.ShapeDtypeStruct((b, n_heads, HEAD), F32),
                   jax.ShapeDtypeStruct((b, n_heads, HEAD, HEAD), F32)],
        scratch_shapes=[pltpu.VMEM((n_heads, HEAD), F32)],
        compiler_params=_cparams(1),
        name="rwkv_step",
    )(feats.reshape(7, b, n_heads, HEAD), state_wkv, hv(r_k), hv(lnx_g), hv(lnx_b))
    return y.reshape(b, dim), s


def _layer(x, p, states, wts, n_ssm_heads):
    b, l, d = x.shape
    xt = x.reshape(b * l, d)
    g_mix = wts['norm_mix'].reshape(1, -1)
    tm = 1024
    u_dtype = F32 if states is not None else BF16
    u_ssm = norm_matmul(xt, g_mix, wts['w_ssm'], tm, wts['w_ssm'].shape[1] // 7, u_dtype)
    u_rwkv = norm_matmul(xt, g_mix, wts['w_shift'], tm, wts['w_shift'].shape[1] // 2, u_dtype)
    u_gate = norm_matmul(xt, g_mix, wts['w_gates'], tm, wts['w_gates'].shape[1] // 2, u_dtype)
    d_inner = n_ssm_heads * HEAD
    conv_dim = d_inner + 2 * SSM_GROUPS * SSM_STATE
    ssd_w = (wts['conv_w'], wts['conv_b'], wts['dt_bias'], wts['a_log'], wts['d_skip'], wts['ssm_norm'])
    rwkv_w = tuple(wts[k] for k in ('shift_mu', 'decay_w0', 'decay_w2', 'aaa_a0', 'aaa_a2', 'gate_g2', 'k_k', 'k_a',
                                    'r_k', 'lnx_g', 'lnx_b'))
    if states is None:
        ym, ssm_new = ssd_prompt(u_ssm.reshape(b, l, -1), *ssd_w, n_ssm_heads)
        ym = ym.reshape(b * l, d_inner)
        tail = x[:, l - (CONV_WIDTH - 1):].reshape(b * (CONV_WIDTH - 1), d)
        conv_new = norm_matmul(tail, g_mix, wts['w_ssm'], tm, wts['w_ssm'].shape[1] // 7)[:, :conv_dim]
        conv_new = conv_new.reshape(b, CONV_WIDTH - 1, conv_dim)
        yr, wkv_new = rwkv_prompt(u_rwkv.reshape(b, l, -1), *rwkv_w)
        yr = yr.reshape(b * l, -1)
        shift_new = norm_matmul(tail, g_mix, wts['w_shift'], tm, wts['w_shift'].shape[1] // 2)
        shift_new = shift_new.reshape(b, CONV_WIDTH - 1, -1)[:, CONV_WIDTH - 2]
    else:
        conv_prev, ssm_prev, wkv_prev, shift_prev = states
        ym, ssm_new = ssd_step(u_ssm, conv_prev, ssm_prev, *ssd_w, n_ssm_heads)
        conv_new = jnp.concatenate([conv_prev[:, 1:], u_ssm[:, None, :conv_dim]], axis=1)
        yr, wkv_new = rwkv_step(u_rwkv, shift_prev, wkv_prev, *rwkv_w)
        shift_new = u_rwkv
    x2, q = mix_and_query(xt, ym, yr, u_gate, wts['w_out_ssm'], wts['w_out_rwkv'], wts['w_out'], wts['norm_ffn'],
                          wts['peer_wq'], 512)
    y = peer_ple_final(x2, q, p.reshape(b * l, -1), wts['peer_k1'], wts['peer_k2'], wts['peer_u'], wts['peer_vt'],
                       wts['norm_ffn'], wts['norm_ple'], wts['w_ple_gate'], wts['w_ple_proj'], wts['norm_final'],
                       512, 8 * N_KEYS)
    return (y.reshape(b, l, d), ssm_new.reshape(b, n_ssm_heads, HEAD, SSM_STATE), conv_new, wkv_new, shift_new)


def kernel(x_prompt, x_sample, p_prompt, p_sample, state_ssm, state_conv, state_wkv, state_shift, norm_mix, w_in,
           conv_w, conv_b, dt_bias, a_log, d_skip, ssm_norm, w_out_ssm, shift_mu, decay_w0, decay_w2, aaa_a0, aaa_a2,
           gate_g2, k_k, k_a, r_k, lnx_g, lnx_b, w_out_rwkv, w_out, norm_ffn, peer_wq, peer_k1, peer_k2, peer_u,
           peer_v, norm_ple, w_ple_gate, w_ple_proj, norm_final):
    depth = w_in.shape[0]
    assert depth == 1, "single-layer trunk"
    d_model = x_prompt.shape[-1]
    n_ssm_heads = dt_bias.shape[1]
    d_inner = n_ssm_heads * HEAD
    conv_dim = conv_w.shape[2]
    shift_dim = shift_mu.shape[1]
    bf = lambda a: a.astype(BF16)
    o = np.cumsum([0, d_inner, conv_dim, n_ssm_heads, shift_dim, d_model, d_model])
    wi = w_in[0]
    wts = {
        'w_ssm': bf(jnp.concatenate([wi[:, o[1]:o[2]], wi[:, o[0]:o[1]], wi[:, o[2]:o[3]],
                                     jnp.zeros((d_model, LANES - n_ssm_heads), F32)], axis=1)),
        'w_shift': bf(wi[:, o[3]:o[4]]),
        'w_gates': bf(wi[:, o[4]:o[6]]),
        'w_out_ssm': bf(w_out_ssm[0]), 'w_out_rwkv': bf(w_out_rwkv[0]), 'w_out': bf(w_out[0]),
        'peer_wq': bf(peer_wq[0]), 'peer_k1': bf(peer_k1[0]), 'peer_k2': bf(peer_k2[0]),
        'peer_u': bf(peer_u[0]),
        'peer_vt': bf(peer_v[0]).reshape(-1, 8 * N_KEYS, d_model).transpose(0, 2, 1),
        'w_ple_gate': bf(w_ple_gate[0]), 'w_ple_proj': bf(w_ple_proj[0]), 'norm_final': norm_final,
    }
    for name, val in (('norm_mix', norm_mix), ('conv_w', conv_w), ('conv_b', conv_b), ('dt_bias', dt_bias),
                      ('a_log', a_log), ('d_skip', d_skip), ('ssm_norm', ssm_norm), ('shift_mu', shift_mu),
                      ('decay_w0', decay_w0), ('decay_w2', decay_w2), ('aaa_a0', aaa_a0), ('aaa_a2', aaa_a2),
                      ('gate_g2', gate_g2), ('k_k', k_k), ('k_a', k_a), ('r_k', r_k), ('lnx_g', lnx_g),
                      ('lnx_b', lnx_b), ('norm_ffn', norm_ffn), ('norm_ple', norm_ple)):
        wts[name] = val[0]
    yp, ssm_p, conv_p, wkv_p, shift_p = _layer(x_prompt, p_prompt[0], None, wts, n_ssm_heads)
    ys, ssm_s, conv_s, wkv_s, shift_s = _layer(
        x_sample, p_sample[0], (state_conv[0], state_ssm[0], state_wkv[0], state_shift[0]), wts, n_ssm_heads)
    return (yp, ys, ssm_p[None], conv_p[None], wkv_p[None], shift_p[None],
            ssm_s[None], conv_s[None], wkv_s[None], shift_s[None])
```

```python
import functools

import numpy as np
import jax
import jax.numpy as jnp
from jax import lax
from jax.experimental import pallas as pl
from jax.experimental.pallas import tpu as pltpu

F32 = jnp.float32
BF16 = jnp.bfloat16

EPS = 1e-6
LN_X_EPS = 64e-5
HEAD = 64
SSM_STATE = 128
SSM_GROUPS = 8
CONV_WIDTH = 4
CHUNK = 128
LANES = 128
VMEM_LIMIT = 56 * 1024 * 1024

PROJ_ROWS = 1024
PROJ_COLS_MAX = 1792
MIX_ROWS = 512
PEER_ROWS = 512


def _proj_col_tile(n):
    return max(c for c in range(LANES, PROJ_COLS_MAX + 1, LANES) if n % c == 0)


def _cparams(n_axes):
    return pltpu.CompilerParams(dimension_semantics=("arbitrary",) * n_axes,
                                vmem_limit_bytes=VMEM_LIMIT)


def _dot(a, b):
    return jnp.dot(a, b, preferred_element_type=F32)


def _dot_nt(a, b):
    return lax.dot_general(a, b, (((1,), (1,)), ((), ())), preferred_element_type=F32)


def _dot_tn(a, b):
    return lax.dot_general(a, b, (((0,), (0,)), ((), ())), preferred_element_type=F32)


def _split3(x):
    x1 = x.astype(BF16)
    r = x - x1.astype(F32)
    x2 = r.astype(BF16)
    x3 = (r - x2.astype(F32)).astype(BF16)
    return x1, x2, x3


def _dot_exact_rhs(a, e):
    a1, a2, a3 = _split3(a)
    return _dot(a1, e) + (_dot(a2, e) + _dot(a3, e))


def _dot_exact_lhs(e, a):
    a1, a2, a3 = _split3(a)
    return _dot(e, a1) + (_dot(e, a2) + _dot(e, a3))


def _rms(x, g):
    return x * lax.rsqrt(jnp.mean(x * x, axis=-1, keepdims=True) + EPS) * g


def _sigmoid(x):
    return 1.0 / (1.0 + jnp.exp(-x))


def _softplus(x):
    return jnp.maximum(x, 0.0) + jnp.log1p(jnp.exp(-jnp.abs(x)))


def _tril(n, k=0, dtype=F32):
    r = lax.broadcasted_iota(jnp.int32, (n, n), 0)
    c = lax.broadcasted_iota(jnp.int32, (n, n), 1)
    return (c <= r + k)


def _norm_matmul_kernel(x_ref, g_ref, w_ref, o_ref, h_ref):
    @pl.when(pl.program_id(1) == 0)
    def _():
        h_ref[...] = _rms(x_ref[...], g_ref[...]).astype(BF16)

    o_ref[...] = _dot(h_ref[...], w_ref[...])


def norm_matmul(x, g, w, tm, tn):
    m, k = x.shape
    n = w.shape[1]
    tm = min(tm, m)
    assert m % tm == 0 and n % tn == 0, (m, tm, n, tn)
    return pl.pallas_call(
        _norm_matmul_kernel,
        grid=(m // tm, n // tn),
        in_specs=[pl.BlockSpec((tm, k), lambda i, j: (i, 0)),
                  pl.BlockSpec((1, k), lambda i, j: (0, 0)),
                  pl.BlockSpec((k, tn), lambda i, j: (0, j))],
        out_specs=pl.BlockSpec((tm, tn), lambda i, j: (i, j)),
        out_shape=jax.ShapeDtypeStruct((m, n), F32),
        scratch_shapes=[pltpu.VMEM((tm, k), BF16)],
        compiler_params=_cparams(2),
        name="norm_matmul",
    )(x, g, w)


def _ssd_chunk_kernel(xbc_ref, z_ref, dt_ref, convw_ref, convb_ref, dtb_ref, alog_ref, dskip_ref,
                      normg_ref, e_ref, et_ref, y_ref, h_ref, ext_ref, act_ref, *, n_heads):
    c = pl.program_id(1)
    C = xbc_ref.shape[1]
    d_inner = n_heads * HEAD
    gw = d_inner // SSM_GROUPS
    hpg = n_heads // SSM_GROUPS

    @pl.when(c == 0)
    def _():
        ext_ref[0:8, :] = jnp.zeros((8, ext_ref.shape[1]), F32)
        h_ref[...] = jnp.zeros(h_ref.shape, F32)

    @pl.when(c > 0)
    def _():
        ext_ref[0:8, :] = ext_ref[C:C + 8, :]

    ext_ref[8:C + 8, :] = xbc_ref[0]
    conv = convb_ref[...]
    for k in range(CONV_WIDTH):
        off = 8 - (CONV_WIDTH - 1) + k
        conv = conv + ext_ref[off:off + C, :] * convw_ref[k:k + 1, :]
    act_ref[...] = conv * _sigmoid(conv)

    dt = _softplus(dt_ref[0] + dtb_ref[...])
    a = -jnp.exp(alog_ref[...])
    tril = _tril(C)
    acum = _dot_exact_lhs(tril.astype(BF16), dt * a)
    acum_t = acum.T
    dt_t = dt.T
    e = e_ref[...]
    eacum_x = _dot_exact_rhs(jnp.exp(acum), e)
    wdec_x = _dot_exact_rhs(jnp.exp(acum[C - 1:C, :] - acum) * dt, e)
    dec_b = jnp.broadcast_to(jnp.exp(acum_t[:, C - 1:C]), (LANES, SSM_STATE))
    lane = lax.broadcasted_iota(jnp.int32, (C, gw), 1)

    for g in range(SSM_GROUPS):
        cols = slice(g * gw, (g + 1) * gw)
        xg = act_ref[:, cols]
        bg = act_ref[:, d_inner + g * SSM_STATE:d_inner + (g + 1) * SSM_STATE].astype(BF16)
        cg = act_ref[:, d_inner + (SSM_GROUPS + g) * SSM_STATE:
                     d_inner + (SSM_GROUPS + g + 1) * SSM_STATE].astype(BF16)
        cb = _dot_nt(cg, bg)
        y = xg * dskip_ref[:, cols]
        for r in range(hpg):
            h = g * hpg + r
            seg = acum[:, h:h + 1] - acum_t[h:h + 1, :]
            m = cb * jnp.exp(jnp.where(tril, seg, -jnp.inf)) * dt_t[h:h + 1, :]
            xm = jnp.where((lane >= r * HEAD) & (lane < (r + 1) * HEAD), xg, 0.0)
            y = y + _dot(m.astype(BF16), xm.astype(BF16))
        hg = h_ref[0, cols, :]
        y = y + _dot_nt(cg, hg.astype(BF16)) * eacum_x[:, cols]
        zg = z_ref[0, :, cols]
        y = y * (zg * _sigmoid(zg))
        y = _rms(y, normg_ref[:, cols])
        y_ref[0, :, cols] = y.astype(y_ref.dtype)
        dec = _dot_exact_lhs(et_ref[cols, :], dec_b)
        h_ref[0, cols, :] = dec * hg + _dot_tn((xg * wdec_x[:, cols]).astype(BF16), bg)


def ssd_prompt(u, conv_w, conv_b, dt_bias, a_log, d_skip, ssm_norm, n_heads):
    b, l, _ = u.shape
    d_inner = n_heads * HEAD
    conv_dim = d_inner + 2 * SSM_GROUPS * SSM_STATE
    C = CHUNK
    assert l % C == 0 and conv_dim % d_inner == 0
    pad = lambda v: jnp.pad(v.reshape(1, -1), ((0, 0), (0, LANES - v.size)))
    e = (np.arange(d_inner)[None, :] // HEAD == np.arange(LANES)[:, None])
    e_bf = jnp.asarray(e, BF16)
    et_bf = jnp.asarray(e.T, BF16)
    full = lambda shape: pl.BlockSpec(shape, lambda i, j: (0,) * len(shape))
    return pl.pallas_call(
        functools.partial(_ssd_chunk_kernel, n_heads=n_heads),
        grid=(b, l // C),
        in_specs=[pl.BlockSpec((1, C, conv_dim), lambda i, j: (i, j, 0)),
                  pl.BlockSpec((1, C, d_inner), lambda i, j: (i, j, conv_dim // d_inner)),
                  pl.BlockSpec((1, C, LANES), lambda i, j: (i, j, (conv_dim + d_inner) // LANES)),
                  full((CONV_WIDTH, conv_dim)), full((1, conv_dim)), full((1, LANES)), full((1, LANES)),
                  full((1, d_inner)), full((1, d_inner)), full((LANES, d_inner)), full((d_inner, LANES))],
        out_specs=[pl.BlockSpec((1, C, d_inner), lambda i, j: (i, j, 0)),
                   pl.BlockSpec((1, d_inner, SSM_STATE), lambda i, j: (i, 0, 0))],
        out_shape=[jax.ShapeDtypeStruct((b, l, d_inner), BF16),
                   jax.ShapeDtypeStruct((b, d_inner, SSM_STATE), F32)],
        scratch_shapes=[pltpu.VMEM((C + 8, conv_dim), F32), pltpu.VMEM((C, conv_dim), F32)],
        compiler_params=_cparams(2),
        name="ssd_prompt",
    )(u, u, u, conv_w, conv_b.reshape(1, -1), pad(dt_bias), pad(a_log),
      jnp.repeat(d_skip, HEAD).reshape(1, -1), ssm_norm.reshape(1, -1), e_bf, et_bf)


def _head_masks(shape):
    lane = lax.broadcasted_iota(jnp.int32, shape, 1)
    return lane < HEAD, lane >= HEAD


def _rwkv_features(proj, prev, mu_ref, w0_ref, w2_ref, a0_ref, a2_ref, g2_ref, kk_ref, ka_ref, dim):
    xs = proj + (prev - proj) * mu_ref[...]
    k = xs[:, dim:2 * dim]
    t_wa = xs[:, 3 * dim:3 * dim + LANES]
    xg = xs[:, 3 * dim + LANES:3 * dim + 2 * LANES]
    w = -_softplus(-(w0_ref[...] + _dot(jnp.tanh(t_wa).astype(BF16), w2_ref[...]))) - 0.5
    a = _sigmoid(a0_ref[...] + _dot(t_wa.astype(BF16), a2_ref[...]))
    g = _dot(_sigmoid(xg).astype(BF16), g2_ref[...])
    return (xs[:, 0:dim], k * (1.0 + (a - 1.0) * ka_ref[...]), xs[:, 2 * dim:3 * dim], k * kk_ref[...], a,
            -jnp.exp(w), g)


def _rwkv_chunk_kernel(p_ref, mu_ref, w0_ref, w2_ref, a0_ref, a2_ref, g2_ref, kk_ref, ka_ref, rk_ref,
                       lng_ref, lnb_ref, ones2_ref, o_ref, s_ref,
                       ext_ref, r_s, k_s, v_s, kkn_s, a_s, lw_s, g_s, x_s, n_s, q_s, yb_s, bkh_s, pc_s, *, dim):
    c = pl.program_id(1)
    C = p_ref.shape[1]
    n_pairs = dim // LANES

    @pl.when(c == 0)
    def _():
        ext_ref[0:8, :] = jnp.zeros((8, ext_ref.shape[1]), F32)
        s_ref[...] = jnp.zeros(s_ref.shape, F32)

    @pl.when(c > 0)
    def _():
        ext_ref[0:8, :] = ext_ref[C:C + 8, :]

    proj = p_ref[0]
    ext_ref[8:C + 8, :] = proj
    (r_s[...], k_s[...], v_s[...], kkn_s[...], a_s[...], lw_s[...], g_s[...]) = _rwkv_features(
        proj, ext_ref[7:C + 7, :], mu_ref, w0_ref, w2_ref, a0_ref, a2_ref, g2_ref, kk_ref, ka_ref, dim)

    tril = _tril(C)
    tril_strict = _tril(C, -1)
    tril_bf = tril.astype(BF16)
    ones2 = ones2_ref[...]
    m0, m1 = _head_masks((C, LANES))
    row = lax.broadcasted_iota(jnp.int32, (LANES, LANES), 0)
    col = lax.broadcasted_iota(jnp.int32, (LANES, LANES), 1)
    blockdiag = (row < HEAD) == (col < HEAD)
    n_levels = C.bit_length() - 1
    assert 1 << n_levels == C

    def by_head(x):
        return jnp.concatenate([jnp.where(m0, x, 0.0), jnp.where(m1, x, 0.0)], axis=0)

    def head_sum(x):
        s0 = jnp.sum(jnp.where(m0, x, 0.0), axis=1, keepdims=True)
        s1 = jnp.sum(jnp.where(m1, x, 0.0), axis=1, keepdims=True)
        return jnp.where(m0, s0, s1)

    for j in range(n_pairs):
        cols = slice(j * LANES, (j + 1) * LANES)
        lw = lw_s[:, cols]
        logp = _dot_exact_lhs(tril_bf, lw)
        logpc = logp[C - 1:C, :]
        ep, epinv = jnp.exp(logp), jnp.exp(-logp)
        epc = jnp.exp(logpc - logp)
        kkf = kkn_s[:, cols]
        kk = kkf * lax.rsqrt(head_sum(kkf * kkf) + 1e-12)
        aj, kj, vj, rj = a_s[:, cols], k_s[:, cols], v_s[:, cols], r_s[:, cols]
        bvec = kk * aj
        at = -kk * jnp.exp(logp - lw)
        rt = rj * ep
        bk_bf = jnp.concatenate([bvec * epinv, kj * epinv], axis=0).astype(BF16)
        ar = jnp.concatenate([at, rt], axis=0)
        as_ = _dot_nt(ar.astype(BF16), s_ref[0, j].astype(BF16))
        mab, mak, qab, qak = [], [], [], []
        for hm in (m0, m1):
            g = _dot_nt(jnp.where(jnp.concatenate([hm, hm], axis=0), ar, 0.0).astype(BF16), bk_bf)
            mab.append(jnp.where(tril_strict, g[0:C, 0:C], 0.0))
            mak.append(jnp.where(tril_strict, g[0:C, C:2 * C], 0.0))
            qab.append(jnp.where(tril, g[C:2 * C, 0:C], 0.0))
            qak.append(jnp.where(tril, g[C:2 * C, C:2 * C], 0.0))
        x_s[j] = as_[0:C] + _dot(jnp.concatenate(mak, axis=1).astype(BF16), by_head(vj).astype(BF16))
        n_s[j] = jnp.concatenate(mab, axis=1).astype(BF16)
        q_s[j] = jnp.concatenate(qab + qak, axis=1).astype(BF16)
        yb_s[j] = as_[C:2 * C]
        bkh_s[j] = jnp.concatenate([bvec * epc, kj * epc], axis=0).astype(BF16)
        pc_s[j] = jnp.broadcast_to(jnp.exp(logpc), (8, LANES))

    zero = jnp.zeros((C, C), BF16)
    for lvl in range(n_levels):
        for j in range(n_pairs):
            n = n_s[j]
            x_s[j] += _dot(n, by_head(x_s[j]).astype(BF16))
            if lvl + 1 < n_levels:
                nd = jnp.concatenate([jnp.concatenate([n[:, 0:C], zero], axis=1),
                                      jnp.concatenate([zero, n[:, C:2 * C]], axis=1)], axis=0)
                n_s[j] = _dot(n, nd).astype(BF16)

    for j in range(n_pairs):
        cols = slice(j * LANES, (j + 1) * LANES)
        kj, vj, rj = k_s[:, cols], v_s[:, cols], r_s[:, cols]
        x = x_s[j]
        y = yb_s[j] + _dot(q_s[j], jnp.concatenate([by_head(x), by_head(vj)], axis=0).astype(BF16))
        upd = _dot_tn(jnp.concatenate([x, vj], axis=0).astype(BF16), bkh_s[j])
        s_ref[0, j] = pc_s[j, 0:1, :] * s_ref[0, j] + jnp.where(blockdiag, upd, 0.0)

        mean = head_sum(y) * (1.0 / HEAD)
        d = y - mean
        var = head_sum(d * d) * (1.0 / HEAD)
        yn = d * lax.rsqrt(var + LN_X_EPS) * lng_ref[:, cols] + lnb_ref[:, cols]
        bonus = head_sum(rj * kj * rk_ref[:, cols]) * vj
        o_ref[0, :, cols] = ((yn + bonus) * g_s[:, cols]).astype(o_ref.dtype)


def rwkv_prompt(proj, shift_mu, decay_w0, decay_w2, aaa_a0, aaa_a2, gate_g2, k_k, k_a, r_k, lnx_g, lnx_b):
    b, l, sd = proj.shape
    dim = decay_w0.size
    n_heads = dim // HEAD
    dl, al, gl = decay_w2.shape[0], aaa_a2.shape[0], gate_g2.shape[0]
    assert dl + al == LANES and gl == LANES and sd == 3 * dim + 2 * LANES and l % CHUNK == 0
    C = CHUNK
    n_pairs = dim // LANES
    w2 = jnp.concatenate([decay_w2, jnp.zeros((al, dim), F32)], axis=0).astype(BF16)
    a2 = jnp.concatenate([jnp.zeros((dl, dim), F32), aaa_a2], axis=0).astype(BF16)
    ones2 = jnp.asarray(np.arange(LANES)[:, None] // HEAD == np.arange(LANES)[None, :] // HEAD, BF16)
    row = lambda v: v.reshape(1, -1)
    full = lambda shape: pl.BlockSpec(shape, lambda i, j: (0,) * len(shape))
    vec = full((1, dim))
    y, s = pl.pallas_call(
        functools.partial(_rwkv_chunk_kernel, dim=dim),
        grid=(b, l // C),
        in_specs=[pl.BlockSpec((1, C, sd), lambda i, j: (i, j, 0)), full((1, sd)), vec, full((LANES, dim)), vec,
                  full((LANES, dim)), full((LANES, dim)), vec, vec, vec, vec, vec, full((LANES, LANES))],
        out_specs=[pl.BlockSpec((1, C, dim), lambda i, j: (i, j, 0)),
                   pl.BlockSpec((1, dim // LANES, LANES, LANES), lambda i, j: (i, 0, 0, 0))],
        out_shape=[jax.ShapeDtypeStruct((b, l, dim), BF16),
                   jax.ShapeDtypeStruct((b, dim // LANES, LANES, LANES), F32)],
        scratch_shapes=[pltpu.VMEM((C + 8, sd), F32)] + [pltpu.VMEM((C, dim), F32)] * 7
                       + [pltpu.VMEM((n_pairs, C, LANES), F32), pltpu.VMEM((n_pairs, C, 2 * C), BF16),
                          pltpu.VMEM((n_pairs, C, 4 * C), BF16), pltpu.VMEM((n_pairs, C, LANES), F32),
                          pltpu.VMEM((n_pairs, 2 * C, LANES), BF16), pltpu.VMEM((n_pairs, 8, LANES), F32)],
        compiler_params=_cparams(2),
        name="rwkv_prompt",
    )(proj, row(shift_mu), row(decay_w0), w2, row(aaa_a0), a2, gate_g2.astype(BF16), row(k_k), row(k_a),
      row(r_k), row(lnx_g), row(lnx_b), ones2)
    s = s.reshape(b, dim // LANES, 2, HEAD, 2, HEAD)
    s = jnp.stack([s[:, :, 0, :, 0, :], s[:, :, 1, :, 1, :]], axis=2).reshape(b, n_heads, HEAD, HEAD)
    return y, s


def _mix_kernel(x_ref, ym_ref, yr_ref, gate_ref, wm_ref, wr_ref, wo_ref, nf_ref, wq_ref, x2_ref, q_ref):
    d = x_ref.shape[1]
    y_m = _dot(ym_ref[...].astype(BF16), wm_ref[...])
    y_r = _dot(yr_ref[...].astype(BF16), wr_ref[...])
    mix = _sigmoid(gate_ref[:, 0:d]) * y_m + _sigmoid(gate_ref[:, d:2 * d]) * y_r
    x2 = x_ref[...] + _dot(mix.astype(BF16), wo_ref[...])
    x2_ref[...] = x2
    q_ref[...] = _dot(_rms(x2, nf_ref[...]).astype(BF16), wq_ref[...]).astype(q_ref.dtype)


def mix_and_query(x, ym, yr, gates, w_out_ssm, w_out_rwkv, w_out, norm_ffn, peer_wq, tm):
    m, d = x.shape
    tm = min(tm, m)
    assert m % tm == 0
    rows = lambda n: pl.BlockSpec((tm, n), lambda i: (i, 0))
    full = lambda a: pl.BlockSpec(a.shape, lambda i: (0, 0))
    nq = peer_wq.shape[1]
    args = (x, ym, yr, gates, w_out_ssm, w_out_rwkv, w_out, norm_ffn.reshape(1, -1), peer_wq)
    return pl.pallas_call(
        _mix_kernel,
        grid=(m // tm,),
        in_specs=[rows(d), rows(ym.shape[1]), rows(yr.shape[1]), rows(2 * d)] + [full(a) for a in args[4:]],
        out_specs=[rows(d), rows(nq)],
        out_shape=[jax.ShapeDtypeStruct((m, d), F32), jax.ShapeDtypeStruct((m, nq), BF16)],
        compiler_params=_cparams(1),
        name="mix_and_query",
    )(*args)


PEER_TOPK = 16
N_KEYS = 128
PEER_CHUNK = 8 * N_KEYS


def _erf_gelu(x):
    return 0.5 * x * (1.0 + lax.erf(x * np.float32(1.0 / np.sqrt(2.0))))


NOT_TOP = 64.0


def _kth_largest(s, k, want_rank=False):
    tops = []
    rank = jnp.full(s.shape, NOT_TOP, F32)
    for r in range(k):
        m = jnp.max(s, axis=0, keepdims=True)
        tops.append(m)
        hit = s == m
        if want_rank:
            rank = jnp.where(hit, float(r), rank)
        s = jnp.where(hit, -jnp.inf, s)
    return (tops, rank) if want_rank else tops


def _peer_kernel(x2_ref, q_ref, p_ref, k1_ref, k2_ref, *rest, n_heads, n_e, n_split):
    u_refs, vt_refs = rest[:n_split], rest[n_split:2 * n_split]
    (nf_ref, npl_ref, wg_ref, wp_ref, nfin_ref, y_ref, hb_s, n1_s, c_s, rk2_s, d_s, act0_s, act1_s, w0_s, w1_s,
     acc_s) = rest[2 * n_split:]
    j = pl.program_id(1)
    tt = x2_ref.shape[0]
    ec = n_split * u_refs[0].shape[0]
    qd = N_KEYS
    assert ec == PEER_CHUNK

    @pl.when(j == 0)
    def _():
        hb_s[...] = _rms(x2_ref[...], nf_ref[...]).T.astype(BF16)
        acc_s[...] = jnp.zeros(acc_s.shape, F32)
        sub8 = lax.broadcasted_iota(jnp.int32, (8, LANES), 0)
        n_ts = tt // LANES

        def select_experts(unit, carry):
            h = unit // n_ts
            tok = pl.ds(pl.multiple_of((unit % n_ts) * LANES, LANES), LANES)
            q1 = q_ref[tok, pl.ds(pl.multiple_of(2 * h * qd, qd), qd)]
            q2 = q_ref[tok, pl.ds(pl.multiple_of((2 * h + 1) * qd, qd), qd)]
            s1 = _dot_nt(k1_ref[h], q1)
            s2 = _dot_nt(k2_ref[h], q2)
            v1 = _kth_largest(s1, PEER_TOPK)
            v2, rank2 = _kth_largest(s2, PEER_TOPK, want_rank=True)
            v2lo = jnp.concatenate(v2[0:8], axis=0)
            pieces = [v1[0] + v2lo, v1[0] + jnp.concatenate(v2[8:16], axis=0), v1[1] + v2lo]
            for k1 in range(2, 8):
                pieces.append(jnp.where(sub8 < PEER_TOPK // (k1 + 1), v1[k1] + v2lo, -jnp.inf))
            pieces.append(jnp.concatenate(v1[8:16], axis=0) + v2[0])
            cand = jnp.concatenate(pieces, axis=0)
            top = _kth_largest(cand, PEER_TOPK)
            z = sum(jnp.exp(t - top[0]) for t in top)
            th = top[PEER_TOPK - 1]
            n1_s[h, :, tok] = sum(jnp.where(s1 + v >= th, 1.0, 0.0) for v in v2)
            c_s[h, :, tok] = jnp.exp(s1 - v1[0]) / z
            rk2_s[h, :, tok] = rank2.astype(BF16)
            d_s[h, :, tok] = jnp.exp(s2 - v2[0]).astype(BF16)
            return carry

        lax.fori_loop(0, n_heads * n_ts, select_experts, 0, unroll=2)

    act = (act0_s, act1_s)
    wts = (w0_s, w1_s)

    def step(par, do_a, do_b, do_c):
        i_rows = pl.ds(pl.multiple_of(jnp.clip(j - 1, 0, n_e - 1) * 8, 8), 8)
        jh = N_KEYS // 2
        n_blocks = (tt // LANES) * (N_KEYS // jh)
        d = n_split * vt_refs[0].shape[1]
        for ts in range(tt // LANES):
            tok = slice(ts * LANES, (ts + 1) * LANES)
            for jb in range(N_KEYS // jh):
                blk = ts * (N_KEYS // jh) + jb
                per = n_blocks // n_split
                if blk % per == 0:
                    sp = blk // per
                    ra = slice(sp * (ec // n_split), (sp + 1) * (ec // n_split))
                    rc = slice(sp * (d // n_split), (sp + 1) * (d // n_split))
                    if do_a:
                        act[par][ra, :] = _dot(u_refs[sp][...], hb_s[...])
                    if do_c:
                        acc_s[rc, :] += _dot(vt_refs[sp][0], wts[par][...])
                if not do_b:
                    continue
                jrows = slice(jb * jh, (jb + 1) * jh)
                acc = [jnp.zeros((jh, LANES), BF16)] * 8
                for h in range(n_heads):
                    rk, dd = rk2_s[h, jrows, tok], d_s[h, jrows, tok]
                    n1b, cb = n1_s[h, i_rows, tok], c_s[h, i_rows, tok]
                    for il in range(8):
                        sel = rk < n1b[il:il + 1].astype(BF16)
                        acc[il] = acc[il] + jnp.where(sel, dd, 0.0) * cb[il:il + 1].astype(BF16)
                for il in range(8):
                    rows = slice(il * N_KEYS + jb * jh, il * N_KEYS + (jb + 1) * jh)
                    wts[1 - par][rows, tok] = acc[il] * _erf_gelu(act[1 - par][rows, tok]).astype(BF16)

    assert n_e % 2 == 0 and n_e >= 4
    pl.when(j == 0)(lambda: step(0, True, False, False))
    pl.when(j == 1)(lambda: step(1, True, True, False))
    pl.when((j >= 2) & (j < n_e) & (j % 2 == 0))(lambda: step(0, True, True, True))
    pl.when((j >= 2) & (j < n_e) & (j % 2 == 1))(lambda: step(1, True, True, True))
    pl.when(j == n_e)(lambda: step(0, False, True, True))
    pl.when(j == n_e + 1)(lambda: step(1, False, False, True))

    @pl.when(j == n_e + 1)
    def _():
        x3 = x2_ref[...] + acc_s[...].T
        gate = _sigmoid(_dot(_rms(x3, npl_ref[...]).astype(BF16), wg_ref[...]))
        x4 = x3 + gate * _dot(p_ref[...].astype(BF16), wp_ref[...])
        y_ref[...] = _rms(x4, nfin_ref[...])


def peer_ple_final(x2, q, p, peer_k1, peer_k2, peer_u, peer_vt, norm_ffn, norm_ple, w_ple_gate, w_ple_proj,
                   norm_final, tt, ec):
    t, d = x2.shape
    n_heads = peer_k1.shape[0]
    n_exp = peer_u.shape[0]
    tt = min(tt, t)
    assert t % tt == 0 and n_exp % ec == 0 and tt % LANES == 0 and ec % N_KEYS == 0
    assert peer_k1.shape[1:] == (N_KEYS, N_KEYS) and n_exp == N_KEYS * N_KEYS
    rows = lambda n: pl.BlockSpec((tt, n), lambda i, j: (i, 0))
    full = lambda a: pl.BlockSpec(a.shape, lambda i, j: (0,) * a.ndim)
    row = lambda v: v.reshape(1, -1)
    consts = (row(norm_ffn), row(norm_ple), w_ple_gate, w_ple_proj, row(norm_final))
    n_e = n_exp // ec
    assert peer_vt.shape == (n_e, d, ec)
    hs = lambda dt: pltpu.VMEM((n_heads, N_KEYS, tt), dt)
    n_split = 2
    u_map = lambda s, i, j: (jnp.minimum(j, n_e - 1) * n_split + s, 0)
    vt_map = lambda s, i, j: (jnp.clip(j - 2, 0, n_e - 1), s, 0)
    return pl.pallas_call(
        functools.partial(_peer_kernel, n_heads=n_heads, n_e=n_e, n_split=n_split),
        grid=(t // tt, n_e + 2),
        in_specs=[rows(d), rows(q.shape[1]), rows(p.shape[1]), full(peer_k1), full(peer_k2),
                  *[pl.BlockSpec((ec // n_split, d), functools.partial(u_map, s)) for s in range(n_split)],
                  *[pl.BlockSpec((1, d // n_split, ec), functools.partial(vt_map, s)) for s in range(n_split)]]
                 + [full(a) for a in consts],
        out_specs=rows(d),
        out_shape=jax.ShapeDtypeStruct((t, d), F32),
        scratch_shapes=[pltpu.VMEM((d, tt), BF16), hs(F32), hs(F32), hs(BF16), hs(BF16),
                        pltpu.VMEM((ec, tt), F32), pltpu.VMEM((ec, tt), F32),
                        pltpu.VMEM((ec, tt), BF16), pltpu.VMEM((ec, tt), BF16), pltpu.VMEM((d, tt), F32)],
        compiler_params=_cparams(2),
        name="peer_ple_final",
    )(x2, q, p, peer_k1, peer_k2, *([peer_u] * n_split), *([peer_vt] * n_split), *consts)


def _as_column(x_row):
    n = x_row.shape[1]
    eye = lax.broadcasted_iota(jnp.int32, (n, n), 0) == lax.broadcasted_iota(jnp.int32, (n, n), 1)
    return jnp.sum(jnp.where(eye, jnp.broadcast_to(x_row, (n, n)), 0.0), axis=1, keepdims=True)


def _rows8(x_row):
    return jnp.broadcast_to(x_row, (8, x_row.shape[1]))


def _ssd_step_kernel(xbc_ref, z_ref, dt_ref, conv_ref, h_ref, convw_ref, convb_ref, dtb_ref, alog_ref, dskip_ref,
                     normg_ref, e_ref, y_ref, ho_ref, *, n_heads):
    d_inner = n_heads * HEAD
    gw = d_inner // SSM_GROUPS
    cs = conv_ref[0]
    conv = convb_ref[...] + xbc_ref[0] * convw_ref[CONV_WIDTH - 1:CONV_WIDTH, :]
    for k in range(CONV_WIDTH - 1):
        conv = conv + cs[k:k + 1, :] * convw_ref[k:k + 1, :]
    act = conv * _sigmoid(conv)
    dt = _softplus(dt_ref[0] + dtb_ref[...])
    da = jnp.exp(dt * -jnp.exp(alog_ref[...]))
    e = e_ref[...]
    dt_x = _dot_exact_rhs(_rows8(dt), e)[0:1]
    da_x = _dot_exact_rhs(_rows8(da), e)[0:1]
    for g in range(SSM_GROUPS):
        cols = slice(g * gw, (g + 1) * gw)
        xg = act[:, cols]
        bg = act[:, d_inner + g * SSM_STATE:d_inner + (g + 1) * SSM_STATE]
        cg = act[:, d_inner + (SSM_GROUPS + g) * SSM_STATE:d_inner + (SSM_GROUPS + g + 1) * SSM_STATE]
        hn = _as_column(da_x[:, cols]) * h_ref[0, cols, :] + _as_column(xg * dt_x[:, cols]) * bg
        ho_ref[0, cols, :] = hn
        y = _dot_nt(_rows8(cg).astype(BF16), hn.astype(BF16))[0:1] + xg * dskip_ref[:, cols]
        zg = z_ref[0, :, cols]
        y = y * (zg * _sigmoid(zg))
        y_ref[0, :, cols] = _rms(y, normg_ref[:, cols])


def ssd_step(u, state_conv, state_ssm, conv_w, conv_b, dt_bias, a_log, d_skip, ssm_norm, n_heads):
    b = u.shape[0]
    d_inner = n_heads * HEAD
    conv_dim = d_inner + 2 * SSM_GROUPS * SSM_STATE
    pad = lambda v: jnp.pad(v.reshape(1, -1), ((0, 0), (0, LANES - v.size)))
    e_bf = jnp.asarray(np.arange(d_inner)[None, :] // HEAD == np.arange(LANES)[:, None], BF16)
    full = lambda shape: pl.BlockSpec(shape, lambda i: (0,) * len(shape))
    u3 = u.reshape(b, 1, -1)
    y, h = pl.pallas_call(
        functools.partial(_ssd_step_kernel, n_heads=n_heads),
        grid=(b,),
        in_specs=[pl.BlockSpec((1, 1, conv_dim), lambda i: (i, 0, 0)),
                  pl.BlockSpec((1, 1, d_inner), lambda i: (i, 0, conv_dim // d_inner)),
                  pl.BlockSpec((1, 1, LANES), lambda i: (i, 0, (conv_dim + d_inner) // LANES)),
                  pl.BlockSpec((1, CONV_WIDTH - 1, conv_dim), lambda i: (i, 0, 0)),
                  pl.BlockSpec((1, d_inner, SSM_STATE), lambda i: (i, 0, 0)),
                  full((CONV_WIDTH, conv_dim)), full((1, conv_dim)), full((1, LANES)), full((1, LANES)),
                  full((1, d_inner)), full((1, d_inner)), full((LANES, d_inner))],
        out_specs=[pl.BlockSpec((1, 1, d_inner), lambda i: (i, 0, 0)),
                   pl.BlockSpec((1, d_inner, SSM_STATE), lambda i: (i, 0, 0))],
        out_shape=[jax.ShapeDtypeStruct((b, 1, d_inner), F32),
                   jax.ShapeDtypeStruct((b, d_inner, SSM_STATE), F32)],
        compiler_params=_cparams(1),
        name="ssd_step",
    )(u3, u3, u3, state_conv, state_ssm.reshape(b, d_inner, SSM_STATE), conv_w, conv_b.reshape(1, -1),
      pad(dt_bias), pad(a_log), jnp.repeat(d_skip, HEAD).reshape(1, -1), ssm_norm.reshape(1, -1), e_bf)
    return y.reshape(b, d_inner), h


def _rwkv_step_features_kernel(p_ref, prev_ref, mu_ref, w0_ref, w2_ref, a0_ref, a2_ref, g2_ref, kk_ref, ka_ref,
                               ones2_ref, o_ref, *, dim):
    r, k, v, kkf, a, lw, g = _rwkv_features(p_ref[...], prev_ref[...], mu_ref, w0_ref, w2_ref, a0_ref, a2_ref,
                                            g2_ref, kk_ref, ka_ref, dim)
    for j in range(dim // LANES):
        cols = slice(j * LANES, (j + 1) * LANES)
        kj = kkf[:, cols]
        o_ref[3, :, cols] = kj * lax.rsqrt(_dot_exact_rhs(kj * kj, ones2_ref[...]) + 1e-12)
    o_ref[0], o_ref[1], o_ref[2], o_ref[4], o_ref[5], o_ref[6] = r, k, v, a, jnp.exp(lw), g


def _rwkv_step_kernel(f_ref, s_ref, rk_ref, lng_ref, lnb_ref, y_ref, so_ref, y_s):
    n_heads = s_ref.shape[1]
    eye = lax.broadcasted_iota(jnp.int32, (HEAD, HEAD), 0) == lax.broadcasted_iota(jnp.int32, (HEAD, HEAD), 1)
    for h in range(n_heads):
        row = lambda i: f_ref[i, 0, h:h + 1, :]
        r, k, v, kk, a, w = (row(i) for i in range(6))
        s = s_ref[0, h]
        sa = jnp.sum(s * -kk, axis=1, keepdims=True)
        v_col = jnp.sum(jnp.where(eye, jnp.broadcast_to(v, (HEAD, HEAD)), 0.0), axis=1, keepdims=True)
        sn = s * w + sa * (kk * a) + v_col * k
        so_ref[0, h] = sn
        y_s[h:h + 1, :] = _dot_nt(_rows8(r).astype(BF16), sn.astype(BF16))[0:1]
    y = y_s[...]
    r, k, v, g = f_ref[0, 0], f_ref[1, 0], f_ref[2, 0], f_ref[6, 0]
    d = y - jnp.mean(y, axis=-1, keepdims=True)
    var = jnp.mean(d * d, axis=-1, keepdims=True)
    yn = d * lax.rsqrt(var + LN_X_EPS) * lng_ref[...] + lnb_ref[...]
    bonus = jnp.sum(r * k * rk_ref[...], axis=-1, keepdims=True) * v
    y_ref[0] = (yn + bonus) * g


def rwkv_step(proj, shift_prev, state_wkv, shift_mu, decay_w0, decay_w2, aaa_a0, aaa_a2, gate_g2, k_k, k_a, r_k,
              lnx_g, lnx_b):
    b, sd = proj.shape
    dim = decay_w0.size
    n_heads = dim // HEAD
    dl, al, gl = decay_w2.shape[0], aaa_a2.shape[0], gate_g2.shape[0]
    assert dl + al == LANES and gl == LANES and sd == 3 * dim + 2 * LANES
    w2 = jnp.concatenate([decay_w2, jnp.zeros((al, dim), F32)], axis=0).astype(BF16)
    a2 = jnp.concatenate([jnp.zeros((dl, dim), F32), aaa_a2], axis=0).astype(BF16)
    ones2 = jnp.asarray(np.arange(LANES)[:, None] // HEAD == np.arange(LANES)[None, :] // HEAD, BF16)
    row = lambda v: v.reshape(1, -1)
    args = (proj, shift_prev, row(shift_mu), row(decay_w0), w2, row(aaa_a0), a2, gate_g2.astype(BF16), row(k_k),
            row(k_a), ones2)
    feats = pl.pallas_call(
        functools.partial(_rwkv_step_features_kernel, dim=dim),
        grid=(1,),
        in_specs=[pl.BlockSpec(a.shape, lambda i: (0, 0)) for a in args],
        out_specs=pl.BlockSpec((7, b, dim), lambda i: (0, 0, 0)),
        out_shape=jax.ShapeDtypeStruct((7, b, dim), F32),
        compiler_params=_cparams(1),
        name="rwkv_step_features",
    )(*args)
    hv = lambda v: v.reshape(n_heads, HEAD)
    full = pl.BlockSpec((n_heads, HEAD), lambda i: (0, 0))
    y, s = pl.pallas_call(
        _rwkv_step_kernel,
        grid=(b,),
        in_specs=[pl.BlockSpec((7, 1, n_heads, HEAD), lambda i: (0, i, 0, 0)),
                  pl.BlockSpec((1, n_heads, HEAD, HEAD), lambda i: (i, 0, 0, 0)), full, full, full],
        out_specs=[pl.BlockSpec((1, n_heads, HEAD), lambda i: (i, 0, 0)),
                   pl.BlockSpec((1, n_heads, HEAD, HEAD), lambda i: (i, 0, 0, 0))],
        out_shape=[jax.ShapeDtypeStruct((b, n_heads, HEAD), F32),
                   jax.ShapeDtypeStruct((b, n_heads, HEAD, HEAD), F32)],
        scratch_shapes=[pltpu.VMEM((n_heads, HEAD), F32)],
        compiler_params=_cparams(1),
        name="rwkv_step",
    )(feats.reshape(7, b, n_heads, HEAD), state_wkv, hv(r_k), hv(lnx_g), hv(lnx_b))
    return y.reshape(b, dim), s


def _layer(x, p, states, wts, n_ssm_heads):
    b, l, d = x.shape
    xt = x.reshape(b * l, d)
    g_mix = wts['norm_mix'].reshape(1, -1)
    u_ssm, u_rwkv, u_gate = (norm_matmul(xt, g_mix, wts[k], PROJ_ROWS, _proj_col_tile(wts[k].shape[1]))
                             for k in ('w_ssm', 'w_shift', 'w_gates'))
    d_inner = n_ssm_heads * HEAD
    conv_dim = d_inner + 2 * SSM_GROUPS * SSM_STATE
    ssd_w = (wts['conv_w'], wts['conv_b'], wts['dt_bias'], wts['a_log'], wts['d_skip'], wts['ssm_norm'])
    rwkv_w = tuple(wts[k] for k in ('shift_mu', 'decay_w0', 'decay_w2', 'aaa_a0', 'aaa_a2', 'gate_g2', 'k_k', 'k_a',
                                    'r_k', 'lnx_g', 'lnx_b'))
    if states is None:
        ym, ssm_new = ssd_prompt(u_ssm.reshape(b, l, -1), *ssd_w, n_ssm_heads)
        ym = ym.reshape(b * l, d_inner)
        conv_new = u_ssm.reshape(b, l, -1)[:, l - (CONV_WIDTH - 1):, :conv_dim]
        yr, wkv_new = rwkv_prompt(u_rwkv.reshape(b, l, -1), *rwkv_w)
        yr = yr.reshape(b * l, -1)
        shift_new = u_rwkv.reshape(b, l, -1)[:, l - 1]
    else:
        conv_prev, ssm_prev, wkv_prev, shift_prev = states
        ym, ssm_new = ssd_step(u_ssm, conv_prev, ssm_prev, *ssd_w, n_ssm_heads)
        conv_new = jnp.concatenate([conv_prev[:, 1:], u_ssm[:, None, :conv_dim]], axis=1)
        yr, wkv_new = rwkv_step(u_rwkv, shift_prev, wkv_prev, *rwkv_w)
        shift_new = u_rwkv
    x2, q = mix_and_query(xt, ym, yr, u_gate, wts['w_out_ssm'], wts['w_out_rwkv'], wts['w_out'], wts['norm_ffn'],
                          wts['peer_wq'], MIX_ROWS)
    y = peer_ple_final(x2, q, p.reshape(b * l, -1), wts['peer_k1'], wts['peer_k2'], wts['peer_u'], wts['peer_vt'],
                       wts['norm_ffn'], wts['norm_ple'], wts['w_ple_gate'], wts['w_ple_proj'], wts['norm_final'],
                       PEER_ROWS, PEER_CHUNK)
    return (y.reshape(b, l, d), ssm_new.reshape(b, n_ssm_heads, HEAD, SSM_STATE), conv_new, wkv_new, shift_new)


def kernel(x_prompt, x_sample, p_prompt, p_sample, state_ssm, state_conv, state_wkv, state_shift, norm_mix, w_in,
           conv_w, conv_b, dt_bias, a_log, d_skip, ssm_norm, w_out_ssm, shift_mu, decay_w0, decay_w2, aaa_a0, aaa_a2,
           gate_g2, k_k, k_a, r_k, lnx_g, lnx_b, w_out_rwkv, w_out, norm_ffn, peer_wq, peer_k1, peer_k2, peer_u,
           peer_v, norm_ple, w_ple_gate, w_ple_proj, norm_final):
    depth = w_in.shape[0]
    assert depth == 1, "single-layer trunk"
    d_model = x_prompt.shape[-1]
    n_ssm_heads = dt_bias.shape[1]
    d_inner = n_ssm_heads * HEAD
    conv_dim = conv_w.shape[2]
    shift_dim = shift_mu.shape[1]
    bf = lambda a: a.astype(BF16)
    o = np.cumsum([0, d_inner, conv_dim, n_ssm_heads, shift_dim, d_model, d_model])
    wi = w_in[0]
    wts = {
        'w_ssm': bf(jnp.concatenate([wi[:, o[1]:o[2]], wi[:, o[0]:o[1]], wi[:, o[2]:o[3]],
                                     jnp.zeros((d_model, LANES - n_ssm_heads), F32)], axis=1)),
        'w_shift': bf(wi[:, o[3]:o[4]]),
        'w_gates': bf(wi[:, o[4]:o[6]]),
        'w_out_ssm': bf(w_out_ssm[0]), 'w_out_rwkv': bf(w_out_rwkv[0]), 'w_out': bf(w_out[0]),
        'peer_wq': bf(peer_wq[0]), 'peer_k1': bf(peer_k1[0]), 'peer_k2': bf(peer_k2[0]),
        'peer_u': bf(peer_u[0]),
        'peer_vt': bf(peer_v[0]).reshape(-1, PEER_CHUNK, d_model).transpose(0, 2, 1),
        'w_ple_gate': bf(w_ple_gate[0]), 'w_ple_proj': bf(w_ple_proj[0]), 'norm_final': norm_final,
    }
    for name, val in (('norm_mix', norm_mix), ('conv_w', conv_w), ('conv_b', conv_b), ('dt_bias', dt_bias),
                      ('a_log', a_log), ('d_skip', d_skip), ('ssm_norm', ssm_norm), ('shift_mu', shift_mu),
                      ('decay_w0', decay_w0), ('decay_w2', decay_w2), ('aaa_a0', aaa_a0), ('aaa_a2', aaa_a2),
                      ('gate_g2', gate_g2), ('k_k', k_k), ('k_a', k_a), ('r_k', r_k), ('lnx_g', lnx_g),
                      ('lnx_b', lnx_b), ('norm_ffn', norm_ffn), ('norm_ple', norm_ple)):
        wts[name] = val[0]
    yp, ssm_p, conv_p, wkv_p, shift_p = _layer(x_prompt, p_prompt[0], None, wts, n_ssm_heads)
    ys, ssm_s, conv_s, wkv_s, shift_s = _layer(
        x_sample, p_sample[0], (state_conv[0], state_ssm[0], state_wkv[0], state_shift[0]), wts, n_ssm_heads)
    return (yp, ys, ssm_p[None], conv_p[None], wkv_p[None], shift_p[None],
            ssm_s[None], conv_s[None], wkv_s[None], shift_s[None])
```

```python
import functools

import numpy as np
import jax
import jax.numpy as jnp
from jax import lax
from jax.experimental import pallas as pl
from jax.experimental.pallas import tpu as pltpu

F32 = jnp.float32
BF16 = jnp.bfloat16

EPS = 1e-6
LN_X_EPS = 64e-5
HEAD = 64
SSM_STATE = 128
SSM_GROUPS = 8
CONV_WIDTH = 4
CHUNK = 128
LANES = 128
VMEM_LIMIT = 56 * 1024 * 1024

PROJ_ROWS = 1024
PROJ_COLS_MAX = 1792
MIX_ROWS = 512
PEER_ROWS = 512


def _proj_col_tile(n):
    return max(c for c in range(LANES, PROJ_COLS_MAX + 1, LANES) if n % c == 0)


def _cparams(n_axes):
    return pltpu.CompilerParams(dimension_semantics=("arbitrary",) * n_axes,
                                vmem_limit_bytes=VMEM_LIMIT)


def _dot(a, b):
    return jnp.dot(a, b, preferred_element_type=F32)


def _dot_nt(a, b):
    return lax.dot_general(a, b, (((1,), (1,)), ((), ())), preferred_element_type=F32)


def _dot_tn(a, b):
    return lax.dot_general(a, b, (((0,), (0,)), ((), ())), preferred_element_type=F32)


def _split3(x):
    x1 = x.astype(BF16)
    r = x - x1.astype(F32)
    x2 = r.astype(BF16)
    x3 = (r - x2.astype(F32)).astype(BF16)
    return x1, x2, x3


def _dot_exact_rhs(a, e):
    a1, a2, a3 = _split3(a)
    return _dot(a1, e) + (_dot(a2, e) + _dot(a3, e))


def _dot_exact_lhs(e, a):
    a1, a2, a3 = _split3(a)
    return _dot(e, a1) + (_dot(e, a2) + _dot(e, a3))


def _rms(x, g):
    return x * lax.rsqrt(jnp.mean(x * x, axis=-1, keepdims=True) + EPS) * g


def _sigmoid(x):
    return 1.0 / (1.0 + jnp.exp(-x))


def _softplus(x):
    return jnp.maximum(x, 0.0) + jnp.log1p(jnp.exp(-jnp.abs(x)))


def _tril(n, k=0, dtype=F32):
    r = lax.broadcasted_iota(jnp.int32, (n, n), 0)
    c = lax.broadcasted_iota(jnp.int32, (n, n), 1)
    return (c <= r + k)


def _norm_matmul_kernel(x_ref, g_ref, w_ref, o_ref, h_ref):
    @pl.when(pl.program_id(1) == 0)
    def _():
        h_ref[...] = _rms(x_ref[...], g_ref[...]).astype(BF16)

    o_ref[...] = _dot(h_ref[...], w_ref[...])


def norm_matmul(x, g, w, tm, tn):
    m, k = x.shape
    n = w.shape[1]
    tm = min(tm, m)
    assert m % tm == 0 and n % tn == 0, (m, tm, n, tn)
    return pl.pallas_call(
        _norm_matmul_kernel,
        grid=(m // tm, n // tn),
        in_specs=[pl.BlockSpec((tm, k), lambda i, j: (i, 0)),
                  pl.BlockSpec((1, k), lambda i, j: (0, 0)),
                  pl.BlockSpec((k, tn), lambda i, j: (0, j))],
        out_specs=pl.BlockSpec((tm, tn), lambda i, j: (i, j)),
        out_shape=jax.ShapeDtypeStruct((m, n), F32),
        scratch_shapes=[pltpu.VMEM((tm, k), BF16)],
        compiler_params=_cparams(2),
        name="norm_matmul",
    )(x, g, w)


def _ssd_chunk_kernel(xbc_ref, z_ref, dt_ref, convw_ref, convb_ref, dtb_ref, alog_ref, dskip_ref,
                      normg_ref, e_ref, et_ref, y_ref, h_ref, ext_ref, act_ref, *, n_heads):
    c = pl.program_id(1)
    C = xbc_ref.shape[1]
    d_inner = n_heads * HEAD
    gw = d_inner // SSM_GROUPS
    hpg = n_heads // SSM_GROUPS

    @pl.when(c == 0)
    def _():
        ext_ref[0:8, :] = jnp.zeros((8, ext_ref.shape[1]), F32)
        h_ref[...] = jnp.zeros(h_ref.shape, F32)

    @pl.when(c > 0)
    def _():
        ext_ref[0:8, :] = ext_ref[C:C + 8, :]

    ext_ref[8:C + 8, :] = xbc_ref[0]
    conv = convb_ref[...]
    for k in range(CONV_WIDTH):
        off = 8 - (CONV_WIDTH - 1) + k
        conv = conv + ext_ref[off:off + C, :] * convw_ref[k:k + 1, :]
    act_ref[...] = conv * _sigmoid(conv)

    dt = _softplus(dt_ref[0] + dtb_ref[...])
    a = -jnp.exp(alog_ref[...])
    tril = _tril(C)
    acum = _dot_exact_lhs(tril.astype(BF16), dt * a)
    acum_t = acum.T
    dt_t = dt.T
    e = e_ref[...]
    eacum_x = _dot_exact_rhs(jnp.exp(acum), e)
    wdec_x = _dot_exact_rhs(jnp.exp(acum[C - 1:C, :] - acum) * dt, e)
    dec_b = jnp.broadcast_to(jnp.exp(acum_t[:, C - 1:C]), (LANES, SSM_STATE))
    lane = lax.broadcasted_iota(jnp.int32, (C, gw), 1)

    for g in range(SSM_GROUPS):
        cols = slice(g * gw, (g + 1) * gw)
        xg = act_ref[:, cols]
        bg = act_ref[:, d_inner + g * SSM_STATE:d_inner + (g + 1) * SSM_STATE].astype(BF16)
        cg = act_ref[:, d_inner + (SSM_GROUPS + g) * SSM_STATE:
                     d_inner + (SSM_GROUPS + g + 1) * SSM_STATE].astype(BF16)
        cb = _dot_nt(cg, bg)
        y = xg * dskip_ref[:, cols]
        for r in range(hpg):
            h = g * hpg + r
            seg = acum[:, h:h + 1] - acum_t[h:h + 1, :]
            m = cb * jnp.exp(jnp.where(tril, seg, -jnp.inf)) * dt_t[h:h + 1, :]
            xm = jnp.where((lane >= r * HEAD) & (lane < (r + 1) * HEAD), xg, 0.0)
            y = y + _dot(m.astype(BF16), xm.astype(BF16))
        hg = h_ref[0, cols, :]
        y = y + _dot_nt(cg, hg.astype(BF16)) * eacum_x[:, cols]
        zg = z_ref[0, :, cols]
        y = y * (zg * _sigmoid(zg))
        y = _rms(y, normg_ref[:, cols])
        y_ref[0, :, cols] = y.astype(y_ref.dtype)
        dec = _dot_exact_lhs(et_ref[cols, :], dec_b)
        h_ref[0, cols, :] = dec * hg + _dot_tn((xg * wdec_x[:, cols]).astype(BF16), bg)


def ssd_prompt(u, conv_w, conv_b, dt_bias, a_log, d_skip, ssm_norm, n_heads):
    b, l, _ = u.shape
    d_inner = n_heads * HEAD
    conv_dim = d_inner + 2 * SSM_GROUPS * SSM_STATE
    C = CHUNK
    assert l % C == 0 and conv_dim % d_inner == 0
    pad = lambda v: jnp.pad(v.reshape(1, -1), ((0, 0), (0, LANES - v.size)))
    e = (np.arange(d_inner)[None, :] // HEAD == np.arange(LANES)[:, None])
    e_bf = jnp.asarray(e, BF16)
    et_bf = jnp.asarray(e.T, BF16)
    full = lambda shape: pl.BlockSpec(shape, lambda i, j: (0,) * len(shape))
    return pl.pallas_call(
        functools.partial(_ssd_chunk_kernel, n_heads=n_heads),
        grid=(b, l // C),
        in_specs=[pl.BlockSpec((1, C, conv_dim), lambda i, j: (i, j, 0)),
                  pl.BlockSpec((1, C, d_inner), lambda i, j: (i, j, conv_dim // d_inner)),
                  pl.BlockSpec((1, C, LANES), lambda i, j: (i, j, (conv_dim + d_inner) // LANES)),
                  full((CONV_WIDTH, conv_dim)), full((1, conv_dim)), full((1, LANES)), full((1, LANES)),
                  full((1, d_inner)), full((1, d_inner)), full((LANES, d_inner)), full((d_inner, LANES))],
        out_specs=[pl.BlockSpec((1, C, d_inner), lambda i, j: (i, j, 0)),
                   pl.BlockSpec((1, d_inner, SSM_STATE), lambda i, j: (i, 0, 0))],
        out_shape=[jax.ShapeDtypeStruct((b, l, d_inner), BF16),
                   jax.ShapeDtypeStruct((b, d_inner, SSM_STATE), F32)],
        scratch_shapes=[pltpu.VMEM((C + 8, conv_dim), F32), pltpu.VMEM((C, conv_dim), F32)],
        compiler_params=_cparams(2),
        name="ssd_prompt",
    )(u, u, u, conv_w, conv_b.reshape(1, -1), pad(dt_bias), pad(a_log),
      jnp.repeat(d_skip, HEAD).reshape(1, -1), ssm_norm.reshape(1, -1), e_bf, et_bf)


def _head_masks(shape):
    lane = lax.broadcasted_iota(jnp.int32, shape, 1)
    return lane < HEAD, lane >= HEAD


def _rwkv_features(proj, prev, mu_ref, w0_ref, w2_ref, a0_ref, a2_ref, g2_ref, kk_ref, ka_ref, dim):
    xs = proj + (prev - proj) * mu_ref[...]
    k = xs[:, dim:2 * dim]
    t_wa = xs[:, 3 * dim:3 * dim + LANES]
    xg = xs[:, 3 * dim + LANES:3 * dim + 2 * LANES]
    w = -_softplus(-(w0_ref[...] + _dot(jnp.tanh(t_wa).astype(BF16), w2_ref[...]))) - 0.5
    a = _sigmoid(a0_ref[...] + _dot(t_wa.astype(BF16), a2_ref[...]))
    g = _dot(_sigmoid(xg).astype(BF16), g2_ref[...])
    return (xs[:, 0:dim], k * (1.0 + (a - 1.0) * ka_ref[...]), xs[:, 2 * dim:3 * dim], k * kk_ref[...], a,
            -jnp.exp(w), g)


def _rwkv_chunk_kernel(p_ref, mu_ref, w0_ref, w2_ref, a0_ref, a2_ref, g2_ref, kk_ref, ka_ref, rk_ref,
                       lng_ref, lnb_ref, ones2_ref, o_ref, s_ref,
                       ext_ref, r_s, k_s, v_s, kkn_s, a_s, lw_s, g_s, x_s, n_s, q_s, yb_s, bkh_s, pc_s, *, dim):
    c = pl.program_id(1)
    C = p_ref.shape[1]
    n_pairs = dim // LANES

    @pl.when(c == 0)
    def _():
        ext_ref[0:8, :] = jnp.zeros((8, ext_ref.shape[1]), F32)
        s_ref[...] = jnp.zeros(s_ref.shape, F32)

    @pl.when(c > 0)
    def _():
        ext_ref[0:8, :] = ext_ref[C:C + 8, :]

    proj = p_ref[0]
    ext_ref[8:C + 8, :] = proj
    (r_s[...], k_s[...], v_s[...], kkn_s[...], a_s[...], lw_s[...], g_s[...]) = _rwkv_features(
        proj, ext_ref[7:C + 7, :], mu_ref, w0_ref, w2_ref, a0_ref, a2_ref, g2_ref, kk_ref, ka_ref, dim)

    tril = _tril(C)
    tril_strict = _tril(C, -1)
    tril_bf = tril.astype(BF16)
    ones2 = ones2_ref[...]
    m0, m1 = _head_masks((C, LANES))
    row = lax.broadcasted_iota(jnp.int32, (LANES, LANES), 0)
    col = lax.broadcasted_iota(jnp.int32, (LANES, LANES), 1)
    blockdiag = (row < HEAD) == (col < HEAD)
    n_levels = C.bit_length() - 1
    assert 1 << n_levels == C

    def by_head(x):
        return jnp.concatenate([jnp.where(m0, x, 0.0), jnp.where(m1, x, 0.0)], axis=0)

    def head_sum(x):
        s0 = jnp.sum(jnp.where(m0, x, 0.0), axis=1, keepdims=True)
        s1 = jnp.sum(jnp.where(m1, x, 0.0), axis=1, keepdims=True)
        return jnp.where(m0, s0, s1)

    for j in range(n_pairs):
        cols = slice(j * LANES, (j + 1) * LANES)
        lw = lw_s[:, cols]
        logp = _dot_exact_lhs(tril_bf, lw)
        logpc = logp[C - 1:C, :]
        ep, epinv = jnp.exp(logp), jnp.exp(-logp)
        epc = jnp.exp(logpc - logp)
        kkf = kkn_s[:, cols]
        kk = kkf * lax.rsqrt(head_sum(kkf * kkf) + 1e-12)
        aj, kj, vj, rj = a_s[:, cols], k_s[:, cols], v_s[:, cols], r_s[:, cols]
        bvec = kk * aj
        at = -kk * jnp.exp(logp - lw)
        rt = rj * ep
        bk_bf = jnp.concatenate([bvec * epinv, kj * epinv], axis=0).astype(BF16)
        ar = jnp.concatenate([at, rt], axis=0)
        as_ = _dot_nt(ar.astype(BF16), s_ref[0, j].astype(BF16))
        mab, mak, qab, qak = [], [], [], []
        for hm in (m0, m1):
            g = _dot_nt(jnp.where(jnp.concatenate([hm, hm], axis=0), ar, 0.0).astype(BF16), bk_bf)
            mab.append(jnp.where(tril_strict, g[0:C, 0:C], 0.0))
            mak.append(jnp.where(tril_strict, g[0:C, C:2 * C], 0.0))
            qab.append(jnp.where(tril, g[C:2 * C, 0:C], 0.0))
            qak.append(jnp.where(tril, g[C:2 * C, C:2 * C], 0.0))
        x_s[j] = as_[0:C] + _dot(jnp.concatenate(mak, axis=1).astype(BF16), by_head(vj).astype(BF16))
        n_s[j] = jnp.concatenate(mab, axis=1).astype(BF16)
        q_s[j] = jnp.concatenate(qab + qak, axis=1).astype(BF16)
        yb_s[j] = as_[C:2 * C]
        bkh_s[j] = jnp.concatenate([bvec * epc, kj * epc], axis=0).astype(BF16)
        pc_s[j] = jnp.broadcast_to(jnp.exp(logpc), (8, LANES))

    zero = jnp.zeros((C, C), BF16)
    for lvl in range(n_levels):
        for j in range(n_pairs):
            n = n_s[j]
            x_s[j] += _dot(n, by_head(x_s[j]).astype(BF16))
            if lvl + 1 < n_levels:
                nd = jnp.concatenate([jnp.concatenate([n[:, 0:C], zero], axis=1),
                                      jnp.concatenate([zero, n[:, C:2 * C]], axis=1)], axis=0)
                n_s[j] = _dot(n, nd).astype(BF16)

    for j in range(n_pairs):
        cols = slice(j * LANES, (j + 1) * LANES)
        kj, vj, rj = k_s[:, cols], v_s[:, cols], r_s[:, cols]
        x = x_s[j]
        y = yb_s[j] + _dot(q_s[j], jnp.concatenate([by_head(x), by_head(vj)], axis=0).astype(BF16))
        upd = _dot_tn(jnp.concatenate([x, vj], axis=0).astype(BF16), bkh_s[j])
        s_ref[0, j] = pc_s[j, 0:1, :] * s_ref[0, j] + jnp.where(blockdiag, upd, 0.0)

        mean = head_sum(y) * (1.0 / HEAD)
        d = y - mean
        var = head_sum(d * d) * (1.0 / HEAD)
        yn = d * lax.rsqrt(var + LN_X_EPS) * lng_ref[:, cols] + lnb_ref[:, cols]
        bonus = head_sum(rj * kj * rk_ref[:, cols]) * vj
        o_ref[0, :, cols] = ((yn + bonus) * g_s[:, cols]).astype(o_ref.dtype)


def rwkv_prompt(proj, shift_mu, decay_w0, decay_w2, aaa_a0, aaa_a2, gate_g2, k_k, k_a, r_k, lnx_g, lnx_b):
    b, l, sd = proj.shape
    dim = decay_w0.size
    n_heads = dim // HEAD
    dl, al, gl = decay_w2.shape[0], aaa_a2.shape[0], gate_g2.shape[0]
    assert dl + al == LANES and gl == LANES and sd == 3 * dim + 2 * LANES and l % CHUNK == 0
    C = CHUNK
    n_pairs = dim // LANES
    w2 = jnp.concatenate([decay_w2, jnp.zeros((al, dim), F32)], axis=0).astype(BF16)
    a2 = jnp.concatenate([jnp.zeros((dl, dim), F32), aaa_a2], axis=0).astype(BF16)
    ones2 = jnp.asarray(np.arange(LANES)[:, None] // HEAD == np.arange(LANES)[None, :] // HEAD, BF16)
    row = lambda v: v.reshape(1, -1)
    full = lambda shape: pl.BlockSpec(shape, lambda i, j: (0,) * len(shape))
    vec = full((1, dim))
    y, s = pl.pallas_call(
        functools.partial(_rwkv_chunk_kernel, dim=dim),
        grid=(b, l // C),
        in_specs=[pl.BlockSpec((1, C, sd), lambda i, j: (i, j, 0)), full((1, sd)), vec, full((LANES, dim)), vec,
                  full((LANES, dim)), full((LANES, dim)), vec, vec, vec, vec, vec, full((LANES, LANES))],
        out_specs=[pl.BlockSpec((1, C, dim), lambda i, j: (i, j, 0)),
                   pl.BlockSpec((1, dim // LANES, LANES, LANES), lambda i, j: (i, 0, 0, 0))],
        out_shape=[jax.ShapeDtypeStruct((b, l, dim), BF16),
                   jax.ShapeDtypeStruct((b, dim // LANES, LANES, LANES), F32)],
        scratch_shapes=[pltpu.VMEM((C + 8, sd), F32)] + [pltpu.VMEM((C, dim), F32)] * 7
                       + [pltpu.VMEM((n_pairs, C, LANES), F32), pltpu.VMEM((n_pairs, C, 2 * C), BF16),
                          pltpu.VMEM((n_pairs, C, 4 * C), BF16), pltpu.VMEM((n_pairs, C, LANES), F32),
                          pltpu.VMEM((n_pairs, 2 * C, LANES), BF16), pltpu.VMEM((n_pairs, 8, LANES), F32)],
        compiler_params=_cparams(2),
        name="rwkv_prompt",
    )(proj, row(shift_mu), row(decay_w0), w2, row(aaa_a0), a2, gate_g2.astype(BF16), row(k_k), row(k_a),
      row(r_k), row(lnx_g), row(lnx_b), ones2)
    s = s.reshape(b, dim // LANES, 2, HEAD, 2, HEAD)
    s = jnp.stack([s[:, :, 0, :, 0, :], s[:, :, 1, :, 1, :]], axis=2).reshape(b, n_heads, HEAD, HEAD)
    return y, s


def _mix_kernel(x_ref, ym_ref, yr_ref, gate_ref, wm_ref, wr_ref, wo_ref, nf_ref, wq_ref, x2_ref, q_ref):
    d = x_ref.shape[1]
    y_m = _dot(ym_ref[...].astype(BF16), wm_ref[...])
    y_r = _dot(yr_ref[...].astype(BF16), wr_ref[...])
    mix = _sigmoid(gate_ref[:, 0:d]) * y_m + _sigmoid(gate_ref[:, d:2 * d]) * y_r
    x2 = x_ref[...] + _dot(mix.astype(BF16), wo_ref[...])
    x2_ref[...] = x2
    q_ref[...] = _dot(_rms(x2, nf_ref[...]).astype(BF16), wq_ref[...]).astype(q_ref.dtype)


def mix_and_query(x, ym, yr, gates, w_out_ssm, w_out_rwkv, w_out, norm_ffn, peer_wq, tm):
    m, d = x.shape
    tm = min(tm, m)
    assert m % tm == 0
    rows = lambda n: pl.BlockSpec((tm, n), lambda i: (i, 0))
    full = lambda a: pl.BlockSpec(a.shape, lambda i: (0, 0))
    nq = peer_wq.shape[1]
    args = (x, ym, yr, gates, w_out_ssm, w_out_rwkv, w_out, norm_ffn.reshape(1, -1), peer_wq)
    return pl.pallas_call(
        _mix_kernel,
        grid=(m // tm,),
        in_specs=[rows(d), rows(ym.shape[1]), rows(yr.shape[1]), rows(2 * d)] + [full(a) for a in args[4:]],
        out_specs=[rows(d), rows(nq)],
        out_shape=[jax.ShapeDtypeStruct((m, d), F32), jax.ShapeDtypeStruct((m, nq), BF16)],
        compiler_params=_cparams(1),
        name="mix_and_query",
    )(*args)


PEER_TOPK = 16
N_KEYS = 128
PEER_CHUNK = 8 * N_KEYS


def _erf_gelu(x):
    return 0.5 * x * (1.0 + lax.erf(x * np.float32(1.0 / np.sqrt(2.0))))


NOT_TOP = 64.0


def _kth_largest(s, k, want_rank=False):
    tops = []
    rank = jnp.full(s.shape, NOT_TOP, F32)
    for r in range(k):
        m = jnp.max(s, axis=0, keepdims=True)
        tops.append(m)
        hit = s == m
        if want_rank:
            rank = jnp.where(hit, float(r), rank)
        s = jnp.where(hit, -jnp.inf, s)
    return (tops, rank) if want_rank else tops


def _sort_network(n):
    size = 1 << max(n - 1, 1).bit_length()
    pairs = []
    p = 1
    while p < size:
        k = p
        while k >= 1:
            for j in range(k % p, size - k, 2 * k):
                for i in range(min(k, size - j - k)):
                    if (i + j) // (2 * p) == (i + j + k) // (2 * p):
                        pairs.append((i + j, i + j + k))
            k //= 2
        p *= 2
    return [(a, b) for a, b in pairs if b < n]


def _largest_sorted(s, k):
    xs = [s[8 * i:8 * i + 8] for i in range(s.shape[0] // 8)]
    for a, b in _sort_network(len(xs)):
        xs[a], xs[b] = jnp.maximum(xs[a], xs[b]), jnp.minimum(xs[a], xs[b])
    tops = []
    for r in range(k):
        m = jnp.max(xs[0], axis=0, keepdims=True)
        tops.append(m)
        hit = xs[0] == m
        depth = min(len(xs), k - r)
        for i in range(depth - 1):
            xs[i] = jnp.where(hit, xs[i + 1], xs[i])
        xs[depth - 1] = jnp.where(hit, -jnp.inf, xs[depth - 1])
    return tops


def _peer_kernel(x2_ref, q_ref, p_ref, k1_ref, k2_ref, *rest, n_heads, n_e, n_split):
    u_refs, vt_refs = rest[:n_split], rest[n_split:2 * n_split]
    (nf_ref, npl_ref, wg_ref, wp_ref, nfin_ref, y_ref, hb_s, n1_s, c_s, rk2_s, d_s, act0_s, act1_s, w0_s, w1_s,
     acc_s) = rest[2 * n_split:]
    j = pl.program_id(1)
    tt = x2_ref.shape[0]
    ec = n_split * u_refs[0].shape[0]
    qd = N_KEYS
    assert ec == PEER_CHUNK

    @pl.when(j == 0)
    def _():
        hb_s[...] = _rms(x2_ref[...], nf_ref[...]).T.astype(BF16)
        acc_s[...] = jnp.zeros(acc_s.shape, F32)
        sub8 = lax.broadcasted_iota(jnp.int32, (8, LANES), 0)
        n_ts = tt // LANES

        def select_experts(unit, carry):
            h = unit // n_ts
            tok = pl.ds(pl.multiple_of((unit % n_ts) * LANES, LANES), LANES)
            q1 = q_ref[tok, pl.ds(pl.multiple_of(2 * h * qd, qd), qd)]
            q2 = q_ref[tok, pl.ds(pl.multiple_of((2 * h + 1) * qd, qd), qd)]
            s1 = _dot_nt(k1_ref[h], q1)
            s2 = _dot_nt(k2_ref[h], q2)
            v1 = _largest_sorted(s1, PEER_TOPK)
            v2, rank2 = _kth_largest(s2, PEER_TOPK, want_rank=True)
            v2lo = jnp.concatenate(v2[0:8], axis=0)
            pieces = [v1[0] + v2lo, v1[0] + jnp.concatenate(v2[8:16], axis=0), v1[1] + v2lo]
            for k1 in range(2, 8):
                pieces.append(jnp.where(sub8 < PEER_TOPK // (k1 + 1), v1[k1] + v2lo, -jnp.inf))
            pieces.append(jnp.concatenate(v1[8:16], axis=0) + v2[0])
            cand = jnp.concatenate(pieces, axis=0)
            top = _largest_sorted(cand, PEER_TOPK)
            z = sum(jnp.exp(t - top[0]) for t in top)
            th = top[PEER_TOPK - 1]
            n1_s[h, :, tok] = sum(jnp.where(s1 + v >= th, 1.0, 0.0) for v in v2)
            c_s[h, :, tok] = jnp.exp(s1 - v1[0]) / z
            rk2_s[h, :, tok] = rank2.astype(BF16)
            d_s[h, :, tok] = jnp.exp(s2 - v2[0]).astype(BF16)
            return carry

        lax.fori_loop(0, n_heads * n_ts, select_experts, 0, unroll=4)

    act = (act0_s, act1_s)
    wts = (w0_s, w1_s)

    def step(par, do_a, do_b, do_c):
        i_rows = pl.ds(pl.multiple_of(jnp.clip(j - 1, 0, n_e - 1) * 8, 8), 8)
        jh = N_KEYS // 2
        n_blocks = (tt // LANES) * (N_KEYS // jh)
        d = n_split * vt_refs[0].shape[1]
        for ts in range(tt // LANES):
            tok = slice(ts * LANES, (ts + 1) * LANES)
            for jb in range(N_KEYS // jh):
                blk = ts * (N_KEYS // jh) + jb
                per = n_blocks // n_split
                if blk % per == 0:
                    sp = blk // per
                    ra = slice(sp * (ec // n_split), (sp + 1) * (ec // n_split))
                    rc = slice(sp * (d // n_split), (sp + 1) * (d // n_split))
                    if do_a:
                        act[par][ra, :] = _dot(u_refs[sp][...], hb_s[...])
                    if do_c:
                        acc_s[rc, :] += _dot(vt_refs[sp][0], wts[par][...])
                if not do_b:
                    continue
                jrows = slice(jb * jh, (jb + 1) * jh)
                acc = [jnp.zeros((jh, LANES), BF16)] * 8
                for h in range(n_heads):
                    rk, dd = rk2_s[h, jrows, tok], d_s[h, jrows, tok]
                    n1b, cb = n1_s[h, i_rows, tok], c_s[h, i_rows, tok]
                    for il in range(8):
                        sel = rk < n1b[il:il + 1].astype(BF16)
                        acc[il] = acc[il] + jnp.where(sel, dd, 0.0) * cb[il:il + 1].astype(BF16)
                for il in range(8):
                    rows = slice(il * N_KEYS + jb * jh, il * N_KEYS + (jb + 1) * jh)
                    wts[1 - par][rows, tok] = acc[il] * _erf_gelu(act[1 - par][rows, tok]).astype(BF16)

    assert n_e % 2 == 0 and n_e >= 4
    pl.when(j == 0)(lambda: step(0, True, False, False))
    pl.when(j == 1)(lambda: step(1, True, True, False))
    pl.when((j >= 2) & (j < n_e) & (j % 2 == 0))(lambda: step(0, True, True, True))
    pl.when((j >= 2) & (j < n_e) & (j % 2 == 1))(lambda: step(1, True, True, True))
    pl.when(j == n_e)(lambda: step(0, False, True, True))
    pl.when(j == n_e + 1)(lambda: step(1, False, False, True))

    @pl.when(j == n_e + 1)
    def _():
        x3 = x2_ref[...] + acc_s[...].T
        gate = _sigmoid(_dot(_rms(x3, npl_ref[...]).astype(BF16), wg_ref[...]))
        x4 = x3 + gate * _dot(p_ref[...].astype(BF16), wp_ref[...])
        y_ref[...] = _rms(x4, nfin_ref[...])


def peer_ple_final(x2, q, p, peer_k1, peer_k2, peer_u, peer_vt, norm_ffn, norm_ple, w_ple_gate, w_ple_proj,
                   norm_final, tt, ec):
    t, d = x2.shape
    n_heads = peer_k1.shape[0]
    n_exp = peer_u.shape[0]
    tt = min(tt, t)
    assert t % tt == 0 and n_exp % ec == 0 and tt % LANES == 0 and ec % N_KEYS == 0
    assert peer_k1.shape[1:] == (N_KEYS, N_KEYS) and n_exp == N_KEYS * N_KEYS
    rows = lambda n: pl.BlockSpec((tt, n), lambda i, j: (i, 0))
    full = lambda a: pl.BlockSpec(a.shape, lambda i, j: (0,) * a.ndim)
    row = lambda v: v.reshape(1, -1)
    consts = (row(norm_ffn), row(norm_ple), w_ple_gate, w_ple_proj, row(norm_final))
    n_e = n_exp // ec
    assert peer_vt.shape == (n_e, d, ec)
    hs = lambda dt: pltpu.VMEM((n_heads, N_KEYS, tt), dt)
    n_split = 2
    u_map = lambda s, i, j: (jnp.minimum(j, n_e - 1) * n_split + s, 0)
    vt_map = lambda s, i, j: (jnp.clip(j - 2, 0, n_e - 1), s, 0)
    return pl.pallas_call(
        functools.partial(_peer_kernel, n_heads=n_heads, n_e=n_e, n_split=n_split),
        grid=(t // tt, n_e + 2),
        in_specs=[rows(d), rows(q.shape[1]), rows(p.shape[1]), full(peer_k1), full(peer_k2),
                  *[pl.BlockSpec((ec // n_split, d), functools.partial(u_map, s)) for s in range(n_split)],
                  *[pl.BlockSpec((1, d // n_split, ec), functools.partial(vt_map, s)) for s in range(n_split)]]
                 + [full(a) for a in consts],
        out_specs=rows(d),
        out_shape=jax.ShapeDtypeStruct((t, d), F32),
        scratch_shapes=[pltpu.VMEM((d, tt), BF16), hs(F32), hs(F32), hs(BF16), hs(BF16),
                        pltpu.VMEM((ec, tt), F32), pltpu.VMEM((ec, tt), F32),
                        pltpu.VMEM((ec, tt), BF16), pltpu.VMEM((ec, tt), BF16), pltpu.VMEM((d, tt), F32)],
        compiler_params=_cparams(2),
        name="peer_ple_final",
    )(x2, q, p, peer_k1, peer_k2, *([peer_u] * n_split), *([peer_vt] * n_split), *consts)


def _as_column(x_row):
    n = x_row.shape[1]
    eye = lax.broadcasted_iota(jnp.int32, (n, n), 0) == lax.broadcasted_iota(jnp.int32, (n, n), 1)
    return jnp.sum(jnp.where(eye, jnp.broadcast_to(x_row, (n, n)), 0.0), axis=1, keepdims=True)


def _rows8(x_row):
    return jnp.broadcast_to(x_row, (8, x_row.shape[1]))


def _ssd_step_kernel(xbc_ref, z_ref, dt_ref, conv_ref, h_ref, convw_ref, convb_ref, dtb_ref, alog_ref, dskip_ref,
                     normg_ref, e_ref, y_ref, ho_ref, *, n_heads):
    d_inner = n_heads * HEAD
    gw = d_inner // SSM_GROUPS
    cs = conv_ref[0]
    conv = convb_ref[...] + xbc_ref[0] * convw_ref[CONV_WIDTH - 1:CONV_WIDTH, :]
    for k in range(CONV_WIDTH - 1):
        conv = conv + cs[k:k + 1, :] * convw_ref[k:k + 1, :]
    act = conv * _sigmoid(conv)
    dt = _softplus(dt_ref[0] + dtb_ref[...])
    da = jnp.exp(dt * -jnp.exp(alog_ref[...]))
    e = e_ref[...]
    dt_x = _dot_exact_rhs(_rows8(dt), e)[0:1]
    da_x = _dot_exact_rhs(_rows8(da), e)[0:1]
    for g in range(SSM_GROUPS):
        cols = slice(g * gw, (g + 1) * gw)
        xg = act[:, cols]
        bg = act[:, d_inner + g * SSM_STATE:d_inner + (g + 1) * SSM_STATE]
        cg = act[:, d_inner + (SSM_GROUPS + g) * SSM_STATE:d_inner + (SSM_GROUPS + g + 1) * SSM_STATE]
        hn = _as_column(da_x[:, cols]) * h_ref[0, cols, :] + _as_column(xg * dt_x[:, cols]) * bg
        ho_ref[0, cols, :] = hn
        y = _dot_nt(_rows8(cg).astype(BF16), hn.astype(BF16))[0:1] + xg * dskip_ref[:, cols]
        zg = z_ref[0, :, cols]
        y = y * (zg * _sigmoid(zg))
        y_ref[0, :, cols] = _rms(y, normg_ref[:, cols])


def ssd_step(u, state_conv, state_ssm, conv_w, conv_b, dt_bias, a_log, d_skip, ssm_norm, n_heads):
    b = u.shape[0]
    d_inner = n_heads * HEAD
    conv_dim = d_inner + 2 * SSM_GROUPS * SSM_STATE
    pad = lambda v: jnp.pad(v.reshape(1, -1), ((0, 0), (0, LANES - v.size)))
    e_bf = jnp.asarray(np.arange(d_inner)[None, :] // HEAD == np.arange(LANES)[:, None], BF16)
    full = lambda shape: pl.BlockSpec(shape, lambda i: (0,) * len(shape))
    u3 = u.reshape(b, 1, -1)
    y, h = pl.pallas_call(
        functools.partial(_ssd_step_kernel, n_heads=n_heads),
        grid=(b,),
        in_specs=[pl.BlockSpec((1, 1, conv_dim), lambda i: (i, 0, 0)),
                  pl.BlockSpec((1, 1, d_inner), lambda i: (i, 0, conv_dim // d_inner)),
                  pl.BlockSpec((1, 1, LANES), lambda i: (i, 0, (conv_dim + d_inner) // LANES)),
                  pl.BlockSpec((1, CONV_WIDTH - 1, conv_dim), lambda i: (i, 0, 0)),
                  pl.BlockSpec((1, d_inner, SSM_STATE), lambda i: (i, 0, 0)),
                  full((CONV_WIDTH, conv_dim)), full((1, conv_dim)), full((1, LANES)), full((1, LANES)),
                  full((1, d_inner)), full((1, d_inner)), full((LANES, d_inner))],
        out_specs=[pl.BlockSpec((1, 1, d_inner), lambda i: (i, 0, 0)),
                   pl.BlockSpec((1, d_inner, SSM_STATE), lambda i: (i, 0, 0))],
        out_shape=[jax.ShapeDtypeStruct((b, 1, d_inner), F32),
                   jax.ShapeDtypeStruct((b, d_inner, SSM_STATE), F32)],
        compiler_params=_cparams(1),
        name="ssd_step",
    )(u3, u3, u3, state_conv, state_ssm.reshape(b, d_inner, SSM_STATE), conv_w, conv_b.reshape(1, -1),
      pad(dt_bias), pad(a_log), jnp.repeat(d_skip, HEAD).reshape(1, -1), ssm_norm.reshape(1, -1), e_bf)
    return y.reshape(b, d_inner), h


def _rwkv_step_features_kernel(p_ref, prev_ref, mu_ref, w0_ref, w2_ref, a0_ref, a2_ref, g2_ref, kk_ref, ka_ref,
                               ones2_ref, o_ref, *, dim):
    r, k, v, kkf, a, lw, g = _rwkv_features(p_ref[...], prev_ref[...], mu_ref, w0_ref, w2_ref, a0_ref, a2_ref,
                                            g2_ref, kk_ref, ka_ref, dim)
    for j in range(dim // LANES):
        cols = slice(j * LANES, (j + 1) * LANES)
        kj = kkf[:, cols]
        o_ref[3, :, cols] = kj * lax.rsqrt(_dot_exact_rhs(kj * kj, ones2_ref[...]) + 1e-12)
    o_ref[0], o_ref[1], o_ref[2], o_ref[4], o_ref[5], o_ref[6] = r, k, v, a, jnp.exp(lw), g


def _rwkv_step_kernel(f_ref, s_ref, rk_ref, lng_ref, lnb_ref, y_ref, so_ref, y_s):
    n_heads = s_ref.shape[1]
    eye = lax.broadcasted_iota(jnp.int32, (HEAD, HEAD), 0) == lax.broadcasted_iota(jnp.int32, (HEAD, HEAD), 1)
    for h in range(n_heads):
        row = lambda i: f_ref[i, 0, h:h + 1, :]
        r, k, v, kk, a, w = (row(i) for i in range(6))
        s = s_ref[0, h]
        sa = jnp.sum(s * -kk, axis=1, keepdims=True)
        v_col = jnp.sum(jnp.where(eye, jnp.broadcast_to(v, (HEAD, HEAD)), 0.0), axis=1, keepdims=True)
        sn = s * w + sa * (kk * a) + v_col * k
        so_ref[0, h] = sn
        y_s[h:h + 1, :] = _dot_nt(_rows8(r).astype(BF16), sn.astype(BF16))[0:1]
    y = y_s[...]
    r, k, v, g = f_ref[0, 0], f_ref[1, 0], f_ref[2, 0], f_ref[6, 0]
    d = y - jnp.mean(y, axis=-1, keepdims=True)
    var = jnp.mean(d * d, axis=-1, keepdims=True)
    yn = d * lax.rsqrt(var + LN_X_EPS) * lng_ref[...] + lnb_ref[...]
    bonus = jnp.sum(r * k * rk_ref[...], axis=-1, keepdims=True) * v
    y_ref[0] = (yn + bonus) * g


def rwkv_step(proj, shift_prev, state_wkv, shift_mu, decay_w0, decay_w2, aaa_a0, aaa_a2, gate_g2, k_k, k_a, r_k,
              lnx_g, lnx_b):
    b, sd = proj.shape
    dim = decay_w0.size
    n_heads = dim // HEAD
    dl, al, gl = decay_w2.shape[0], aaa_a2.shape[0], gate_g2.shape[0]
    assert dl + al == LANES and gl == LANES and sd == 3 * dim + 2 * LANES
    w2 = jnp.concatenate([decay_w2, jnp.zeros((al, dim), F32)], axis=0).astype(BF16)
    a2 = jnp.concatenate([jnp.zeros((dl, dim), F32), aaa_a2], axis=0).astype(BF16)
    ones2 = jnp.asarray(np.arange(LANES)[:, None] // HEAD == np.arange(LANES)[None, :] // HEAD, BF16)
    row = lambda v: v.reshape(1, -1)
    args = (proj, shift_prev, row(shift_mu), row(decay_w0), w2, row(aaa_a0), a2, gate_g2.astype(BF16), row(k_k),
            row(k_a), ones2)
    feats = pl.pallas_call(
        functools.partial(_rwkv_step_features_kernel, dim=dim),
        grid=(1,),
        in_specs=[pl.BlockSpec(a.shape, lambda i: (0, 0)) for a in args],
        out_specs=pl.BlockSpec((7, b, dim), lambda i: (0, 0, 0)),
        out_shape=jax.ShapeDtypeStruct((7, b, dim), F32),
        compiler_params=_cparams(1),
        name="rwkv_step_features",
    )(*args)
    hv = lambda v: v.reshape(n_heads, HEAD)
    full = pl.BlockSpec((n_heads, HEAD), lambda i: (0, 0))
    y, s = pl.pallas_call(
        _rwkv_step_kernel,
        grid=(b,),
        in_specs=[pl.BlockSpec((7, 1, n_heads, HEAD), lambda i: (0, i, 0, 0)),
                  pl.BlockSpec((1, n_heads, HEAD, HEAD), lambda i: (i, 0, 0, 0)), full, full, full],
        out_specs=[pl.BlockSpec((1, n_heads, HEAD), lambda i: (i, 0, 0)),
                   pl.BlockSpec((1, n_heads, HEAD, HEAD), lambda i: (i, 0, 0, 0))],
        out_shape=[jax.ShapeDtypeStruct((b, n_heads, HEAD), F32),
                   jax.ShapeDtypeStruct((b, n_heads, HEAD, HEAD), F32)],
        scratch_shapes=[pltpu.VMEM((n_heads, HEAD), F32)],
        compiler_params=_cparams(1),
        name="rwkv_step",
    )(feats.reshape(7, b, n_heads, HEAD), state_wkv, hv(r_k), hv(lnx_g), hv(lnx_b))
    return y.reshape(b, dim), s


def _layer(x, p, states, wts, n_ssm_heads):
    b, l, d = x.shape
    xt = x.reshape(b * l, d)
    g_mix = wts['norm_mix'].reshape(1, -1)
    u_ssm, u_rwkv, u_gate = (norm_matmul(xt, g_mix, wts[k], PROJ_ROWS, _proj_col_tile(wts[k].shape[1]))
                             for k in ('w_ssm', 'w_shift', 'w_gates'))
    d_inner = n_ssm_heads * HEAD
    conv_dim = d_inner + 2 * SSM_GROUPS * SSM_STATE
    ssd_w = (wts['conv_w'], wts['conv_b'], wts['dt_bias'], wts['a_log'], wts['d_skip'], wts['ssm_norm'])
    rwkv_w = tuple(wts[k] for k in ('shift_mu', 'decay_w0', 'decay_w2', 'aaa_a0', 'aaa_a2', 'gate_g2', 'k_k', 'k_a',
                                    'r_k', 'lnx_g', 'lnx_b'))
    if states is None:
        ym, ssm_new = ssd_prompt(u_ssm.reshape(b, l, -1), *ssd_w, n_ssm_heads)
        ym = ym.reshape(b * l, d_inner)
        conv_new = u_ssm.reshape(b, l, -1)[:, l - (CONV_WIDTH - 1):, :conv_dim]
        yr, wkv_new = rwkv_prompt(u_rwkv.reshape(b, l, -1), *rwkv_w)
        yr = yr.reshape(b * l, -1)
        shift_new = u_rwkv.reshape(b, l, -1)[:, l - 1]
    else:
        conv_prev, ssm_prev, wkv_prev, shift_prev = states
        ym, ssm_new = ssd_step(u_ssm, conv_prev, ssm_prev, *ssd_w, n_ssm_heads)
        conv_new = jnp.concatenate([conv_prev[:, 1:], u_ssm[:, None, :conv_dim]], axis=1)
        yr, wkv_new = rwkv_step(u_rwkv, shift_prev, wkv_prev, *rwkv_w)
        shift_new = u_rwkv
    x2, q = mix_and_query(xt, ym, yr, u_gate, wts['w_out_ssm'], wts['w_out_rwkv'], wts['w_out'], wts['norm_ffn'],
                          wts['peer_wq'], MIX_ROWS)
    y = peer_ple_final(x2, q, p.reshape(b * l, -1), wts['peer_k1'], wts['peer_k2'], wts['peer_u'], wts['peer_vt'],
                       wts['norm_ffn'], wts['norm_ple'], wts['w_ple_gate'], wts['w_ple_proj'], wts['norm_final'],
                       PEER_ROWS, PEER_CHUNK)
    return (y.reshape(b, l, d), ssm_new.reshape(b, n_ssm_heads, HEAD, SSM_STATE), conv_new, wkv_new, shift_new)


def kernel(x_prompt, x_sample, p_prompt, p_sample, state_ssm, state_conv, state_wkv, state_shift, norm_mix, w_in,
           conv_w, conv_b, dt_bias, a_log, d_skip, ssm_norm, w_out_ssm, shift_mu, decay_w0, decay_w2, aaa_a0, aaa_a2,
           gate_g2, k_k, k_a, r_k, lnx_g, lnx_b, w_out_rwkv, w_out, norm_ffn, peer_wq, peer_k1, peer_k2, peer_u,
           peer_v, norm_ple, w_ple_gate, w_ple_proj, norm_final):
    depth = w_in.shape[0]
    assert depth == 1, "single-layer trunk"
    d_model = x_prompt.shape[-1]
    n_ssm_heads = dt_bias.shape[1]
    d_inner = n_ssm_heads * HEAD
    conv_dim = conv_w.shape[2]
    shift_dim = shift_mu.shape[1]
    bf = lambda a: a.astype(BF16)
    o = np.cumsum([0, d_inner, conv_dim, n_ssm_heads, shift_dim, d_model, d_model])
    wi = w_in[0]
    wts = {
        'w_ssm': bf(jnp.concatenate([wi[:, o[1]:o[2]], wi[:, o[0]:o[1]], wi[:, o[2]:o[3]],
                                     jnp.zeros((d_model, LANES - n_ssm_heads), F32)], axis=1)),
        'w_shift': bf(wi[:, o[3]:o[4]]),
        'w_gates': bf(wi[:, o[4]:o[6]]),
        'w_out_ssm': bf(w_out_ssm[0]), 'w_out_rwkv': bf(w_out_rwkv[0]), 'w_out': bf(w_out[0]),
        'peer_wq': bf(peer_wq[0]), 'peer_k1': bf(peer_k1[0]), 'peer_k2': bf(peer_k2[0]),
        'peer_u': bf(peer_u[0]),
        'peer_vt': bf(peer_v[0]).reshape(-1, PEER_CHUNK, d_model).transpose(0, 2, 1),
        'w_ple_gate': bf(w_ple_gate[0]), 'w_ple_proj': bf(w_ple_proj[0]), 'norm_final': norm_final,
    }
    for name, val in (('norm_mix', norm_mix), ('conv_w', conv_w), ('conv_b', conv_b), ('dt_bias', dt_bias),
                      ('a_log', a_log), ('d_skip', d_skip), ('ssm_norm', ssm_norm), ('shift_mu', shift_mu),
                      ('decay_w0', decay_w0), ('decay_w2', decay_w2), ('aaa_a0', aaa_a0), ('aaa_a2', aaa_a2),
                      ('gate_g2', gate_g2), ('k_k', k_k), ('k_a', k_a), ('r_k', r_k), ('lnx_g', lnx_g),
                      ('lnx_b', lnx_b), ('norm_ffn', norm_ffn), ('norm_ple', norm_ple)):
        wts[name] = val[0]
    yp, ssm_p, conv_p, wkv_p, shift_p = _layer(x_prompt, p_prompt[0], None, wts, n_ssm_heads)
    ys, ssm_s, conv_s, wkv_s, shift_s = _layer(
        x_sample, p_sample[0], (state_conv[0], state_ssm[0], state_wkv[0], state_shift[0]), wts, n_ssm_heads)
    return (yp, ys, ssm_p[None], conv_p[None], wkv_p[None], shift_p[None],
            ssm_s[None], conv_s[None], wkv_s[None], shift_s[None])
```

```python
import functools

import numpy as np
import jax
import jax.numpy as jnp
from jax import lax
from jax.experimental import pallas as pl
from jax.experimental.pallas import tpu as pltpu

F32 = jnp.float32
BF16 = jnp.bfloat16

EPS = 1e-6
LN_X_EPS = 64e-5
HEAD = 64
SSM_STATE = 128
SSM_GROUPS = 8
CONV_WIDTH = 4
CHUNK = 128
LANES = 128
VMEM_LIMIT = 56 * 1024 * 1024

PROJ_ROWS = 1024
PROJ_COLS_MAX = 1792
MIX_ROWS = 512
STEP_SEQS = 4
PEER_ROWS = 512


def _proj_col_tile(n):
    return max(c for c in range(LANES, PROJ_COLS_MAX + 1, LANES) if n % c == 0)


def _cparams(n_axes):
    return pltpu.CompilerParams(dimension_semantics=("arbitrary",) * n_axes,
                                vmem_limit_bytes=VMEM_LIMIT)


def _dot(a, b):
    return jnp.dot(a, b, preferred_element_type=F32)


def _dot_nt(a, b):
    return lax.dot_general(a, b, (((1,), (1,)), ((), ())), preferred_element_type=F32)


def _dot_tn(a, b):
    return lax.dot_general(a, b, (((0,), (0,)), ((), ())), preferred_element_type=F32)


def _split3(x):
    x1 = x.astype(BF16)
    r = x - x1.astype(F32)
    x2 = r.astype(BF16)
    x3 = (r - x2.astype(F32)).astype(BF16)
    return x1, x2, x3


def _dot_exact_rhs(a, e):
    a1, a2, a3 = _split3(a)
    return _dot(a1, e) + (_dot(a2, e) + _dot(a3, e))


def _dot_exact_lhs(e, a):
    a1, a2, a3 = _split3(a)
    return _dot(e, a1) + (_dot(e, a2) + _dot(e, a3))


def _rms(x, g):
    return x * lax.rsqrt(jnp.mean(x * x, axis=-1, keepdims=True) + EPS) * g


def _sigmoid(x):
    return 1.0 / (1.0 + jnp.exp(-x))


def _softplus(x):
    return jnp.maximum(x, 0.0) + jnp.log1p(jnp.exp(-jnp.abs(x)))


def _tril(n, k=0, dtype=F32):
    r = lax.broadcasted_iota(jnp.int32, (n, n), 0)
    c = lax.broadcasted_iota(jnp.int32, (n, n), 1)
    return (c <= r + k)


def _norm_matmul_kernel(x_ref, g_ref, w_ref, o_ref, h_ref):
    @pl.when(pl.program_id(1) == 0)
    def _():
        h_ref[...] = _rms(x_ref[...], g_ref[...]).astype(BF16)

    o_ref[...] = _dot(h_ref[...], w_ref[...])


def norm_matmul(x, g, w, tm, tn):
    m, k = x.shape
    n = w.shape[1]
    tm = min(tm, m)
    assert m % tm == 0 and n % tn == 0, (m, tm, n, tn)
    return pl.pallas_call(
        _norm_matmul_kernel,
        grid=(m // tm, n // tn),
        in_specs=[pl.BlockSpec((tm, k), lambda i, j: (i, 0)),
                  pl.BlockSpec((1, k), lambda i, j: (0, 0)),
                  pl.BlockSpec((k, tn), lambda i, j: (0, j))],
        out_specs=pl.BlockSpec((tm, tn), lambda i, j: (i, j)),
        out_shape=jax.ShapeDtypeStruct((m, n), F32),
        scratch_shapes=[pltpu.VMEM((tm, k), BF16)],
        compiler_params=_cparams(2),
        name="norm_matmul",
    )(x, g, w)


def _ssd_chunk_kernel(xbc_ref, z_ref, dt_ref, convw_ref, convb_ref, dtb_ref, alog_ref, dskip_ref,
                      normg_ref, e_ref, et_ref, y_ref, h_ref, ext_ref, act_ref, *, n_heads):
    c = pl.program_id(1)
    C = xbc_ref.shape[1]
    d_inner = n_heads * HEAD
    gw = d_inner // SSM_GROUPS
    hpg = n_heads // SSM_GROUPS

    @pl.when(c == 0)
    def _():
        ext_ref[0:8, :] = jnp.zeros((8, ext_ref.shape[1]), F32)
        h_ref[...] = jnp.zeros(h_ref.shape, F32)

    @pl.when(c > 0)
    def _():
        ext_ref[0:8, :] = ext_ref[C:C + 8, :]

    ext_ref[8:C + 8, :] = xbc_ref[0]
    conv = convb_ref[...]
    for k in range(CONV_WIDTH):
        off = 8 - (CONV_WIDTH - 1) + k
        conv = conv + ext_ref[off:off + C, :] * convw_ref[k:k + 1, :]
    act_ref[...] = conv * _sigmoid(conv)

    dt = _softplus(dt_ref[0] + dtb_ref[...])
    a = -jnp.exp(alog_ref[...])
    tril = _tril(C)
    acum = _dot_exact_lhs(tril.astype(BF16), dt * a)
    acum_t = acum.T
    dt_t = dt.T
    e = e_ref[...]
    eacum_x = _dot_exact_rhs(jnp.exp(acum), e)
    wdec_x = _dot_exact_rhs(jnp.exp(acum[C - 1:C, :] - acum) * dt, e)
    dec_b = jnp.broadcast_to(jnp.exp(acum_t[:, C - 1:C]), (LANES, SSM_STATE))
    lane = lax.broadcasted_iota(jnp.int32, (C, gw), 1)

    for g in range(SSM_GROUPS):
        cols = slice(g * gw, (g + 1) * gw)
        xg = act_ref[:, cols]
        bg = act_ref[:, d_inner + g * SSM_STATE:d_inner + (g + 1) * SSM_STATE].astype(BF16)
        cg = act_ref[:, d_inner + (SSM_GROUPS + g) * SSM_STATE:
                     d_inner + (SSM_GROUPS + g + 1) * SSM_STATE].astype(BF16)
        cb = _dot_nt(cg, bg)
        y = xg * dskip_ref[:, cols]
        for r in range(hpg):
            h = g * hpg + r
            seg = acum[:, h:h + 1] - acum_t[h:h + 1, :]
            m = cb * jnp.exp(jnp.where(tril, seg, -jnp.inf)) * dt_t[h:h + 1, :]
            xm = jnp.where((lane >= r * HEAD) & (lane < (r + 1) * HEAD), xg, 0.0)
            y = y + _dot(m.astype(BF16), xm.astype(BF16))
        hg = h_ref[0, cols, :]
        y = y + _dot_nt(cg, hg.astype(BF16)) * eacum_x[:, cols]
        zg = z_ref[0, :, cols]
        y = y * (zg * _sigmoid(zg))
        y = _rms(y, normg_ref[:, cols])
        y_ref[0, :, cols] = y.astype(y_ref.dtype)
        dec = _dot_exact_lhs(et_ref[cols, :], dec_b)
        h_ref[0, cols, :] = dec * hg + _dot_tn((xg * wdec_x[:, cols]).astype(BF16), bg)


def ssd_prompt(u, conv_w, conv_b, dt_bias, a_log, d_skip, ssm_norm, n_heads):
    b, l, _ = u.shape
    d_inner = n_heads * HEAD
    conv_dim = d_inner + 2 * SSM_GROUPS * SSM_STATE
    C = CHUNK
    assert l % C == 0 and conv_dim % d_inner == 0
    pad = lambda v: jnp.pad(v.reshape(1, -1), ((0, 0), (0, LANES - v.size)))
    e = (np.arange(d_inner)[None, :] // HEAD == np.arange(LANES)[:, None])
    e_bf = jnp.asarray(e, BF16)
    et_bf = jnp.asarray(e.T, BF16)
    full = lambda shape: pl.BlockSpec(shape, lambda i, j: (0,) * len(shape))
    return pl.pallas_call(
        functools.partial(_ssd_chunk_kernel, n_heads=n_heads),
        grid=(b, l // C),
        in_specs=[pl.BlockSpec((1, C, conv_dim), lambda i, j: (i, j, 0)),
                  pl.BlockSpec((1, C, d_inner), lambda i, j: (i, j, conv_dim // d_inner)),
                  pl.BlockSpec((1, C, LANES), lambda i, j: (i, j, (conv_dim + d_inner) // LANES)),
                  full((CONV_WIDTH, conv_dim)), full((1, conv_dim)), full((1, LANES)), full((1, LANES)),
                  full((1, d_inner)), full((1, d_inner)), full((LANES, d_inner)), full((d_inner, LANES))],
        out_specs=[pl.BlockSpec((1, C, d_inner), lambda i, j: (i, j, 0)),
                   pl.BlockSpec((1, d_inner, SSM_STATE), lambda i, j: (i, 0, 0))],
        out_shape=[jax.ShapeDtypeStruct((b, l, d_inner), BF16),
                   jax.ShapeDtypeStruct((b, d_inner, SSM_STATE), F32)],
        scratch_shapes=[pltpu.VMEM((C + 8, conv_dim), F32), pltpu.VMEM((C, conv_dim), F32)],
        compiler_params=_cparams(2),
        name="ssd_prompt",
    )(u, u, u, conv_w, conv_b.reshape(1, -1), pad(dt_bias), pad(a_log),
      jnp.repeat(d_skip, HEAD).reshape(1, -1), ssm_norm.reshape(1, -1), e_bf, et_bf)


def _head_masks(shape):
    lane = lax.broadcasted_iota(jnp.int32, shape, 1)
    return lane < HEAD, lane >= HEAD


def _rwkv_features(proj, prev, mu_ref, w0_ref, w2_ref, a0_ref, a2_ref, g2_ref, kk_ref, ka_ref, dim):
    xs = proj + (prev - proj) * mu_ref[...]
    k = xs[:, dim:2 * dim]
    t_wa = xs[:, 3 * dim:3 * dim + LANES]
    xg = xs[:, 3 * dim + LANES:3 * dim + 2 * LANES]
    w = -_softplus(-(w0_ref[...] + _dot(jnp.tanh(t_wa).astype(BF16), w2_ref[...]))) - 0.5
    a = _sigmoid(a0_ref[...] + _dot(t_wa.astype(BF16), a2_ref[...]))
    g = _dot(_sigmoid(xg).astype(BF16), g2_ref[...])
    return (xs[:, 0:dim], k * (1.0 + (a - 1.0) * ka_ref[...]), xs[:, 2 * dim:3 * dim], k * kk_ref[...], a,
            -jnp.exp(w), g)


def _rwkv_chunk_kernel(p_ref, mu_ref, w0_ref, w2_ref, a0_ref, a2_ref, g2_ref, kk_ref, ka_ref, rk_ref,
                       lng_ref, lnb_ref, ones2_ref, o_ref, s_ref,
                       ext_ref, r_s, k_s, v_s, kkn_s, a_s, lw_s, g_s, x_s, n_s, q_s, yb_s, bkh_s, pc_s, *, dim):
    c = pl.program_id(1)
    C = p_ref.shape[1]
    n_pairs = dim // LANES

    @pl.when(c == 0)
    def _():
        ext_ref[0:8, :] = jnp.zeros((8, ext_ref.shape[1]), F32)
        s_ref[...] = jnp.zeros(s_ref.shape, F32)

    @pl.when(c > 0)
    def _():
        ext_ref[0:8, :] = ext_ref[C:C + 8, :]

    proj = p_ref[0]
    ext_ref[8:C + 8, :] = proj
    (r_s[...], k_s[...], v_s[...], kkn_s[...], a_s[...], lw_s[...], g_s[...]) = _rwkv_features(
        proj, ext_ref[7:C + 7, :], mu_ref, w0_ref, w2_ref, a0_ref, a2_ref, g2_ref, kk_ref, ka_ref, dim)

    tril = _tril(C)
    tril_strict = _tril(C, -1)
    tril_bf = tril.astype(BF16)
    ones2 = ones2_ref[...]
    m0, m1 = _head_masks((C, LANES))
    row = lax.broadcasted_iota(jnp.int32, (LANES, LANES), 0)
    col = lax.broadcasted_iota(jnp.int32, (LANES, LANES), 1)
    blockdiag = (row < HEAD) == (col < HEAD)
    n_levels = C.bit_length() - 1
    assert 1 << n_levels == C

    def by_head(x):
        return jnp.concatenate([jnp.where(m0, x, 0.0), jnp.where(m1, x, 0.0)], axis=0)

    def head_sum(x):
        s0 = jnp.sum(jnp.where(m0, x, 0.0), axis=1, keepdims=True)
        s1 = jnp.sum(jnp.where(m1, x, 0.0), axis=1, keepdims=True)
        return jnp.where(m0, s0, s1)

    for j in range(n_pairs):
        cols = slice(j * LANES, (j + 1) * LANES)
        lw = lw_s[:, cols]
        logp = _dot_exact_lhs(tril_bf, lw)
        logpc = logp[C - 1:C, :]
        ep, epinv = jnp.exp(logp), jnp.exp(-logp)
        epc = jnp.exp(logpc - logp)
        kkf = kkn_s[:, cols]
        kk = kkf * lax.rsqrt(head_sum(kkf * kkf) + 1e-12)
        aj, kj, vj, rj = a_s[:, cols], k_s[:, cols], v_s[:, cols], r_s[:, cols]
        bvec = kk * aj
        at = -kk * jnp.exp(logp - lw)
        rt = rj * ep
        bk_bf = jnp.concatenate([bvec * epinv, kj * epinv], axis=0).astype(BF16)
        ar = jnp.concatenate([at, rt], axis=0)
        as_ = _dot_nt(ar.astype(BF16), s_ref[0, j].astype(BF16))
        mab, mak, qab, qak = [], [], [], []
        for hm in (m0, m1):
            g = _dot_nt(jnp.where(jnp.concatenate([hm, hm], axis=0), ar, 0.0).astype(BF16), bk_bf)
            mab.append(jnp.where(tril_strict, g[0:C, 0:C], 0.0))
            mak.append(jnp.where(tril_strict, g[0:C, C:2 * C], 0.0))
            qab.append(jnp.where(tril, g[C:2 * C, 0:C], 0.0))
            qak.append(jnp.where(tril, g[C:2 * C, C:2 * C], 0.0))
        x_s[j] = as_[0:C] + _dot(jnp.concatenate(mak, axis=1).astype(BF16), by_head(vj).astype(BF16))
        n_s[j] = jnp.concatenate(mab, axis=1).astype(BF16)
        q_s[j] = jnp.concatenate(qab + qak, axis=1).astype(BF16)
        yb_s[j] = as_[C:2 * C]
        bkh_s[j] = jnp.concatenate([bvec * epc, kj * epc], axis=0).astype(BF16)
        pc_s[j] = jnp.broadcast_to(jnp.exp(logpc), (8, LANES))

    zero = jnp.zeros((C, C), BF16)
    for lvl in range(n_levels):
        for j in range(n_pairs):
            n = n_s[j]
            x_s[j] += _dot(n, by_head(x_s[j]).astype(BF16))
            if lvl + 1 < n_levels:
                nd = jnp.concatenate([jnp.concatenate([n[:, 0:C], zero], axis=1),
                                      jnp.concatenate([zero, n[:, C:2 * C]], axis=1)], axis=0)
                n_s[j] = _dot(n, nd).astype(BF16)

    for j in range(n_pairs):
        cols = slice(j * LANES, (j + 1) * LANES)
        kj, vj, rj = k_s[:, cols], v_s[:, cols], r_s[:, cols]
        x = x_s[j]
        y = yb_s[j] + _dot(q_s[j], jnp.concatenate([by_head(x), by_head(vj)], axis=0).astype(BF16))
        upd = _dot_tn(jnp.concatenate([x, vj], axis=0).astype(BF16), bkh_s[j])
        s_ref[0, j] = pc_s[j, 0:1, :] * s_ref[0, j] + jnp.where(blockdiag, upd, 0.0)

        mean = head_sum(y) * (1.0 / HEAD)
        d = y - mean
        var = head_sum(d * d) * (1.0 / HEAD)
        yn = d * lax.rsqrt(var + LN_X_EPS) * lng_ref[:, cols] + lnb_ref[:, cols]
        bonus = head_sum(rj * kj * rk_ref[:, cols]) * vj
        o_ref[0, :, cols] = ((yn + bonus) * g_s[:, cols]).astype(o_ref.dtype)


def rwkv_prompt(proj, shift_mu, decay_w0, decay_w2, aaa_a0, aaa_a2, gate_g2, k_k, k_a, r_k, lnx_g, lnx_b):
    b, l, sd = proj.shape
    dim = decay_w0.size
    n_heads = dim // HEAD
    dl, al, gl = decay_w2.shape[0], aaa_a2.shape[0], gate_g2.shape[0]
    assert dl + al == LANES and gl == LANES and sd == 3 * dim + 2 * LANES and l % CHUNK == 0
    C = CHUNK
    n_pairs = dim // LANES
    w2 = jnp.concatenate([decay_w2, jnp.zeros((al, dim), F32)], axis=0).astype(BF16)
    a2 = jnp.concatenate([jnp.zeros((dl, dim), F32), aaa_a2], axis=0).astype(BF16)
    ones2 = jnp.asarray(np.arange(LANES)[:, None] // HEAD == np.arange(LANES)[None, :] // HEAD, BF16)
    row = lambda v: v.reshape(1, -1)
    full = lambda shape: pl.BlockSpec(shape, lambda i, j: (0,) * len(shape))
    vec = full((1, dim))
    y, s = pl.pallas_call(
        functools.partial(_rwkv_chunk_kernel, dim=dim),
        grid=(b, l // C),
        in_specs=[pl.BlockSpec((1, C, sd), lambda i, j: (i, j, 0)), full((1, sd)), vec, full((LANES, dim)), vec,
                  full((LANES, dim)), full((LANES, dim)), vec, vec, vec, vec, vec, full((LANES, LANES))],
        out_specs=[pl.BlockSpec((1, C, dim), lambda i, j: (i, j, 0)),
                   pl.BlockSpec((1, dim // LANES, LANES, LANES), lambda i, j: (i, 0, 0, 0))],
        out_shape=[jax.ShapeDtypeStruct((b, l, dim), BF16),
                   jax.ShapeDtypeStruct((b, dim // LANES, LANES, LANES), F32)],
        scratch_shapes=[pltpu.VMEM((C + 8, sd), F32)] + [pltpu.VMEM((C, dim), F32)] * 7
                       + [pltpu.VMEM((n_pairs, C, LANES), F32), pltpu.VMEM((n_pairs, C, 2 * C), BF16),
                          pltpu.VMEM((n_pairs, C, 4 * C), BF16), pltpu.VMEM((n_pairs, C, LANES), F32),
                          pltpu.VMEM((n_pairs, 2 * C, LANES), BF16), pltpu.VMEM((n_pairs, 8, LANES), F32)],
        compiler_params=_cparams(2),
        name="rwkv_prompt",
    )(proj, row(shift_mu), row(decay_w0), w2, row(aaa_a0), a2, gate_g2.astype(BF16), row(k_k), row(k_a),
      row(r_k), row(lnx_g), row(lnx_b), ones2)
    s = s.reshape(b, dim // LANES, 2, HEAD, 2, HEAD)
    s = jnp.stack([s[:, :, 0, :, 0, :], s[:, :, 1, :, 1, :]], axis=2).reshape(b, n_heads, HEAD, HEAD)
    return y, s


def _mix_kernel(x_ref, ym_ref, yr_ref, gate_ref, wm_ref, wr_ref, wo_ref, nf_ref, wq_ref, x2_ref, q_ref):
    d = x_ref.shape[1]
    y_m = _dot(ym_ref[...].astype(BF16), wm_ref[...])
    y_r = _dot(yr_ref[...].astype(BF16), wr_ref[...])
    mix = _sigmoid(gate_ref[:, 0:d]) * y_m + _sigmoid(gate_ref[:, d:2 * d]) * y_r
    x2 = x_ref[...] + _dot(mix.astype(BF16), wo_ref[...])
    x2_ref[...] = x2
    q_ref[...] = _dot(_rms(x2, nf_ref[...]).astype(BF16), wq_ref[...]).astype(q_ref.dtype)


def mix_and_query(x, ym, yr, gates, w_out_ssm, w_out_rwkv, w_out, norm_ffn, peer_wq, tm):
    m, d = x.shape
    tm = min(tm, m)
    assert m % tm == 0
    rows = lambda n: pl.BlockSpec((tm, n), lambda i: (i, 0))
    full = lambda a: pl.BlockSpec(a.shape, lambda i: (0, 0))
    nq = peer_wq.shape[1]
    args = (x, ym, yr, gates, w_out_ssm, w_out_rwkv, w_out, norm_ffn.reshape(1, -1), peer_wq)
    return pl.pallas_call(
        _mix_kernel,
        grid=(m // tm,),
        in_specs=[rows(d), rows(ym.shape[1]), rows(yr.shape[1]), rows(2 * d)] + [full(a) for a in args[4:]],
        out_specs=[rows(d), rows(nq)],
        out_shape=[jax.ShapeDtypeStruct((m, d), F32), jax.ShapeDtypeStruct((m, nq), BF16)],
        compiler_params=_cparams(1),
        name="mix_and_query",
    )(*args)


PEER_TOPK = 16
N_KEYS = 128
PEER_CHUNK = 8 * N_KEYS


def _erf_gelu(x):
    return 0.5 * x * (1.0 + lax.erf(x * np.float32(1.0 / np.sqrt(2.0))))


NOT_TOP = 64.0


def _kth_largest(s, k, want_rank=False):
    tops = []
    rank = jnp.full(s.shape, NOT_TOP, F32)
    for r in range(k):
        m = jnp.max(s, axis=0, keepdims=True)
        tops.append(m)
        hit = s == m
        if want_rank:
            rank = jnp.where(hit, float(r), rank)
        s = jnp.where(hit, -jnp.inf, s)
    return (tops, rank) if want_rank else tops


def _sort_network(n):
    size = 1 << max(n - 1, 1).bit_length()
    pairs = []
    p = 1
    while p < size:
        k = p
        while k >= 1:
            for j in range(k % p, size - k, 2 * k):
                for i in range(min(k, size - j - k)):
                    if (i + j) // (2 * p) == (i + j + k) // (2 * p):
                        pairs.append((i + j, i + j + k))
            k //= 2
        p *= 2
    return [(a, b) for a, b in pairs if b < n]


def _largest_sorted(s, k):
    xs = [s[8 * i:8 * i + 8] for i in range(s.shape[0] // 8)]
    for a, b in _sort_network(len(xs)):
        xs[a], xs[b] = jnp.maximum(xs[a], xs[b]), jnp.minimum(xs[a], xs[b])
    tops = []
    for r in range(k):
        m = jnp.max(xs[0], axis=0, keepdims=True)
        tops.append(m)
        hit = xs[0] == m
        depth = min(len(xs), k - r)
        for i in range(depth - 1):
            xs[i] = jnp.where(hit, xs[i + 1], xs[i])
        xs[depth - 1] = jnp.where(hit, -jnp.inf, xs[depth - 1])
    return tops


def _peer_kernel(x2_ref, q_ref, p_ref, k1_ref, k2_ref, *rest, n_heads, n_e, n_split):
    u_refs, vt_refs = rest[:n_split], rest[n_split:2 * n_split]
    (nf_ref, npl_ref, wg_ref, wp_ref, nfin_ref, y_ref, hb_s, n1_s, c_s, rk2_s, d_s, act0_s, act1_s, w0_s, w1_s,
     acc_s) = rest[2 * n_split:]
    j = pl.program_id(1)
    tt = x2_ref.shape[0]
    ec = n_split * u_refs[0].shape[0]
    qd = N_KEYS
    assert ec == PEER_CHUNK

    @pl.when(j == 0)
    def _():
        hb_s[...] = _rms(x2_ref[...], nf_ref[...]).T.astype(BF16)
        acc_s[...] = jnp.zeros(acc_s.shape, F32)
        sub8 = lax.broadcasted_iota(jnp.int32, (8, LANES), 0)
        n_ts = tt // LANES

        def select_experts(unit, carry):
            h = unit // n_ts
            tok = pl.ds(pl.multiple_of((unit % n_ts) * LANES, LANES), LANES)
            q1 = q_ref[tok, pl.ds(pl.multiple_of(2 * h * qd, qd), qd)]
            q2 = q_ref[tok, pl.ds(pl.multiple_of((2 * h + 1) * qd, qd), qd)]
            s1 = _dot_nt(k1_ref[h], q1)
            s2 = _dot_nt(k2_ref[h], q2)
            v1 = _largest_sorted(s1, PEER_TOPK)
            v2, rank2 = _kth_largest(s2, PEER_TOPK, want_rank=True)
            v2lo = jnp.concatenate(v2[0:8], axis=0)
            pieces = [v1[0] + v2lo, v1[0] + jnp.concatenate(v2[8:16], axis=0), v1[1] + v2lo]
            for k1 in range(2, 8):
                pieces.append(jnp.where(sub8 < PEER_TOPK // (k1 + 1), v1[k1] + v2lo, -jnp.inf))
            pieces.append(jnp.concatenate(v1[8:16], axis=0) + v2[0])
            cand = jnp.concatenate(pieces, axis=0)
            top = _largest_sorted(cand, PEER_TOPK)
            z = sum(jnp.exp(t - top[0]) for t in top)
            th = top[PEER_TOPK - 1]
            n1_s[h, :, tok] = sum(jnp.where(s1 + v >= th, 1.0, 0.0) for v in v2)
            c_s[h, :, tok] = jnp.exp(s1 - v1[0]) / z
            rk2_s[h, :, tok] = rank2.astype(BF16)
            d_s[h, :, tok] = jnp.exp(s2 - v2[0]).astype(BF16)
            return carry

        lax.fori_loop(0, n_heads * n_ts, select_experts, 0, unroll=4)

    act = (act0_s, act1_s)
    wts = (w0_s, w1_s)

    def step(par, do_a, do_b, do_c):
        i_rows = pl.ds(pl.multiple_of(jnp.clip(j - 1, 0, n_e - 1) * 8, 8), 8)
        jh = N_KEYS // 2
        n_blocks = (tt // LANES) * (N_KEYS // jh)
        d = n_split * vt_refs[0].shape[1]
        for ts in range(tt // LANES):
            tok = slice(ts * LANES, (ts + 1) * LANES)
            for jb in range(N_KEYS // jh):
                blk = ts * (N_KEYS // jh) + jb
                per = n_blocks // n_split
                if blk % per == 0:
                    sp = blk // per
                    ra = slice(sp * (ec // n_split), (sp + 1) * (ec // n_split))
                    rc = slice(sp * (d // n_split), (sp + 1) * (d // n_split))
                    if do_a:
                        act[par][ra, :] = _dot(u_refs[sp][...], hb_s[...])
                    if do_c:
                        acc_s[rc, :] += _dot(vt_refs[sp][0], wts[par][...])
                if not do_b:
                    continue
                jrows = slice(jb * jh, (jb + 1) * jh)
                acc = [jnp.zeros((jh, LANES), BF16)] * 8
                for h in range(n_heads):
                    rk, dd = rk2_s[h, jrows, tok], d_s[h, jrows, tok]
                    n1b, cb = n1_s[h, i_rows, tok], c_s[h, i_rows, tok]
                    for il in range(8):
                        sel = rk < n1b[il:il + 1].astype(BF16)
                        acc[il] = acc[il] + jnp.where(sel, dd, 0.0) * cb[il:il + 1].astype(BF16)
                for il in range(8):
                    rows = slice(il * N_KEYS + jb * jh, il * N_KEYS + (jb + 1) * jh)
                    wts[1 - par][rows, tok] = acc[il] * _erf_gelu(act[1 - par][rows, tok]).astype(BF16)

    assert n_e % 2 == 0 and n_e >= 4
    pl.when(j == 0)(lambda: step(0, True, False, False))
    pl.when(j == 1)(lambda: step(1, True, True, False))
    pl.when((j >= 2) & (j < n_e) & (j % 2 == 0))(lambda: step(0, True, True, True))
    pl.when((j >= 2) & (j < n_e) & (j % 2 == 1))(lambda: step(1, True, True, True))
    pl.when(j == n_e)(lambda: step(0, False, True, True))
    pl.when(j == n_e + 1)(lambda: step(1, False, False, True))

    @pl.when(j == n_e + 1)
    def _():
        x3 = x2_ref[...] + acc_s[...].T
        gate = _sigmoid(_dot(_rms(x3, npl_ref[...]).astype(BF16), wg_ref[...]))
        x4 = x3 + gate * _dot(p_ref[...].astype(BF16), wp_ref[...])
        y_ref[...] = _rms(x4, nfin_ref[...])


def peer_ple_final(x2, q, p, peer_k1, peer_k2, peer_u, peer_vt, norm_ffn, norm_ple, w_ple_gate, w_ple_proj,
                   norm_final, tt, ec):
    t, d = x2.shape
    n_heads = peer_k1.shape[0]
    n_exp = peer_u.shape[0]
    tt = min(tt, t)
    assert t % tt == 0 and n_exp % ec == 0 and tt % LANES == 0 and ec % N_KEYS == 0
    assert peer_k1.shape[1:] == (N_KEYS, N_KEYS) and n_exp == N_KEYS * N_KEYS
    rows = lambda n: pl.BlockSpec((tt, n), lambda i, j: (i, 0))
    full = lambda a: pl.BlockSpec(a.shape, lambda i, j: (0,) * a.ndim)
    row = lambda v: v.reshape(1, -1)
    consts = (row(norm_ffn), row(norm_ple), w_ple_gate, w_ple_proj, row(norm_final))
    n_e = n_exp // ec
    assert peer_vt.shape == (n_e, d, ec)
    hs = lambda dt: pltpu.VMEM((n_heads, N_KEYS, tt), dt)
    n_split = 2
    u_map = lambda s, i, j: (jnp.minimum(j, n_e - 1) * n_split + s, 0)
    vt_map = lambda s, i, j: (jnp.clip(j - 2, 0, n_e - 1), s, 0)
    return pl.pallas_call(
        functools.partial(_peer_kernel, n_heads=n_heads, n_e=n_e, n_split=n_split),
        grid=(t // tt, n_e + 2),
        in_specs=[rows(d), rows(q.shape[1]), rows(p.shape[1]), full(peer_k1), full(peer_k2),
                  *[pl.BlockSpec((ec // n_split, d), functools.partial(u_map, s)) for s in range(n_split)],
                  *[pl.BlockSpec((1, d // n_split, ec), functools.partial(vt_map, s)) for s in range(n_split)]]
                 + [full(a) for a in consts],
        out_specs=rows(d),
        out_shape=jax.ShapeDtypeStruct((t, d), F32),
        scratch_shapes=[pltpu.VMEM((d, tt), BF16), hs(F32), hs(F32), hs(BF16), hs(BF16),
                        pltpu.VMEM((ec, tt), F32), pltpu.VMEM((ec, tt), F32),
                        pltpu.VMEM((ec, tt), BF16), pltpu.VMEM((ec, tt), BF16), pltpu.VMEM((d, tt), F32)],
        compiler_params=_cparams(2),
        name="peer_ple_final",
    )(x2, q, p, peer_k1, peer_k2, *([peer_u] * n_split), *([peer_vt] * n_split), *consts)


def _as_column(x_row):
    n = x_row.shape[1]
    eye = lax.broadcasted_iota(jnp.int32, (n, n), 0) == lax.broadcasted_iota(jnp.int32, (n, n), 1)
    return jnp.sum(jnp.where(eye, jnp.broadcast_to(x_row, (n, n)), 0.0), axis=1, keepdims=True)


def _rows8(x_row):
    return jnp.broadcast_to(x_row, (8, x_row.shape[1]))


def _ssd_step_kernel(xbc_ref, z_ref, dt_ref, conv_ref, h_ref, convw_ref, convb_ref, dtb_ref, alog_ref, dskip_ref,
                     normg_ref, e_ref, y_ref, ho_ref, *, n_heads):
    d_inner = n_heads * HEAD
    gw = d_inner // SSM_GROUPS
    e = e_ref[...]
    for bi in range(xbc_ref.shape[0]):
        cs = conv_ref[bi]
        conv = convb_ref[...] + xbc_ref[bi] * convw_ref[CONV_WIDTH - 1:CONV_WIDTH, :]
        for k in range(CONV_WIDTH - 1):
            conv = conv + cs[k:k + 1, :] * convw_ref[k:k + 1, :]
        act = conv * _sigmoid(conv)
        dt = _softplus(dt_ref[bi] + dtb_ref[...])
        da = jnp.exp(dt * -jnp.exp(alog_ref[...]))
        dt_x = _dot_exact_rhs(_rows8(dt), e)[0:1]
        da_x = _dot_exact_rhs(_rows8(da), e)[0:1]
        for g in range(SSM_GROUPS):
            cols = slice(g * gw, (g + 1) * gw)
            xg = act[:, cols]
            bg = act[:, d_inner + g * SSM_STATE:d_inner + (g + 1) * SSM_STATE]
            cg = act[:, d_inner + (SSM_GROUPS + g) * SSM_STATE:d_inner + (SSM_GROUPS + g + 1) * SSM_STATE]
            hn = _as_column(da_x[:, cols]) * h_ref[bi, cols, :] + _as_column(xg * dt_x[:, cols]) * bg
            ho_ref[bi, cols, :] = hn
            y = _dot_nt(_rows8(cg).astype(BF16), hn.astype(BF16))[0:1] + xg * dskip_ref[:, cols]
            zg = z_ref[bi, :, cols]
            y = y * (zg * _sigmoid(zg))
            y_ref[bi, :, cols] = _rms(y, normg_ref[:, cols])


def ssd_step(u, state_conv, state_ssm, conv_w, conv_b, dt_bias, a_log, d_skip, ssm_norm, n_heads):
    b = u.shape[0]
    d_inner = n_heads * HEAD
    conv_dim = d_inner + 2 * SSM_GROUPS * SSM_STATE
    pad = lambda v: jnp.pad(v.reshape(1, -1), ((0, 0), (0, LANES - v.size)))
    e_bf = jnp.asarray(np.arange(d_inner)[None, :] // HEAD == np.arange(LANES)[:, None], BF16)
    full = lambda shape: pl.BlockSpec(shape, lambda i: (0,) * len(shape))
    u3 = u.reshape(b, 1, -1)
    nb = STEP_SEQS if b % STEP_SEQS == 0 else 1
    y, h = pl.pallas_call(
        functools.partial(_ssd_step_kernel, n_heads=n_heads),
        grid=(b // nb,),
        in_specs=[pl.BlockSpec((nb, 1, conv_dim), lambda i: (i, 0, 0)),
                  pl.BlockSpec((nb, 1, d_inner), lambda i: (i, 0, conv_dim // d_inner)),
                  pl.BlockSpec((nb, 1, LANES), lambda i: (i, 0, (conv_dim + d_inner) // LANES)),
                  pl.BlockSpec((nb, CONV_WIDTH - 1, conv_dim), lambda i: (i, 0, 0)),
                  pl.BlockSpec((nb, d_inner, SSM_STATE), lambda i: (i, 0, 0)),
                  full((CONV_WIDTH, conv_dim)), full((1, conv_dim)), full((1, LANES)), full((1, LANES)),
                  full((1, d_inner)), full((1, d_inner)), full((LANES, d_inner))],
        out_specs=[pl.BlockSpec((nb, 1, d_inner), lambda i: (i, 0, 0)),
                   pl.BlockSpec((nb, d_inner, SSM_STATE), lambda i: (i, 0, 0))],
        out_shape=[jax.ShapeDtypeStruct((b, 1, d_inner), F32),
                   jax.ShapeDtypeStruct((b, d_inner, SSM_STATE), F32)],
        compiler_params=_cparams(1),
        name="ssd_step",
    )(u3, u3, u3, state_conv, state_ssm.reshape(b, d_inner, SSM_STATE), conv_w, conv_b.reshape(1, -1),
      pad(dt_bias), pad(a_log), jnp.repeat(d_skip, HEAD).reshape(1, -1), ssm_norm.reshape(1, -1), e_bf)
    return y.reshape(b, d_inner), h


def _rwkv_step_features_kernel(p_ref, prev_ref, mu_ref, w0_ref, w2_ref, a0_ref, a2_ref, g2_ref, kk_ref, ka_ref,
                               ones2_ref, o_ref, *, dim):
    r, k, v, kkf, a, lw, g = _rwkv_features(p_ref[...], prev_ref[...], mu_ref, w0_ref, w2_ref, a0_ref, a2_ref,
                                            g2_ref, kk_ref, ka_ref, dim)
    for j in range(dim // LANES):
        cols = slice(j * LANES, (j + 1) * LANES)
        kj = kkf[:, cols]
        o_ref[3, :, cols] = kj * lax.rsqrt(_dot_exact_rhs(kj * kj, ones2_ref[...]) + 1e-12)
    o_ref[0], o_ref[1], o_ref[2], o_ref[4], o_ref[5], o_ref[6] = r, k, v, a, jnp.exp(lw), g


def _rwkv_step_kernel(f_ref, s_ref, rk_ref, lng_ref, lnb_ref, y_ref, so_ref, y_s):
    n_heads = s_ref.shape[1]
    eye = lax.broadcasted_iota(jnp.int32, (HEAD, HEAD), 0) == lax.broadcasted_iota(jnp.int32, (HEAD, HEAD), 1)
    for bi in range(s_ref.shape[0]):
        for h in range(n_heads):
            row = lambda i: f_ref[i, bi, h:h + 1, :]
            r, k, v, kk, a, w = (row(i) for i in range(6))
            s = s_ref[bi, h]
            sa = jnp.sum(s * -kk, axis=1, keepdims=True)
            v_col = jnp.sum(jnp.where(eye, jnp.broadcast_to(v, (HEAD, HEAD)), 0.0), axis=1, keepdims=True)
            sn = s * w + sa * (kk * a) + v_col * k
            so_ref[bi, h] = sn
            y_s[bi, h:h + 1, :] = _dot_nt(_rows8(r).astype(BF16), sn.astype(BF16))[0:1]
        y = y_s[bi]
        r, k, v, g = f_ref[0, bi], f_ref[1, bi], f_ref[2, bi], f_ref[6, bi]
        d = y - jnp.mean(y, axis=-1, keepdims=True)
        var = jnp.mean(d * d, axis=-1, keepdims=True)
        yn = d * lax.rsqrt(var + LN_X_EPS) * lng_ref[...] + lnb_ref[...]
        bonus = jnp.sum(r * k * rk_ref[...], axis=-1, keepdims=True) * v
        y_ref[bi] = (yn + bonus) * g


def rwkv_step(proj, shift_prev, state_wkv, shift_mu, decay_w0, decay_w2, aaa_a0, aaa_a2, gate_g2, k_k, k_a, r_k,
              lnx_g, lnx_b):
    b, sd = proj.shape
    dim = decay_w0.size
    n_heads = dim // HEAD
    dl, al, gl = decay_w2.shape[0], aaa_a2.shape[0], gate_g2.shape[0]
    assert dl + al == LANES and gl == LANES and sd == 3 * dim + 2 * LANES
    w2 = jnp.concatenate([decay_w2, jnp.zeros((al, dim), F32)], axis=0).astype(BF16)
    a2 = jnp.concatenate([jnp.zeros((dl, dim), F32), aaa_a2], axis=0).astype(BF16)
    ones2 = jnp.asarray(np.arange(LANES)[:, None] // HEAD == np.arange(LANES)[None, :] // HEAD, BF16)
    row = lambda v: v.reshape(1, -1)
    args = (proj, shift_prev, row(shift_mu), row(decay_w0), w2, row(aaa_a0), a2, gate_g2.astype(BF16), row(k_k),
            row(k_a), ones2)
    feats = pl.pallas_call(
        functools.partial(_rwkv_step_features_kernel, dim=dim),
        grid=(1,),
        in_specs=[pl.BlockSpec(a.shape, lambda i: (0, 0)) for a in args],
        out_specs=pl.BlockSpec((7, b, dim), lambda i: (0, 0, 0)),
        out_shape=jax.ShapeDtypeStruct((7, b, dim), F32),
        compiler_params=_cparams(1),
        name="rwkv_step_features",
    )(*args)
    hv = lambda v: v.reshape(n_heads, HEAD)
    full = pl.BlockSpec((n_heads, HEAD), lambda i: (0, 0))
    nb = STEP_SEQS if b % STEP_SEQS == 0 else 1
    y, s = pl.pallas_call(
        _rwkv_step_kernel,
        grid=(b // nb,),
        in_specs=[pl.BlockSpec((7, nb, n_heads, HEAD), lambda i: (0, i, 0, 0)),
                  pl.BlockSpec((nb, n_heads, HEAD, HEAD), lambda i: (i, 0, 0, 0)), full, full, full],
        out_specs=[pl.BlockSpec((nb, n_heads, HEAD), lambda i: (i, 0, 0)),
                   pl.BlockSpec((nb, n_heads, HEAD, HEAD), lambda i: (i, 0, 0, 0))],
        out_shape=[jax.ShapeDtypeStruct((b, n_heads, HEAD), F32),
                   jax.ShapeDtypeStruct((b, n_heads, HEAD, HEAD), F32)],
        scratch_shapes=[pltpu.VMEM((nb, n_heads, HEAD), F32)],
        compiler_params=_cparams(1),
        name="rwkv_step",
    )(feats.reshape(7, b, n_heads, HEAD), state_wkv, hv(r_k), hv(lnx_g), hv(lnx_b))
    return y.reshape(b, dim), s


def _layer(x, p, states, wts, n_ssm_heads):
    b, l, d = x.shape
    xt = x.reshape(b * l, d)
    g_mix = wts['norm_mix'].reshape(1, -1)
    u_ssm, u_rwkv, u_gate = (norm_matmul(xt, g_mix, wts[k], PROJ_ROWS, _proj_col_tile(wts[k].shape[1]))
                             for k in ('w_ssm', 'w_shift', 'w_gates'))
    d_inner = n_ssm_heads * HEAD
    conv_dim = d_inner + 2 * SSM_GROUPS * SSM_STATE
    ssd_w = (wts['conv_w'], wts['conv_b'], wts['dt_bias'], wts['a_log'], wts['d_skip'], wts['ssm_norm'])
    rwkv_w = tuple(wts[k] for k in ('shift_mu', 'decay_w0', 'decay_w2', 'aaa_a0', 'aaa_a2', 'gate_g2', 'k_k', 'k_a',
                                    'r_k', 'lnx_g', 'lnx_b'))
    if states is None:
        ym, ssm_new = ssd_prompt(u_ssm.reshape(b, l, -1), *ssd_w, n_ssm_heads)
        ym = ym.reshape(b * l, d_inner)
        conv_new = u_ssm.reshape(b, l, -1)[:, l - (CONV_WIDTH - 1):, :conv_dim]
        yr, wkv_new = rwkv_prompt(u_rwkv.reshape(b, l, -1), *rwkv_w)
        yr = yr.reshape(b * l, -1)
        shift_new = u_rwkv.reshape(b, l, -1)[:, l - 1]
    else:
        conv_prev, ssm_prev, wkv_prev, shift_prev = states
        ym, ssm_new = ssd_step(u_ssm, conv_prev, ssm_prev, *ssd_w, n_ssm_heads)
        conv_new = jnp.concatenate([conv_prev[:, 1:], u_ssm[:, None, :conv_dim]], axis=1)
        yr, wkv_new = rwkv_step(u_rwkv, shift_prev, wkv_prev, *rwkv_w)
        shift_new = u_rwkv
    x2, q = mix_and_query(xt, ym, yr, u_gate, wts['w_out_ssm'], wts['w_out_rwkv'], wts['w_out'], wts['norm_ffn'],
                          wts['peer_wq'], MIX_ROWS)
    y = peer_ple_final(x2, q, p.reshape(b * l, -1), wts['peer_k1'], wts['peer_k2'], wts['peer_u'], wts['peer_vt'],
                       wts['norm_ffn'], wts['norm_ple'], wts['w_ple_gate'], wts['w_ple_proj'], wts['norm_final'],
                       PEER_ROWS, PEER_CHUNK)
    return (y.reshape(b, l, d), ssm_new.reshape(b, n_ssm_heads, HEAD, SSM_STATE), conv_new, wkv_new, shift_new)


def kernel(x_prompt, x_sample, p_prompt, p_sample, state_ssm, state_conv, state_wkv, state_shift, norm_mix, w_in,
           conv_w, conv_b, dt_bias, a_log, d_skip, ssm_norm, w_out_ssm, shift_mu, decay_w0, decay_w2, aaa_a0, aaa_a2,
           gate_g2, k_k, k_a, r_k, lnx_g, lnx_b, w_out_rwkv, w_out, norm_ffn, peer_wq, peer_k1, peer_k2, peer_u,
           peer_v, norm_ple, w_ple_gate, w_ple_proj, norm_final):
    depth = w_in.shape[0]
    assert depth == 1, "single-layer trunk"
    d_model = x_prompt.shape[-1]
    n_ssm_heads = dt_bias.shape[1]
    d_inner = n_ssm_heads * HEAD
    conv_dim = conv_w.shape[2]
    shift_dim = shift_mu.shape[1]
    bf = lambda a: a.astype(BF16)
    o = np.cumsum([0, d_inner, conv_dim, n_ssm_heads, shift_dim, d_model, d_model])
    wi = w_in[0]
    wts = {
        'w_ssm': bf(jnp.concatenate([wi[:, o[1]:o[2]], wi[:, o[0]:o[1]], wi[:, o[2]:o[3]],
                                     jnp.zeros((d_model, LANES - n_ssm_heads), F32)], axis=1)),
        'w_shift': bf(wi[:, o[3]:o[4]]),
        'w_gates': bf(wi[:, o[4]:o[6]]),
        'w_out_ssm': bf(w_out_ssm[0]), 'w_out_rwkv': bf(w_out_rwkv[0]), 'w_out': bf(w_out[0]),
        'peer_wq': bf(peer_wq[0]), 'peer_k1': bf(peer_k1[0]), 'peer_k2': bf(peer_k2[0]),
        'peer_u': bf(peer_u[0]),
        'peer_vt': bf(peer_v[0]).reshape(-1, PEER_CHUNK, d_model).transpose(0, 2, 1),
        'w_ple_gate': bf(w_ple_gate[0]), 'w_ple_proj': bf(w_ple_proj[0]), 'norm_final': norm_final,
    }
    for name, val in (('norm_mix', norm_mix), ('conv_w', conv_w), ('conv_b', conv_b), ('dt_bias', dt_bias),
                      ('a_log', a_log), ('d_skip', d_skip), ('ssm_norm', ssm_norm), ('shift_mu', shift_mu),
                      ('decay_w0', decay_w0), ('decay_w2', decay_w2), ('aaa_a0', aaa_a0), ('aaa_a2', aaa_a2),
                      ('gate_g2', gate_g2), ('k_k', k_k), ('k_a', k_a), ('r_k', r_k), ('lnx_g', lnx_g),
                      ('lnx_b', lnx_b), ('norm_ffn', norm_ffn), ('norm_ple', norm_ple)):
        wts[name] = val[0]
    yp, ssm_p, conv_p, wkv_p, shift_p = _layer(x_prompt, p_prompt[0], None, wts, n_ssm_heads)
    ys, ssm_s, conv_s, wkv_s, shift_s = _layer(
        x_sample, p_sample[0], (state_conv[0], state_ssm[0], state_wkv[0], state_shift[0]), wts, n_ssm_heads)
    return (yp, ys, ssm_p[None], conv_p[None], wkv_p[None], shift_p[None],
            ssm_s[None], conv_s[None], wkv_s[None], shift_s[None])
```
